```python
import jax, jax.numpy as jnp
from jax import lax
import numpy as np

D_MODEL = 1024
BATCH = 2
SEQ = 8192
DEPTH = 4

N_MIXERS = 2
HEAD_DIM = 64
MIX_WIDTH = D_MODEL
MEM_HEADS = 4
MEM_WIDTH = MEM_HEADS * HEAD_DIM
MIXER_WIDTH = MIX_WIDTH - MEM_WIDTH
SB_HEADS = MIXER_WIDTH // HEAD_DIM
CONV_WIDTH = 31
N_MEM = 256
N_EXPERTS = 32
TOP_K = 4
D_FF = D_MODEL
SWIGLU_LIMIT = 7.0
SWIGLU_ALPHA = 1.702
Q_BLOCK = 128
EXPERT_BLOCK = 128
LN_EPS = 1e-5
DN_ALPHA = (2 * DEPTH) ** 0.25
DN_BETA = (8 * DEPTH) ** -0.25
N_CONV_LAYERS = (DEPTH + 1) // 2
N_SB_LAYERS = DEPTH // 2

kernel_name = "hybrid_conv_stickbreak_memxattn_moe_deepnorm"


def layer_norm(x, g, b):
    xf = x.astype(jnp.float32)
    mu = jnp.mean(xf, axis=-1, keepdims=True)
    var = jnp.mean(jnp.square(xf - mu), axis=-1, keepdims=True)
    return ((xf - mu) * lax.rsqrt(var + LN_EPS) * g.astype(jnp.float32) + b.astype(jnp.float32)).astype(x.dtype)


def conformer_conv(u, dw_w, dw_b, ln_g, ln_b):
    a, gate = jnp.split(u, 2, axis=-1)
    h = a * jax.nn.sigmoid(gate)
    c = h.shape[-1]
    h = lax.conv_general_dilated(
        h, dw_w[:, None, :].astype(h.dtype), window_strides=(1,),
        padding=[(CONV_WIDTH - 1, 0)],
        dimension_numbers=('NWC', 'WIO', 'NWC'), feature_group_count=c) + dw_b
    return jax.nn.silu(layer_norm(h, ln_g, ln_b))


def stick_breaking_attention(q, k, v):
    b, s, h, dh = q.shape
    n_blk = s // Q_BLOCK
    scale = dh ** -0.5
    key_pos = jnp.arange(s)
    q_blocks = q.reshape(b, n_blk, Q_BLOCK, h, dh).transpose(1, 0, 2, 3, 4)

    def one_block(args):
        q_blk, blk = args
        q_pos = blk * Q_BLOCK + jnp.arange(Q_BLOCK)
        z = jnp.einsum('bqhd,bkhd->bhqk', q_blk, k, preferred_element_type=jnp.float32) * scale
        causal = key_pos[None, :] < q_pos[:, None]
        log_not = jnp.where(causal, jax.nn.log_sigmoid(-z), 0.0)
        tail = lax.cumsum(log_not, axis=3, reverse=True) - log_not
        w = jnp.where(causal, jnp.exp(jax.nn.log_sigmoid(z) + tail), 0.0)
        return jnp.einsum('bhqk,bkhd->bqhd', w.astype(v.dtype), v)

    out = lax.map(one_block, (q_blocks, jnp.arange(n_blk)))
    return out.transpose(1, 0, 2, 3, 4).reshape(b, s, h, dh)


def memory_attention(qm, mem_k, mem_v):
    s = jnp.einsum('bshd,bmhd->bhsm', qm, mem_k, preferred_element_type=jnp.float32) * (HEAD_DIM ** -0.5)
    p = jax.nn.softmax(s, axis=-1)
    return jnp.einsum('bhsm,bmhd->bshd', p.astype(mem_v.dtype), mem_v)


def moe_ffn(h, w_router, b_router, w_gate_up, b_gate_up, w_down, b_down):
    b, s, d = h.shape
    n_tok = b * s
    xt = h.reshape(n_tok, d)
    logits = jnp.matmul(xt, w_router, preferred_element_type=jnp.float32) + b_router.astype(jnp.float32)
    top_vals, top_idx = lax.top_k(logits, TOP_K)
    gates = jax.nn.softmax(top_vals, axis=-1)
    n_asg = n_tok * TOP_K
    expert_of = top_idx.reshape(n_asg).astype(jnp.int32)
    token_of = jnp.arange(n_asg, dtype=jnp.int32) // TOP_K
    order = jnp.argsort(expert_of, stable=True)
    sorted_e = expert_of[order]
    counts = jnp.bincount(expert_of, length=N_EXPERTS).astype(jnp.int32)
    starts = jnp.cumsum(counts) - counts
    padded = (counts + EXPERT_BLOCK - 1) // EXPERT_BLOCK * EXPERT_BLOCK
    pad_ends = jnp.cumsum(padded)
    pad_starts = pad_ends - padded
    dest = pad_starts[sorted_e] + jnp.arange(n_asg, dtype=jnp.int32) - starts[sorted_e]
    n_rows = n_asg + N_EXPERTS * EXPERT_BLOCK
    n_blocks = n_rows // EXPERT_BLOCK
    row_token = jnp.full((n_rows,), n_tok, jnp.int32).at[dest].set(token_of[order])
    row_gate = jnp.zeros((n_rows,), jnp.float32).at[dest].set(gates.reshape(n_asg)[order])
    block_expert = jnp.minimum(
        jnp.searchsorted(pad_ends, jnp.arange(n_blocks, dtype=jnp.int32) * EXPERT_BLOCK, side='right'),
        N_EXPERTS - 1).astype(jnp.int32)
    x_pad = jnp.concatenate([xt, jnp.zeros((1, d), xt.dtype)], axis=0)
    xs = x_pad[row_token].reshape(n_blocks, EXPERT_BLOCK, d)

    def run_block(args):
        xb, e = args
        gu = xb @ w_gate_up[e] + b_gate_up[e]
        gate = jnp.minimum(gu[:, :D_FF], SWIGLU_LIMIT)
        up = jnp.clip(gu[:, D_FF:], -SWIGLU_LIMIT, SWIGLU_LIMIT)
        act = (up + 1.0) * gate * jax.nn.sigmoid(SWIGLU_ALPHA * gate)
        return act @ w_down[e] + b_down[e]

    ys = lax.map(run_block, (xs, block_expert)).reshape(n_rows, d)
    ys = ys * row_gate[:, None].astype(ys.dtype)
    out = jnp.zeros((n_tok + 1, d), ys.dtype).at[row_token].add(ys)[:n_tok]
    return out.reshape(b, s, d)


def setup_inputs(seed: int = 0) -> dict:
    key = jax.random.key(seed)
    ks = jax.random.split(key, 24)
    f32 = jnp.float32
    d = D_MODEL

    def nrm(k, shape, scale):
        return jax.random.normal(k, shape, f32) * scale

    return {
        "x": nrm(ks[0], (BATCH, SEQ, d), 1.0),
        "mem": nrm(ks[1], (BATCH, N_MEM, d), 1.0),
        "mem_ln_g": 1.0 + nrm(ks[2], (d,), 0.02),
        "mem_ln_b": nrm(ks[3], (d,), 0.02),
        "w_mem_kv": nrm(ks[4], (d, 2 * MEM_WIDTH), d ** -0.5),
        "w_in_conv": nrm(ks[5], (N_CONV_LAYERS, d, 2 * MIXER_WIDTH + MEM_WIDTH), d ** -0.5),
        "conv_dw_w": nrm(ks[6], (N_CONV_LAYERS, CONV_WIDTH, MIXER_WIDTH), CONV_WIDTH ** -0.5),
        "conv_dw_b": nrm(ks[7], (N_CONV_LAYERS, MIXER_WIDTH), 0.02),
        "conv_ln_g": 1.0 + nrm(ks[8], (N_CONV_LAYERS, MIXER_WIDTH), 0.02),
        "conv_ln_b": nrm(ks[9], (N_CONV_LAYERS, MIXER_WIDTH), 0.02),
        "w_in_sb": nrm(ks[10], (N_SB_LAYERS, d, 3 * MIXER_WIDTH + MEM_WIDTH), d ** -0.5),
        "w_mix_out": nrm(ks[11], (DEPTH, MIX_WIDTH, d), DN_BETA * MIX_WIDTH ** -0.5),
        "ln_mix_g": 1.0 + nrm(ks[12], (DEPTH, d), 0.02),
        "ln_mix_b": nrm(ks[13], (DEPTH, d), 0.02),
        "w_router": nrm(ks[14], (DEPTH, d, N_EXPERTS), d ** -0.5),
        "b_router": nrm(ks[15], (DEPTH, N_EXPERTS), 0.01),
        "w_gate_up": nrm(ks[16], (DEPTH, N_EXPERTS, d, 2 * D_FF), d ** -0.5),
        "b_gate_up": nrm(ks[17], (DEPTH, N_EXPERTS, 2 * D_FF), 0.02),
        "w_down": nrm(ks[18], (DEPTH, N_EXPERTS, D_FF, d), DN_BETA * D_FF ** -0.5),
        "b_down": nrm(ks[19], (DEPTH, N_EXPERTS, d), 0.02),
        "ln_moe_g": 1.0 + nrm(ks[20], (DEPTH, d), 0.02),
        "ln_moe_b": nrm(ks[21], (DEPTH, d), 0.02),
    }


def reference(x, mem, mem_ln_g, mem_ln_b, w_mem_kv, w_in_conv, conv_dw_w, conv_dw_b, conv_ln_g,
              conv_ln_b, w_in_sb, w_mix_out, ln_mix_g, ln_mix_b, w_router, b_router, w_gate_up,
              b_gate_up, w_down, b_down, ln_moe_g, ln_moe_b):
    b, s, _ = x.shape
    mem_kv = layer_norm(mem, mem_ln_g, mem_ln_b) @ w_mem_kv
    mem_k = mem_kv[..., :MEM_WIDTH].reshape(b, -1, MEM_HEADS, HEAD_DIM)
    mem_v = mem_kv[..., MEM_WIDTH:].reshape(b, -1, MEM_HEADS, HEAD_DIM)

    h = x
    for i in range(DEPTH):
        j = i // N_MIXERS
        if i % N_MIXERS == 0:
            u = h @ w_in_conv[j]
            y_mix = conformer_conv(u[..., :2 * MIXER_WIDTH], conv_dw_w[j], conv_dw_b[j],
                                   conv_ln_g[j], conv_ln_b[j])
            qm = u[..., 2 * MIXER_WIDTH:]
        else:
            u = h @ w_in_sb[j]
            q = u[..., :MIXER_WIDTH].reshape(b, s, SB_HEADS, HEAD_DIM)
            k = u[..., MIXER_WIDTH:2 * MIXER_WIDTH].reshape(b, s, SB_HEADS, HEAD_DIM)
            v = u[..., 2 * MIXER_WIDTH:3 * MIXER_WIDTH].reshape(b, s, SB_HEADS, HEAD_DIM)
            y_mix = stick_breaking_attention(q, k, v).reshape(b, s, MIXER_WIDTH)
            qm = u[..., 3 * MIXER_WIDTH:]
        y_mem = memory_attention(qm.reshape(b, s, MEM_HEADS, HEAD_DIM), mem_k, mem_v).reshape(b, s, MEM_WIDTH)
        y = jnp.concatenate([y_mix, y_mem], axis=-1) @ w_mix_out[i]
        h = layer_norm(DN_ALPHA * h + y, ln_mix_g[i], ln_mix_b[i])
        y = moe_ffn(h, w_router[i], b_router[i], w_gate_up[i], b_gate_up[i], w_down[i], b_down[i])
        h = layer_norm(DN_ALPHA * h + y, ln_moe_g[i], ln_moe_b[i])
    return h
```

```python
import functools

import jax
import jax.numpy as jnp
from jax import lax
from jax.experimental import pallas as pl
from jax.experimental.pallas import tpu as pltpu

F32 = jnp.float32
BF16 = jnp.bfloat16
I32 = jnp.int32

HEAD_DIM = 64
MEM_HEADS = 4
MEM_WIDTH = MEM_HEADS * HEAD_DIM
CONV_WIDTH = 31
N_EXPERTS = 32
TOP_K = 4
SWIGLU_LIMIT = 7.0
SWIGLU_ALPHA = 1.702
LN_EPS = 1e-5

LANES = 128
CONV_HALO = 32
CONV_ROWS = 16
SB_SKIP_BELOW = -104.0
VMEM_LIMIT = 48 * 1024 * 1024


def _layer_norm(x, g, b):
    mu = jnp.mean(x, axis=-1, keepdims=True)
    xc = x - mu
    var = jnp.mean(xc * xc, axis=-1, keepdims=True)
    return xc * lax.rsqrt(var + LN_EPS) * g + b


def _sigmoid(x):
    return 1.0 / (1.0 + jnp.exp(-x))


def _tile(n, want):
    t = min(n, want)
    assert n % t == 0, (n, t)
    return t


def _mem_kv_kernel(mem_ref, g_ref, b_ref, w_ref, kx_ref, vx_ref):
    n_mem = mem_ref.shape[0]
    xn = _layer_norm(mem_ref[...], g_ref[...], b_ref[...])
    kv = jnp.dot(xn.astype(BF16), w_ref[...], preferred_element_type=F32)
    k_t = (kv[:, :MEM_WIDTH] * (HEAD_DIM ** -0.5)).T
    v = kv[:, MEM_WIDTH:]
    k_head = lax.broadcasted_iota(I32, (MEM_WIDTH, n_mem), 0) // HEAD_DIM
    v_head = lax.broadcasted_iota(I32, (n_mem, MEM_WIDTH), 1) // HEAD_DIM
    for hd in range(MEM_HEADS):
        kx_ref[:, hd * n_mem:(hd + 1) * n_mem] = jnp.where(k_head == hd, k_t, 0.0).astype(BF16)
        vx_ref[hd * n_mem:(hd + 1) * n_mem, :] = jnp.where(v_head == hd, v, 0.0).astype(BF16)


def _mem_kv(mem, g, b, w_bf):
    bsz, n_mem, d = mem.shape
    return pl.pallas_call(
        _mem_kv_kernel,
        grid=(bsz,),
        in_specs=[
            pl.BlockSpec((None, n_mem, d), lambda i: (i, 0, 0)),
            pl.BlockSpec((1, d), lambda i: (0, 0)),
            pl.BlockSpec((1, d), lambda i: (0, 0)),
            pl.BlockSpec((d, 2 * MEM_WIDTH), lambda i: (0, 0)),
        ],
        out_specs=[
            pl.BlockSpec((None, MEM_WIDTH, MEM_HEADS * n_mem), lambda i: (i, 0, 0)),
            pl.BlockSpec((None, MEM_HEADS * n_mem, MEM_WIDTH), lambda i: (i, 0, 0)),
        ],
        out_shape=[
            jax.ShapeDtypeStruct((bsz, MEM_WIDTH, MEM_HEADS * n_mem), BF16),
            jax.ShapeDtypeStruct((bsz, MEM_HEADS * n_mem, MEM_WIDTH), BF16),
        ],
        name="mem_kv",
    )(mem, g.reshape(1, d), b.reshape(1, d), w_bf)


def _conv_mix_kernel(h_ref, w_ref, dww_ref, dwb_ref, g_ref, b_ref, y_ref, qm_ref, gbuf, *, tm, c):
    @pl.when(pl.program_id(1) == 0)
    def _():
        gbuf[0:CONV_HALO, :] = jnp.zeros((CONV_HALO, c), F32)

    @pl.when(pl.program_id(1) > 0)
    def _():
        gbuf[0:CONV_HALO, :] = gbuf[tm:tm + CONV_HALO, :]

    u = jnp.dot(h_ref[...].astype(BF16), w_ref[...], preferred_element_type=F32)
    gbuf[CONV_HALO:, :] = u[:, :c] * _sigmoid(u[:, c:2 * c])
    qm_ref[...] = u[:, 2 * c:].astype(BF16)

    first_tap = CONV_HALO - (CONV_WIDTH - 1)

    def rows(i, carry):
        r0 = pl.multiple_of(i * CONV_ROWS, CONV_ROWS)
        win = gbuf[pl.ds(r0, CONV_ROWS + CONV_HALO), :]
        acc = jnp.broadcast_to(dwb_ref[...], (CONV_ROWS, c))
        for j in range(CONV_WIDTH):
            acc = acc + win[first_tap + j:first_tap + j + CONV_ROWS, :] * dww_ref[j:j + 1, :]
        yn = _layer_norm(acc, g_ref[...], b_ref[...])
        y_ref[pl.ds(r0, CONV_ROWS), :] = (yn * _sigmoid(yn)).astype(BF16)
        return carry

    lax.fori_loop(0, tm // CONV_ROWS, rows, 0)


def _conv_mix(h, w_bf, dw_w, dw_b, ln_g, ln_b):
    bsz, s, d = h.shape
    c = dw_w.shape[1]
    tm = _tile(s, 512)
    n_out = w_bf.shape[1]
    row = lambda a: a.reshape(1, -1)
    return pl.pallas_call(
        functools.partial(_conv_mix_kernel, tm=tm, c=c),
        grid=(bsz, s // tm),
        in_specs=[
            pl.BlockSpec((None, tm, d), lambda b, i: (b, i, 0)),
            pl.BlockSpec((d, n_out), lambda b, i: (0, 0)),
            pl.BlockSpec((CONV_WIDTH, c), lambda b, i: (0, 0)),
            pl.BlockSpec((1, c), lambda b, i: (0, 0)),
            pl.BlockSpec((1, c), lambda b, i: (0, 0)),
            pl.BlockSpec((1, c), lambda b, i: (0, 0)),
        ],
        out_specs=[
            pl.BlockSpec((None, tm, c), lambda b, i: (b, i, 0)),
            pl.BlockSpec((None, tm, MEM_WIDTH), lambda b, i: (b, i, 0)),
        ],
        out_shape=[
            jax.ShapeDtypeStruct((bsz, s, c), BF16),
            jax.ShapeDtypeStruct((bsz, s, MEM_WIDTH), BF16),
        ],
        scratch_shapes=[pltpu.VMEM((tm + CONV_HALO, c), F32)],
        compiler_params=pltpu.CompilerParams(
            dimension_semantics=("arbitrary", "arbitrary"), vmem_limit_bytes=VMEM_LIMIT),
        name="conv_mix",
    )(h, w_bf, dw_w, row(dw_b), row(ln_g), row(ln_b))


def _sb_proj_kernel(h_ref, w_ref, u_ref, *, c):
    u = jnp.dot(h_ref[...].astype(BF16), w_ref[...], preferred_element_type=F32)
    u_ref[:, :c] = (u[:, :c] * (HEAD_DIM ** -0.5)).astype(BF16)
    u_ref[:, c:] = u[:, c:].astype(BF16)


def _sb_proj(h, w_bf, c):
    bsz, s, d = h.shape
    tm = _tile(s, 512)
    n_out = w_bf.shape[1]
    return pl.pallas_call(
        functools.partial(_sb_proj_kernel, c=c),
        grid=(bsz, s // tm),
        in_specs=[
            pl.BlockSpec((None, tm, d), lambda b, i: (b, i, 0)),
            pl.BlockSpec((d, n_out), lambda b, i: (0, 0)),
        ],
        out_specs=pl.BlockSpec((None, tm, n_out), lambda b, i: (b, i, 0)),
        out_shape=jax.ShapeDtypeStruct((bsz, s, n_out), BF16),
        compiler_params=pltpu.CompilerParams(
            dimension_semantics=("arbitrary", "arbitrary"), vmem_limit_bytes=VMEM_LIMIT),
        name="sb_proj",
    )(h, w_bf)


def _sb_attn_kernel(q_ref, k_ref, v_ref, o_ref, *, tq, tk):
    qi = pl.program_id(2)
    q2 = q_ref[...]
    lane = lax.broadcasted_iota(I32, (tq, LANES), 1)
    q_heads = (jnp.where(lane < HEAD_DIM, q2, jnp.zeros_like(q2)),
               jnp.where(lane >= HEAD_DIM, q2, jnp.zeros_like(q2)))
    later = (lax.broadcasted_iota(I32, (tk, tk), 0) > lax.broadcasted_iota(I32, (tk, tk), 1)).astype(BF16)
    q_pos = qi * tq + lax.broadcasted_iota(I32, (tq, tk), 0)
    k_off = lax.broadcasted_iota(I32, (tq, tk), 1)

    def block(kb, carries, accs, masked):
        start = pl.multiple_of(kb * tk, tk)
        k2 = k_ref[pl.ds(start, tk), :]
        v2 = v_ref[pl.ds(start, tk), :]
        new_c, new_a = [], []
        for hd in range(2):
            z = lax.dot_general(q_heads[hd], k2, (((1,), (1,)), ((), ())), preferred_element_type=F32)
            sp = jnp.maximum(z, 0.0) + jnp.log1p(jnp.exp(-jnp.abs(z)))
            log_not = -sp
            log_beta = z - sp
            if masked:
                causal = (kb * tk + k_off) < q_pos
                log_not = jnp.where(causal, log_not, 0.0)
            hi = log_not.astype(BF16)
            lo = (log_not - hi.astype(F32)).astype(BF16)
            tail = (jnp.dot(hi, later, preferred_element_type=F32)
                    + jnp.dot(lo, later, preferred_element_type=F32) + carries[hd])
            w = jnp.exp(log_beta + tail)
            if masked:
                w = jnp.where(causal, w, 0.0)
            new_a.append(accs[hd] + jnp.dot(w.astype(BF16), v2, preferred_element_type=F32))
            new_c.append(carries[hd] + jnp.sum(log_not, axis=1, keepdims=True))
        return new_c, new_a

    carries = [jnp.zeros((tq, 1), F32)] * 2
    accs = [jnp.zeros((tq, LANES), F32)] * 2
    n_diag = tq // tk
    for j in range(n_diag):
        carries, accs = block(qi * n_diag + (n_diag - 1 - j), carries, accs, True)

    def live(c0, c1):
        return (jnp.maximum(jnp.max(c0), jnp.max(c1)) > SB_SKIP_BELOW).astype(I32)

    def cond(state):
        kb, go = state[0], state[1]
        return jnp.logical_and(kb >= 0, go > 0)

    def body(state):
        kb, _, c0, c1, a0, a1 = state
        (c0, c1), (a0, a1) = block(kb, [c0, c1], [a0, a1], False)
        return kb - 1, live(c0, c1), c0, c1, a0, a1

    state = (qi * n_diag - 1, live(*carries), carries[0], carries[1], accs[0], accs[1])
    state = lax.while_loop(cond, body, state)
    o_ref[...] = jnp.where(lane < HEAD_DIM, state[4], state[5]).astype(BF16)


def _sb_attn(u, c):
    bsz, s, _ = u.shape
    tq = _tile(s, 256)
    tk = _tile(tq, 128)
    n_pairs = c // LANES
    return pl.pallas_call(
        functools.partial(_sb_attn_kernel, tq=tq, tk=tk),
        grid=(bsz, n_pairs, s // tq),
        in_specs=[
            pl.BlockSpec((None, tq, LANES), lambda b, p, i: (b, i, p)),
            pl.BlockSpec((None, s, LANES), lambda b, p, i: (b, 0, n_pairs + p)),
            pl.BlockSpec((None, s, LANES), lambda b, p, i: (b, 0, 2 * n_pairs + p)),
        ],
        out_specs=pl.BlockSpec((None, tq, LANES), lambda b, p, i: (b, i, p)),
        out_shape=jax.ShapeDtypeStruct((bsz, s, c), BF16),
        compiler_params=pltpu.CompilerParams(
            dimension_semantics=("arbitrary", "arbitrary", "arbitrary"), vmem_limit_bytes=VMEM_LIMIT),
        name="sb_attn",
    )(u, u, u)


def _mix_out_kernel(h_ref, y_ref, qm_ref, kx_ref, vx_ref, wo_ref, g_ref, b_ref, wr_ref, br_ref,
                    h1_ref, idx_ref, gate_ref, rank_ref, cnt_ref, carry_ref, *, tm, c, n_mem, alpha):
    @pl.when(jnp.logical_and(pl.program_id(0) == 0, pl.program_id(1) == 0))
    def _():
        carry_ref[...] = jnp.zeros_like(carry_ref)

    s = jnp.dot(qm_ref[...], kx_ref[...], preferred_element_type=F32)
    probs = []
    for hd in range(MEM_HEADS):
        sh = s[:, hd * n_mem:(hd + 1) * n_mem]
        e = jnp.exp(sh - jnp.max(sh, axis=-1, keepdims=True))
        probs.append((e / jnp.sum(e, axis=-1, keepdims=True)).astype(BF16))
    y_mem = jnp.dot(jnp.concatenate(probs, axis=-1), vx_ref[...], preferred_element_type=F32)

    y = (jnp.dot(y_ref[...], wo_ref[0:c, :], preferred_element_type=F32)
         + jnp.dot(y_mem.astype(BF16), wo_ref[c:, :], preferred_element_type=F32))
    h1 = _layer_norm(alpha * h_ref[...] + y, g_ref[...], b_ref[...])
    h1_ref[...] = h1

    logits = lax.dot_general(wr_ref[...], h1.astype(BF16), (((1,), (1,)), ((), ())),
                             preferred_element_type=F32) + br_ref[...]
    e_iota = lax.broadcasted_iota(I32, (N_EXPERTS, tm), 0)
    vals, sels = [], []
    for _ in range(TOP_K):
        m = jnp.max(logits, axis=0, keepdims=True)
        sel = jnp.min(jnp.where(logits == m, e_iota, N_EXPERTS), axis=0, keepdims=True)
        vals.append(m)
        sels.append(sel)
        logits = jnp.where(e_iota == sel, -jnp.inf, logits)
    exps = [jnp.exp(v - vals[0]) for v in vals]
    denom = exps[0] + exps[1] + exps[2] + exps[3]
    gate_ref[...] = jnp.concatenate([e / denom for e in exps], axis=0)
    idx_ref[...] = jnp.concatenate(sels, axis=0)

    onehot = jnp.zeros((N_EXPERTS, tm), F32)
    for sel in sels:
        onehot = onehot + (e_iota == sel).astype(F32)
    earlier = (lax.broadcasted_iota(I32, (tm, tm), 0) < lax.broadcasted_iota(I32, (tm, tm), 1)).astype(BF16)
    carry = carry_ref[...]
    rank_e = jnp.dot(onehot.astype(BF16), earlier, preferred_element_type=F32) + carry
    rank_ref[...] = jnp.concatenate(
        [jnp.sum(jnp.where(e_iota == sel, rank_e, 0.0), axis=0, keepdims=True) for sel in sels],
        axis=0).astype(I32)
    carry = carry + jnp.sum(onehot, axis=1, keepdims=True)
    carry_ref[...] = carry
    cnt_ref[...] = jnp.broadcast_to(carry, cnt_ref.shape).astype(I32)


def _mix_out(h, y_mix, qm_src, qm_block, kx, vx, wo_bf, ln_g, ln_b, wr_t_bf, b_router, alpha):
    bsz, s, d = h.shape
    c = y_mix.shape[-1]
    n_mem = vx.shape[1] // MEM_HEADS
    tm = _tile(s, 512)
    n_s = s // tm
    n_tok = bsz * s
    tok_spec = pl.BlockSpec((TOP_K, tm), lambda b, i: (0, b * n_s + i))
    return pl.pallas_call(
        functools.partial(_mix_out_kernel, tm=tm, c=c, n_mem=n_mem, alpha=alpha),
        grid=(bsz, n_s),
        in_specs=[
            pl.BlockSpec((None, tm, d), lambda b, i: (b, i, 0)),
            pl.BlockSpec((None, tm, c), lambda b, i: (b, i, 0)),
            pl.BlockSpec((None, tm, MEM_WIDTH), lambda b, i: (b, i, qm_block)),
            pl.BlockSpec((None, MEM_WIDTH, MEM_HEADS * n_mem), lambda b, i: (b, 0, 0)),
            pl.BlockSpec((None, MEM_HEADS * n_mem, MEM_WIDTH), lambda b, i: (b, 0, 0)),
            pl.BlockSpec((c + MEM_WIDTH, d), lambda b, i: (0, 0)),
            pl.BlockSpec((1, d), lambda b, i: (0, 0)),
            pl.BlockSpec((1, d), lambda b, i: (0, 0)),
            pl.BlockSpec((N_EXPERTS, d), lambda b, i: (0, 0)),
            pl.BlockSpec((N_EXPERTS, 1), lambda b, i: (0, 0)),
        ],
        out_specs=[
            pl.BlockSpec((None, tm, d), lambda b, i: (b, i, 0)),
            tok_spec, tok_spec, tok_spec,
            pl.BlockSpec((N_EXPERTS, LANES), lambda b, i: (0, 0)),
        ],
        out_shape=[
            jax.ShapeDtypeStruct((bsz, s, d), F32),
            jax.ShapeDtypeStruct((TOP_K, n_tok), I32),
            jax.ShapeDtypeStruct((TOP_K, n_tok), F32),
            jax.ShapeDtypeStruct((TOP_K, n_tok), I32),
            jax.ShapeDtypeStruct((N_EXPERTS, LANES), I32),
        ],
        scratch_shapes=[pltpu.VMEM((N_EXPERTS, 1), F32)],
        compiler_params=pltpu.CompilerParams(
            dimension_semantics=("arbitrary", "arbitrary"), vmem_limit_bytes=VMEM_LIMIT),
        name="mix_out",
    )(h, y_mix, qm_src, kx, vx, wo_bf, ln_g.reshape(1, d), ln_b.reshape(1, d), wr_t_bf,
      b_router.reshape(N_EXPERTS, 1))


def _dispatch_kernel(fill_lo_ref, fill_hi_ref, na_ref, dest_ref, h_ref, xs_ref, zero_ref, sem, fill_sem,
                     *, td, tmb):
    def row_copy(t, d):
        return pltpu.make_async_copy(h_ref.at[pl.ds(t, 1)], xs_ref.at[pl.ds(d, 1)], sem)

    def issue(t, carry):
        for k in range(TOP_K):
            row_copy(t, dest_ref[0, 0, k * td + t]).start()
        return carry

    def drain(t, carry):
        for k in range(TOP_K):
            row_copy(t, dest_ref[0, 0, k * td + t]).wait()
        return carry

    lax.fori_loop(0, td, issue, 0)

    @pl.when(pl.program_id(0) == 0)
    def _():
        zero_ref[...] = jnp.zeros_like(zero_ref)

        def fill_copy(r):
            return pltpu.make_async_copy(zero_ref.at[pl.ds(0, 1)], xs_ref.at[pl.ds(r, 1)], fill_sem)

        def per_expert(e, carry):
            lo, hi = fill_lo_ref[e], fill_hi_ref[e]
            lax.fori_loop(lo, hi, lambda r, cc: (fill_copy(r).start(), cc)[1], 0)
            lax.fori_loop(lo, hi, lambda r, cc: (fill_copy(r).wait(), cc)[1], 0)
            return carry

        lax.fori_loop(0, N_EXPERTS, per_expert, 0)

        def block_copy(blk):
            return pltpu.make_async_copy(zero_ref, xs_ref.at[pl.ds(blk * tmb, tmb)], fill_sem)

        n_blocks = xs_ref.shape[0] // tmb
        lax.fori_loop(na_ref[0], n_blocks, lambda blk, cc: (block_copy(blk).start(), cc)[1], 0)
        lax.fori_loop(na_ref[0], n_blocks, lambda blk, cc: (block_copy(blk).wait(), cc)[1], 0)

    lax.fori_loop(0, td, drain, 0)


def _dispatch(h1_flat, dest, fill_lo, fill_hi, n_active, n_rows, tmb):
    n_tok, d = h1_flat.shape
    td = _tile(n_tok, 512)
    n_t = n_tok // td
    dest_blocks = dest.reshape(TOP_K, n_t, td).transpose(1, 0, 2).reshape(n_t, 1, TOP_K * td)
    return pl.pallas_call(
        functools.partial(_dispatch_kernel, td=td, tmb=tmb),
        grid_spec=pltpu.PrefetchScalarGridSpec(
            num_scalar_prefetch=3,
            grid=(n_t,),
            in_specs=[
                pl.BlockSpec((1, 1, TOP_K * td), lambda i, lo, hi, na: (i, 0, 0), memory_space=pltpu.SMEM),
                pl.BlockSpec((td, d), lambda i, lo, hi, na: (i, 0)),
            ],
            out_specs=pl.BlockSpec(memory_space=pl.ANY),
            scratch_shapes=[pltpu.VMEM((tmb, d), F32), pltpu.SemaphoreType.DMA, pltpu.SemaphoreType.DMA],
        ),
        out_shape=jax.ShapeDtypeStruct((n_rows, d), F32),
        compiler_params=pltpu.CompilerParams(dimension_semantics=("arbitrary",)),
        name="moe_dispatch",
    )(fill_lo, fill_hi, n_active, dest_blocks, h1_flat)


def _expert_kernel(be_ref, na_ref, xs_ref, wgu_ref, bgu_ref, wd_ref, bd_ref, ys_ref, wgu_bf, wd_bf, *, f):
    i = pl.program_id(0)
    n_active = na_ref[0]
    j = jnp.minimum(i, n_active - 1)
    prev = jnp.maximum(j - 1, 0)
    new_expert = jnp.logical_or(i == 0, be_ref[j] != be_ref[prev])

    @pl.when(jnp.logical_and(i < n_active, new_expert))
    def _():
        rows = 128
        for r in range(0, wgu_ref.shape[0], rows):
            wgu_bf[r:r + rows, :] = wgu_ref[r:r + rows, :].astype(BF16)
        for r in range(0, wd_ref.shape[0], rows):
            wd_bf[r:r + rows, :] = wd_ref[r:r + rows, :].astype(BF16)

    @pl.when(i < n_active)
    def _():
        x = xs_ref[...].astype(BF16)
        gu = jnp.dot(x, wgu_bf[...], preferred_element_type=F32) + bgu_ref[...]
        gate = jnp.minimum(gu[:, :f], SWIGLU_LIMIT)
        up = jnp.clip(gu[:, f:], -SWIGLU_LIMIT, SWIGLU_LIMIT)
        act = (up + 1.0) * gate * _sigmoid(SWIGLU_ALPHA * gate)
        ys_ref[...] = jnp.dot(act.astype(BF16), wd_bf[...], preferred_element_type=F32) + bd_ref[...]

    @pl.when(i >= n_active)
    def _():
        ys_ref[...] = jnp.zeros_like(ys_ref)


def _experts(xs, block_expert, n_active, w_gate_up, b_gate_up, w_down, b_down, tmb):
    n_rows, d = xs.shape
    n_e, _, f2 = w_gate_up.shape
    f = f2 // 2
    nb = n_rows // tmb

    def blk(i, be, na):
        return jnp.minimum(i, na[0] - 1)

    return pl.pallas_call(
        functools.partial(_expert_kernel, f=f),
        grid_spec=pltpu.PrefetchScalarGridSpec(
            num_scalar_prefetch=2,
            grid=(nb,),
            in_specs=[
                pl.BlockSpec((tmb, d), lambda i, be, na: (blk(i, be, na), 0)),
                pl.BlockSpec((None, d, f2), lambda i, be, na: (be[blk(i, be, na)], 0, 0)),
                pl.BlockSpec((None, 1, f2), lambda i, be, na: (be[blk(i, be, na)], 0, 0)),
                pl.BlockSpec((None, f, d), lambda i, be, na: (be[blk(i, be, na)], 0, 0)),
                pl.BlockSpec((None, 1, d), lambda i, be, na: (be[blk(i, be, na)], 0, 0)),
            ],
            out_specs=pl.BlockSpec((tmb, d), lambda i, be, na: (i, 0)),
            scratch_shapes=[pltpu.VMEM((d, f2), BF16), pltpu.VMEM((f, d), BF16)],
        ),
        out_shape=jax.ShapeDtypeStruct((n_rows, d), F32),
        compiler_params=pltpu.CompilerParams(
            dimension_semantics=("arbitrary",), vmem_limit_bytes=VMEM_LIMIT),
        name="moe_experts",
    )(block_expert, n_active, xs, w_gate_up, b_gate_up.reshape(n_e, 1, f2), w_down,
      b_down.reshape(n_e, 1, d))


def _combine_kernel(dest_ref, gate_ref, h_ref, ys_ref, g_ref, b_ref, o_ref, buf, sem, *, tc, alpha):
    def row_copy(k, t):
        return pltpu.make_async_copy(ys_ref.at[pl.ds(dest_ref[0, 0, k * tc + t], 1)],
                                     buf.at[k, pl.ds(t, 1)], sem)

    def issue(t, carry):
        for k in range(TOP_K):
            row_copy(k, t).start()
        return carry

    def drain(t, carry):
        for k in range(TOP_K):
            row_copy(k, t).wait()
        return carry

    lax.fori_loop(0, tc, issue, 0)
    lax.fori_loop(0, tc, drain, 0)
    gates = gate_ref[...]
    y = buf[0] * gates[:, 0:1]
    for k in range(1, TOP_K):
        y = y + buf[k] * gates[:, k:k + 1]
    o_ref[...] = _layer_norm(alpha * h_ref[...] + y, g_ref[...], b_ref[...])


def _combine(h1_flat, ys, dest, gates_t, ln_g, ln_b, alpha):
    n_tok, d = h1_flat.shape
    tc = _tile(n_tok, 256)
    n_t = n_tok // tc
    dest_blocks = dest.reshape(TOP_K, n_t, tc).transpose(1, 0, 2).reshape(n_t, 1, TOP_K * tc)
    return pl.pallas_call(
        functools.partial(_combine_kernel, tc=tc, alpha=alpha),
        grid=(n_t,),
        in_specs=[
            pl.BlockSpec((1, 1, TOP_K * tc), lambda i: (i, 0, 0), memory_space=pltpu.SMEM),
            pl.BlockSpec((tc, TOP_K), lambda i: (i, 0)),
            pl.BlockSpec((tc, d), lambda i: (i, 0)),
            pl.BlockSpec(memory_space=pl.ANY),
            pl.BlockSpec((1, d), lambda i: (0, 0)),
            pl.BlockSpec((1, d), lambda i: (0, 0)),
        ],
        out_specs=pl.BlockSpec((tc, d), lambda i: (i, 0)),
        out_shape=jax.ShapeDtypeStruct((n_tok, d), F32),
        scratch_shapes=[pltpu.VMEM((TOP_K, tc, d), F32), pltpu.SemaphoreType.DMA],
        compiler_params=pltpu.CompilerParams(
            dimension_semantics=("arbitrary",), vmem_limit_bytes=VMEM_LIMIT),
        name="moe_combine",
    )(dest_blocks, gates_t, h1_flat, ys, ln_g.reshape(1, d), ln_b.reshape(1, d))


def _moe(h1, idx, gates, rank, counts, w_gate_up, b_gate_up, w_down, b_down, ln_g, ln_b, alpha):
    bsz, s, d = h1.shape
    n_tok = bsz * s
    tmb = 256
    n_rows = n_tok * TOP_K + N_EXPERTS * tmb
    nb = n_rows // tmb

    padded = (counts + tmb - 1) // tmb * tmb
    pad_end = jnp.cumsum(padded).astype(I32)
    pad_start = pad_end - padded
    experts = jnp.arange(N_EXPERTS, dtype=I32)
    dest = rank + jnp.sum(jnp.where(idx[:, :, None] == experts, pad_start, 0), axis=-1).astype(I32)
    block_start = jnp.arange(nb, dtype=I32) * tmb
    block_expert = jnp.minimum(
        jnp.sum((block_start[:, None] >= pad_end[None, :]).astype(I32), axis=1), N_EXPERTS - 1).astype(I32)
    n_active = (pad_end[-1:] // tmb).astype(I32)

    h1_flat = h1.reshape(n_tok, d)
    xs = _dispatch(h1_flat, dest, pad_start + counts, pad_end, n_active, n_rows, tmb)
    ys = _experts(xs, block_expert, n_active, w_gate_up, b_gate_up, w_down, b_down, tmb)
    out = _combine(h1_flat, ys, dest, gates.T, ln_g, ln_b, alpha)
    return out.reshape(bsz, s, d)


@jax.jit
def _forward(x, mem, mem_ln_g, mem_ln_b, w_mem_kv, w_in_conv, conv_dw_w, conv_dw_b, conv_ln_g,
             conv_ln_b, w_in_sb, w_mix_out, ln_mix_g, ln_mix_b, w_router, b_router, w_gate_up,
             b_gate_up, w_down, b_down, ln_moe_g, ln_moe_b):
    depth = w_mix_out.shape[0]
    alpha = float((2 * depth) ** 0.25)
    c = conv_dw_w.shape[-1]
    kx, vx = _mem_kv(mem, mem_ln_g, mem_ln_b, w_mem_kv.astype(BF16))

    h = x
    for i in range(depth):
        j = i // 2
        if i % 2 == 0:
            y_mix, qm = _conv_mix(h, w_in_conv[j].astype(BF16), conv_dw_w[j], conv_dw_b[j],
                                  conv_ln_g[j], conv_ln_b[j])
            qm_src, qm_block = qm, 0
        else:
            u = _sb_proj(h, w_in_sb[j].astype(BF16), c)
            y_mix = _sb_attn(u, c)
            qm_src, qm_block = u, (3 * c) // MEM_WIDTH
        h1, idx, gates, rank, cnt = _mix_out(
            h, y_mix, qm_src, qm_block, kx, vx, w_mix_out[i].astype(BF16), ln_mix_g[i], ln_mix_b[i],
            w_router[i].T.astype(BF16), b_router[i], alpha)
        h = _moe(h1, idx, gates, rank, cnt[:, 0], w_gate_up[i], b_gate_up[i], w_down[i], b_down[i],
                 ln_moe_g[i], ln_moe_b[i], alpha)
    return h


def kernel(x, mem, mem_ln_g, mem_ln_b, w_mem_kv, w_in_conv, conv_dw_w, conv_dw_b, conv_ln_g, conv_ln_b,
           w_in_sb, w_mix_out, ln_mix_g, ln_mix_b, w_router, b_router, w_gate_up, b_gate_up, w_down,
           b_down, ln_moe_g, ln_moe_b):
    return _forward(x, mem, mem_ln_g, mem_ln_b, w_mem_kv, w_in_conv, conv_dw_w, conv_dw_b, conv_ln_g,
                    conv_ln_b, w_in_sb, w_mix_out, ln_mix_g, ln_mix_b, w_router, b_router, w_gate_up,
                    b_gate_up, w_down, b_down, ln_moe_g, ln_moe_b)
```

```python
import functools

import jax
import jax.numpy as jnp
from jax import lax
from jax.experimental import pallas as pl
from jax.experimental.pallas import tpu as pltpu

F32 = jnp.float32
BF16 = jnp.bfloat16
I32 = jnp.int32

HEAD_DIM = 64
MEM_HEADS = 4
MEM_WIDTH = MEM_HEADS * HEAD_DIM
CONV_WIDTH = 31
N_EXPERTS = 32
TOP_K = 4
SWIGLU_LIMIT = 7.0
SWIGLU_ALPHA = 1.702
LN_EPS = 1e-5

LANES = 128
CONV_HALO = 32
CONV_STRIDE = 4
CONV_ROWS = 8 * CONV_STRIDE
SB_SKIP_BELOW = -104.0
TOKEN_TILE = 512
EXPERT_ROWS = 256
SORT_ROWS = 256
RUN_ALIGN = 8


def _sorted_rows(tm):
    rows = TOP_K * tm + N_EXPERTS * RUN_ALIGN
    return (rows + SORT_ROWS - 1) // SORT_ROWS * SORT_ROWS
VMEM_LIMIT = 48 * 1024 * 1024
HIGH_HALF = -65536


def _layer_norm(x, g, b):
    mu = jnp.mean(x, axis=-1, keepdims=True)
    xc = x - mu
    var = jnp.mean(xc * xc, axis=-1, keepdims=True)
    return xc * lax.rsqrt(var + LN_EPS) * g + b


def _sigmoid(x):
    return 1.0 / (1.0 + jnp.exp(-x))


def _tile(n, want):
    t = min(n, want)
    assert n % t == 0, (n, t)
    return t


def _pack_halves(x):
    n = x.shape[1] // 2
    lo = lax.bitcast_convert_type(x[:, :n], I32)
    hi = lax.bitcast_convert_type(x[:, n:], I32)
    return jnp.bitwise_or(jnp.bitwise_and(hi, HIGH_HALF), lax.shift_right_logical(lo, 16))


def _unpack_halves(p):
    lo = lax.bitcast_convert_type(lax.shift_left(p, 16), F32)
    hi = lax.bitcast_convert_type(jnp.bitwise_and(p, HIGH_HALF), F32)
    return jnp.concatenate([lo, hi], axis=1).astype(BF16)


def _mem_kv_kernel(mem_ref, g_ref, b_ref, w_ref, kx_ref, vx_ref):
    n_mem = mem_ref.shape[0]
    xn = _layer_norm(mem_ref[...], g_ref[...], b_ref[...])
    kv = jnp.dot(xn.astype(BF16), w_ref[...], preferred_element_type=F32)
    k_t = (kv[:, :MEM_WIDTH] * (HEAD_DIM ** -0.5)).T
    v = kv[:, MEM_WIDTH:]
    k_head = lax.broadcasted_iota(I32, (MEM_WIDTH, n_mem), 0) // HEAD_DIM
    v_head = lax.broadcasted_iota(I32, (n_mem, MEM_WIDTH), 1) // HEAD_DIM
    for hd in range(MEM_HEADS):
        kx_ref[:, hd * n_mem:(hd + 1) * n_mem] = jnp.where(k_head == hd, k_t, 0.0).astype(BF16)
        vx_ref[hd * n_mem:(hd + 1) * n_mem, :] = jnp.where(v_head == hd, v, 0.0).astype(BF16)


def _mem_kv(mem, g, b, w_bf):
    bsz, n_mem, d = mem.shape
    return pl.pallas_call(
        _mem_kv_kernel,
        grid=(bsz,),
        in_specs=[
            pl.BlockSpec((None, n_mem, d), lambda i: (i, 0, 0)),
            pl.BlockSpec((1, d), lambda i: (0, 0)),
            pl.BlockSpec((1, d), lambda i: (0, 0)),
            pl.BlockSpec((d, 2 * MEM_WIDTH), lambda i: (0, 0)),
        ],
        out_specs=[
            pl.BlockSpec((None, MEM_WIDTH, MEM_HEADS * n_mem), lambda i: (i, 0, 0)),
            pl.BlockSpec((None, MEM_HEADS * n_mem, MEM_WIDTH), lambda i: (i, 0, 0)),
        ],
        out_shape=[
            jax.ShapeDtypeStruct((bsz, MEM_WIDTH, MEM_HEADS * n_mem), BF16),
            jax.ShapeDtypeStruct((bsz, MEM_HEADS * n_mem, MEM_WIDTH), BF16),
        ],
        name="mem_kv",
    )(mem, g.reshape(1, d), b.reshape(1, d), w_bf)


def _conv_mix_kernel(h_ref, w_ref, dww_ref, dwb_ref, g_ref, b_ref, y_ref, qm_ref, gbuf, cbuf, *, tm, c):
    n_groups = c // LANES

    @pl.when(pl.program_id(1) == 0)
    def _():
        gbuf[:, 0:CONV_HALO, :] = jnp.zeros((n_groups, CONV_HALO, LANES), F32)

    @pl.when(pl.program_id(1) > 0)
    def _():
        gbuf[:, 0:CONV_HALO, :] = gbuf[:, tm:tm + CONV_HALO, :]

    u = jnp.dot(h_ref[...].astype(BF16), w_ref[...], preferred_element_type=F32)
    glu = u[:, :c] * _sigmoid(u[:, c:2 * c])
    for grp in range(n_groups):
        gbuf[grp, CONV_HALO:, :] = glu[:, grp * LANES:(grp + 1) * LANES]
    qm_ref[...] = u[:, 2 * c:].astype(BF16)

    first_tap = CONV_HALO - (CONV_WIDTH - 1)

    def rows(i, carry):
        r0 = pl.multiple_of(i * CONV_ROWS, CONV_ROWS)
        for grp in range(n_groups):
            src = gbuf.at[grp, pl.ds(r0, CONV_ROWS + CONV_HALO), :]
            dst = cbuf.at[grp, pl.ds(r0, CONV_ROWS), :]
            lanes = slice(grp * LANES, (grp + 1) * LANES)
            taps = [jnp.broadcast_to(dww_ref[j:j + 1, lanes], (8, LANES)) for j in range(CONV_WIDTH)]
            bias = jnp.broadcast_to(dwb_ref[:, lanes], (8, LANES))
            for r in range(CONV_STRIDE):
                acc = bias
                for j in range(CONV_WIDTH):
                    acc = acc + src[pl.ds(first_tap + j + r, 8, stride=CONV_STRIDE), :] * taps[j]
                dst[pl.ds(r, 8, stride=CONV_STRIDE), :] = acc
        conv = jnp.concatenate([cbuf[grp, pl.ds(r0, CONV_ROWS), :] for grp in range(n_groups)], axis=-1)
        yn = _layer_norm(conv, g_ref[...], b_ref[...])
        y_ref[pl.ds(r0, CONV_ROWS), :] = (yn * _sigmoid(yn)).astype(BF16)
        return carry

    lax.fori_loop(0, tm // CONV_ROWS, rows, 0)


def _conv_mix(h, w_bf, dw_w, dw_b, ln_g, ln_b):
    bsz, s, d = h.shape
    c = dw_w.shape[1]
    tm = _tile(s, TOKEN_TILE)
    n_out = w_bf.shape[1]
    row = lambda a: a.reshape(1, -1)
    return pl.pallas_call(
        functools.partial(_conv_mix_kernel, tm=tm, c=c),
        grid=(bsz, s // tm),
        in_specs=[
            pl.BlockSpec((None, tm, d), lambda b, i: (b, i, 0)),
            pl.BlockSpec((d, n_out), lambda b, i: (0, 0)),
            pl.BlockSpec((CONV_WIDTH, c), lambda b, i: (0, 0)),
            pl.BlockSpec((1, c), lambda b, i: (0, 0)),
            pl.BlockSpec((1, c), lambda b, i: (0, 0)),
            pl.BlockSpec((1, c), lambda b, i: (0, 0)),
        ],
        out_specs=[
            pl.BlockSpec((None, tm, c), lambda b, i: (b, i, 0)),
            pl.BlockSpec((None, tm, MEM_WIDTH), lambda b, i: (b, i, 0)),
        ],
        out_shape=[
            jax.ShapeDtypeStruct((bsz, s, c), BF16),
            jax.ShapeDtypeStruct((bsz, s, MEM_WIDTH), BF16),
        ],
        scratch_shapes=[pltpu.VMEM((c // LANES, tm + CONV_HALO, LANES), F32),
                        pltpu.VMEM((c // LANES, tm, LANES), F32)],
        compiler_params=pltpu.CompilerParams(
            dimension_semantics=("arbitrary", "arbitrary"), vmem_limit_bytes=VMEM_LIMIT),
        name="conv_mix",
    )(h, w_bf, dw_w, row(dw_b), row(ln_g), row(ln_b))


def _sb_proj_kernel(h_ref, w_ref, u_ref, *, c):
    u = jnp.dot(h_ref[...].astype(BF16), w_ref[...], preferred_element_type=F32)
    u_ref[:, :c] = (u[:, :c] * (HEAD_DIM ** -0.5)).astype(BF16)
    u_ref[:, c:] = u[:, c:].astype(BF16)


def _sb_proj(h, w_bf, c):
    bsz, s, d = h.shape
    tm = _tile(s, TOKEN_TILE)
    n_out = w_bf.shape[1]
    return pl.pallas_call(
        functools.partial(_sb_proj_kernel, c=c),
        grid=(bsz, s // tm),
        in_specs=[
            pl.BlockSpec((None, tm, d), lambda b, i: (b, i, 0)),
            pl.BlockSpec((d, n_out), lambda b, i: (0, 0)),
        ],
        out_specs=pl.BlockSpec((None, tm, n_out), lambda b, i: (b, i, 0)),
        out_shape=jax.ShapeDtypeStruct((bsz, s, n_out), BF16),
        compiler_params=pltpu.CompilerParams(
            dimension_semantics=("arbitrary", "arbitrary"), vmem_limit_bytes=VMEM_LIMIT),
        name="sb_proj",
    )(h, w_bf)


def _sb_attn_kernel(q_ref, k_ref, v_ref, o_ref, *, tq, tk):
    qi = pl.program_id(2)
    q2 = q_ref[...]
    lane = lax.broadcasted_iota(I32, (tq, LANES), 1)
    q_heads = (jnp.where(lane < HEAD_DIM, q2, jnp.zeros_like(q2)),
               jnp.where(lane >= HEAD_DIM, q2, jnp.zeros_like(q2)))
    later = (lax.broadcasted_iota(I32, (tk, tk), 0) > lax.broadcasted_iota(I32, (tk, tk), 1)).astype(BF16)
    q_pos = qi * tq + lax.broadcasted_iota(I32, (tq, tk), 0)
    k_off = lax.broadcasted_iota(I32, (tq, tk), 1)

    def block(kb, carries, accs, masked):
        start = pl.multiple_of(kb * tk, tk)
        k2 = k_ref[pl.ds(start, tk), :]
        v2 = v_ref[pl.ds(start, tk), :]
        new_c, new_a = [], []
        for hd in range(2):
            z = lax.dot_general(q_heads[hd], k2, (((1,), (1,)), ((), ())), preferred_element_type=F32)
            sp = jnp.maximum(z, 0.0) + jnp.log1p(jnp.exp(-jnp.abs(z)))
            log_not = -sp
            log_beta = z - sp
            if masked:
                causal = (kb * tk + k_off) < q_pos
                log_not = jnp.where(causal, log_not, 0.0)
            hi = log_not.astype(BF16)
            lo = (log_not - hi.astype(F32)).astype(BF16)
            tail = (jnp.dot(hi, later, preferred_element_type=F32)
                    + jnp.dot(lo, later, preferred_element_type=F32) + carries[hd])
            w = jnp.exp(log_beta + tail)
            if masked:
                w = jnp.where(causal, w, 0.0)
            new_a.append(accs[hd] + jnp.dot(w.astype(BF16), v2, preferred_element_type=F32))
            new_c.append(carries[hd] + jnp.sum(log_not, axis=1, keepdims=True))
        return new_c, new_a

    carries = [jnp.zeros((tq, 1), F32)] * 2
    accs = [jnp.zeros((tq, LANES), F32)] * 2
    n_diag = tq // tk
    for j in range(n_diag):
        carries, accs = block(qi * n_diag + (n_diag - 1 - j), carries, accs, True)

    def live(c0, c1):
        return (jnp.maximum(jnp.max(c0), jnp.max(c1)) > SB_SKIP_BELOW).astype(I32)

    def cond(state):
        kb, go = state[0], state[1]
        return jnp.logical_and(kb >= 0, go > 0)

    def body(state):
        kb, _, c0, c1, a0, a1 = state
        (c0, c1), (a0, a1) = block(kb, [c0, c1], [a0, a1], False)
        return kb - 1, live(c0, c1), c0, c1, a0, a1

    state = (qi * n_diag - 1, live(*carries), carries[0], carries[1], accs[0], accs[1])
    state = lax.while_loop(cond, body, state)
    o_ref[...] = jnp.where(lane < HEAD_DIM, state[4], state[5]).astype(BF16)


def _sb_attn(u, c):
    bsz, s, _ = u.shape
    tq = _tile(s, 256)
    tk = _tile(tq, 128)
    n_pairs = c // LANES
    return pl.pallas_call(
        functools.partial(_sb_attn_kernel, tq=tq, tk=tk),
        grid=(bsz, n_pairs, s // tq),
        in_specs=[
            pl.BlockSpec((None, tq, LANES), lambda b, p, i: (b, i, p)),
            pl.BlockSpec((None, s, LANES), lambda b, p, i: (b, 0, n_pairs + p)),
            pl.BlockSpec((None, s, LANES), lambda b, p, i: (b, 0, 2 * n_pairs + p)),
        ],
        out_specs=pl.BlockSpec((None, tq, LANES), lambda b, p, i: (b, i, p)),
        out_shape=jax.ShapeDtypeStruct((bsz, s, c), BF16),
        compiler_params=pltpu.CompilerParams(
            dimension_semantics=("arbitrary", "arbitrary", "arbitrary"), vmem_limit_bytes=VMEM_LIMIT),
        name="sb_attn",
    )(u, u, u)


def _mix_out_kernel(h_ref, y_ref, qm_ref, kx_ref, vx_ref, wo_ref, g_ref, b_ref, wr_ref, br_ref,
                    h1_ref, idx_ref, gate_ref, rank_ref, cnt_ref, *, tm, c, n_mem, alpha):
    s = jnp.dot(qm_ref[...], kx_ref[...], preferred_element_type=F32)
    probs = []
    for hd in range(MEM_HEADS):
        sh = s[:, hd * n_mem:(hd + 1) * n_mem]
        e = jnp.exp(sh - jnp.max(sh, axis=-1, keepdims=True))
        probs.append((e / jnp.sum(e, axis=-1, keepdims=True)).astype(BF16))
    y_mem = jnp.dot(jnp.concatenate(probs, axis=-1), vx_ref[...], preferred_element_type=F32)

    y = (jnp.dot(y_ref[...], wo_ref[0:c, :], preferred_element_type=F32)
         + jnp.dot(y_mem.astype(BF16), wo_ref[c:, :], preferred_element_type=F32))
    h1 = _layer_norm(alpha * h_ref[...] + y, g_ref[...], b_ref[...])
    h1_ref[...] = h1

    logits = lax.dot_general(wr_ref[...], h1.astype(BF16), (((1,), (1,)), ((), ())),
                             preferred_element_type=F32) + br_ref[...]
    e_iota = lax.broadcasted_iota(I32, (N_EXPERTS, tm), 0)
    vals, sels = [], []
    for _ in range(TOP_K):
        m = jnp.max(logits, axis=0, keepdims=True)
        sel = jnp.min(jnp.where(logits == m, e_iota, N_EXPERTS), axis=0, keepdims=True)
        vals.append(m)
        sels.append(sel)
        logits = jnp.where(e_iota == sel, -jnp.inf, logits)
    exps = [jnp.exp(v - vals[0]) for v in vals]
    denom = exps[0] + exps[1] + exps[2] + exps[3]
    gate_ref[...] = jnp.concatenate([e / denom for e in exps], axis=0)
    idx_ref[...] = jnp.concatenate(sels, axis=0)

    onehot = jnp.zeros((N_EXPERTS, tm), F32)
    for sel in sels:
        onehot = onehot + (e_iota == sel).astype(F32)
    earlier = (lax.broadcasted_iota(I32, (tm, tm), 0) < lax.broadcasted_iota(I32, (tm, tm), 1)).astype(BF16)
    rank_e = jnp.dot(onehot.astype(BF16), earlier, preferred_element_type=F32)
    rank_ref[...] = jnp.concatenate(
        [jnp.sum(jnp.where(e_iota == sel, rank_e, 0.0), axis=0, keepdims=True) for sel in sels],
        axis=0).astype(I32)
    cnt_ref[...] = jnp.sum(onehot, axis=1, keepdims=True).astype(I32)


def _mix_out(h, y_mix, qm_src, qm_block, kx, vx, wo_bf, ln_g, ln_b, wr_t_bf, b_router, alpha):
    bsz, s, d = h.shape
    c = y_mix.shape[-1]
    n_mem = vx.shape[1] // MEM_HEADS
    tm = _tile(s, TOKEN_TILE)
    n_s = s // tm
    n_tok = bsz * s
    tok_spec = pl.BlockSpec((TOP_K, tm), lambda b, i: (0, b * n_s + i))
    return pl.pallas_call(
        functools.partial(_mix_out_kernel, tm=tm, c=c, n_mem=n_mem, alpha=alpha),
        grid=(bsz, n_s),
        in_specs=[
            pl.BlockSpec((None, tm, d), lambda b, i: (b, i, 0)),
            pl.BlockSpec((None, tm, c), lambda b, i: (b, i, 0)),
            pl.BlockSpec((None, tm, MEM_WIDTH), lambda b, i: (b, i, qm_block)),
            pl.BlockSpec((None, MEM_WIDTH, MEM_HEADS * n_mem), lambda b, i: (b, 0, 0)),
            pl.BlockSpec((None, MEM_HEADS * n_mem, MEM_WIDTH), lambda b, i: (b, 0, 0)),
            pl.BlockSpec((c + MEM_WIDTH, d), lambda b, i: (0, 0)),
            pl.BlockSpec((1, d), lambda b, i: (0, 0)),
            pl.BlockSpec((1, d), lambda b, i: (0, 0)),
            pl.BlockSpec((N_EXPERTS, d), lambda b, i: (0, 0)),
            pl.BlockSpec((N_EXPERTS, 1), lambda b, i: (0, 0)),
        ],
        out_specs=[
            pl.BlockSpec((None, tm, d), lambda b, i: (b, i, 0)),
            tok_spec, tok_spec, tok_spec,
            pl.BlockSpec((None, N_EXPERTS, 1), lambda b, i: (b * n_s + i, 0, 0)),
        ],
        out_shape=[
            jax.ShapeDtypeStruct((bsz, s, d), F32),
            jax.ShapeDtypeStruct((TOP_K, n_tok), I32),
            jax.ShapeDtypeStruct((TOP_K, n_tok), F32),
            jax.ShapeDtypeStruct((TOP_K, n_tok), I32),
            jax.ShapeDtypeStruct((bsz * n_s, N_EXPERTS, 1), I32),
        ],
        compiler_params=pltpu.CompilerParams(
            dimension_semantics=("arbitrary", "arbitrary"), vmem_limit_bytes=VMEM_LIMIT),
        name="mix_out",
    )(h, y_mix, qm_src, kx, vx, wo_bf, ln_g.reshape(1, d), ln_b.reshape(1, d), wr_t_bf,
      b_router.reshape(N_EXPERTS, 1))


def _copy_rows(src_ref, src_row, dst_ref, dst_row, count, sem, max_bit, wait):
    for bit in range(max_bit, RUN_ALIGN.bit_length() - 2, -1):
        size = 1 << bit
        off = lax.shift_left(lax.shift_right_logical(count, bit + 1), bit + 1)

        @pl.when(jnp.bitwise_and(lax.shift_right_logical(count, bit), 1) == 1)
        def _():
            cp = pltpu.make_async_copy(src_ref.at[pl.ds(pl.multiple_of(src_row + off, RUN_ALIGN), size)],
                                       dst_ref.at[pl.ds(pl.multiple_of(dst_row + off, RUN_ALIGN), size)], sem)
            if wait:
                cp.wait()
            else:
                cp.start()


def _tile_runs(tile, tc_ref, ls_ref, base_ref, local_ref, global_ref, sem, max_bit, to_global, wait):
    def per_expert(e, carry):
        k = tile * N_EXPERTS + e
        if to_global:
            _copy_rows(local_ref, ls_ref[k], global_ref, base_ref[k], tc_ref[k], sem, max_bit, wait)
        else:
            _copy_rows(global_ref, base_ref[k], local_ref, ls_ref[k], tc_ref[k], sem, max_bit, wait)
        return carry

    lax.fori_loop(0, N_EXPERTS, per_expert, 0)


def _dispatch_kernel(tc_ref, ls_ref, base_ref, fill_lo_ref, fill_hi_ref, na_ref,
                     pos_ref, h_ref, xs_ref, buf, zero_ref, sems, fill_sem, *, tm, tmb):
    t = pl.program_id(0)
    n_t = pl.num_programs(0)
    slot = lax.rem(t, 2)
    max_bit = tm.bit_length() - 1

    def runs(tile, slot_, wait):
        _tile_runs(tile, tc_ref, ls_ref, base_ref, buf.at[slot_], xs_ref, sems.at[slot_], max_bit, True, wait)

    @pl.when(t >= 2)
    def _():
        runs(t - 2, slot, True)

    x = h_ref[...].astype(BF16)
    pos = pos_ref[...]

    def sort_rows(i, carry):
        r0 = pl.multiple_of(i * SORT_ROWS, SORT_ROWS)
        row = r0 + lax.broadcasted_iota(I32, (SORT_ROWS, tm), 0)
        perm = jnp.zeros((SORT_ROWS, tm), F32)
        for k in range(TOP_K):
            perm = jnp.where(row == pos[k:k + 1, :], 1.0, perm)
        rows = jnp.dot(perm.astype(BF16), x, preferred_element_type=F32)
        buf[slot, pl.ds(r0, SORT_ROWS), :] = _pack_halves(rows)
        return carry

    lax.fori_loop(0, _sorted_rows(tm) // SORT_ROWS, sort_rows, 0)
    runs(t, slot, False)

    @pl.when(t == 0)
    def _():
        zero_ref[...] = jnp.zeros_like(zero_ref)
        pad_bit = tmb.bit_length() - 2

        def pad_rows(wait):
            def per_expert(e, carry):
                _copy_rows(zero_ref, 0, xs_ref, fill_lo_ref[e], fill_hi_ref[e] - fill_lo_ref[e],
                           fill_sem, pad_bit, wait)
                return carry
            lax.fori_loop(0, N_EXPERTS, per_expert, 0)

        def block_copy(blk):
            return pltpu.make_async_copy(zero_ref, xs_ref.at[pl.ds(blk * tmb, tmb)], fill_sem)

        n_blocks = xs_ref.shape[0] // tmb
        pad_rows(False)
        lax.fori_loop(na_ref[0], n_blocks, lambda blk, cc: (block_copy(blk).start(), cc)[1], 0)
        pad_rows(True)
        lax.fori_loop(na_ref[0], n_blocks, lambda blk, cc: (block_copy(blk).wait(), cc)[1], 0)

    @pl.when(t == n_t - 1)
    def _():
        @pl.when(t >= 1)
        def _():
            runs(t - 1, 1 - slot, True)
        runs(t, slot, True)


def _dispatch(h1_flat, pos, tables, fill_lo, fill_hi, n_active, n_rows, tm, tmb):
    n_tok, d = h1_flat.shape
    n_t = n_tok // tm
    tc, ls, base = tables
    return pl.pallas_call(
        functools.partial(_dispatch_kernel, tm=tm, tmb=tmb),
        grid_spec=pltpu.PrefetchScalarGridSpec(
            num_scalar_prefetch=6,
            grid=(n_t,),
            in_specs=[
                pl.BlockSpec((TOP_K, tm), lambda i, *_: (0, i)),
                pl.BlockSpec((tm, d), lambda i, *_: (i, 0)),
            ],
            out_specs=pl.BlockSpec(memory_space=pl.ANY),
            scratch_shapes=[
                pltpu.VMEM((2, _sorted_rows(tm), d // 2), I32),
                pltpu.VMEM((tmb, d // 2), I32),
                pltpu.SemaphoreType.DMA((2,)),
                pltpu.SemaphoreType.DMA,
            ],
        ),
        out_shape=jax.ShapeDtypeStruct((n_rows, d // 2), I32),
        compiler_params=pltpu.CompilerParams(
            dimension_semantics=("arbitrary",), vmem_limit_bytes=VMEM_LIMIT),
        name="moe_dispatch",
    )(tc, ls, base, fill_lo, fill_hi, n_active, pos, h1_flat)


def _expert_kernel(be_ref, na_ref, xs_ref, wgu_ref, bgu_ref, wd_ref, bd_ref, ys_ref, wgu_bf, wd_bf, *, f):
    i = pl.program_id(0)
    n_active = na_ref[0]
    j = jnp.minimum(i, n_active - 1)
    prev = jnp.maximum(j - 1, 0)
    new_expert = jnp.logical_or(i == 0, be_ref[j] != be_ref[prev])

    @pl.when(jnp.logical_and(i < n_active, new_expert))
    def _():
        rows = 128
        for r in range(0, wgu_ref.shape[0], rows):
            wgu_bf[r:r + rows, :] = wgu_ref[r:r + rows, :].astype(BF16)
        for r in range(0, wd_ref.shape[0], rows):
            wd_bf[r:r + rows, :] = wd_ref[r:r + rows, :].astype(BF16)

    @pl.when(i < n_active)
    def _():
        x = _unpack_halves(xs_ref[...])
        gu = jnp.dot(x, wgu_bf[...], preferred_element_type=F32) + bgu_ref[...]
        gate = jnp.minimum(gu[:, :f], SWIGLU_LIMIT)
        up = jnp.clip(gu[:, f:], -SWIGLU_LIMIT, SWIGLU_LIMIT)
        act = (up + 1.0) * gate * _sigmoid(SWIGLU_ALPHA * gate)
        y = jnp.dot(act.astype(BF16), wd_bf[...], preferred_element_type=F32) + bd_ref[...]
        ys_ref[...] = _pack_halves(y.astype(BF16).astype(F32))

    @pl.when(i >= n_active)
    def _():
        ys_ref[...] = jnp.zeros_like(ys_ref)


def _experts(xs, block_expert, n_active, w_gate_up, b_gate_up, w_down, b_down, tmb):
    n_rows, half = xs.shape
    n_e, d, f2 = w_gate_up.shape
    f = f2 // 2
    nb = n_rows // tmb

    def blk(i, be, na):
        return jnp.minimum(i, na[0] - 1)

    return pl.pallas_call(
        functools.partial(_expert_kernel, f=f),
        grid_spec=pltpu.PrefetchScalarGridSpec(
            num_scalar_prefetch=2,
            grid=(nb,),
            in_specs=[
                pl.BlockSpec((tmb, half), lambda i, be, na: (blk(i, be, na), 0)),
                pl.BlockSpec((None, d, f2), lambda i, be, na: (be[blk(i, be, na)], 0, 0)),
                pl.BlockSpec((None, 1, f2), lambda i, be, na: (be[blk(i, be, na)], 0, 0)),
                pl.BlockSpec((None, f, d), lambda i, be, na: (be[blk(i, be, na)], 0, 0)),
                pl.BlockSpec((None, 1, d), lambda i, be, na: (be[blk(i, be, na)], 0, 0)),
            ],
            out_specs=pl.BlockSpec((tmb, half), lambda i, be, na: (i, 0)),
            scratch_shapes=[pltpu.VMEM((d, f2), BF16), pltpu.VMEM((f, d), BF16)],
        ),
        out_shape=jax.ShapeDtypeStruct((n_rows, half), I32),
        compiler_params=pltpu.CompilerParams(
            dimension_semantics=("arbitrary",), vmem_limit_bytes=VMEM_LIMIT),
        name="moe_experts",
    )(block_expert, n_active, xs, w_gate_up, b_gate_up.reshape(n_e, 1, f2), w_down,
      b_down.reshape(n_e, 1, d))


def _combine_kernel(tc_ref, ls_ref, base_ref, pos_ref, gate_ref, h_ref, ys_ref, g_ref, b_ref, o_ref,
                    buf, sems, *, tm, alpha):
    t = pl.program_id(0)
    n_t = pl.num_programs(0)
    slot = lax.rem(t, 2)
    max_bit = tm.bit_length() - 1

    def runs(tile, slot_, wait):
        _tile_runs(tile, tc_ref, ls_ref, base_ref, buf.at[slot_], ys_ref, sems.at[slot_], max_bit, False, wait)

    @pl.when(t == 0)
    def _():
        buf[...] = jnp.zeros_like(buf)
        runs(t, slot, False)

    @pl.when(t + 1 < n_t)
    def _():
        runs(t + 1, 1 - slot, False)

    runs(t, slot, True)

    pos = pos_ref[...]
    gates = gate_ref[...]
    col = lax.broadcasted_iota(I32, (tm, _sorted_rows(tm)), 1)
    weights = jnp.zeros((tm, _sorted_rows(tm)), F32)
    for k in range(TOP_K):
        weights = jnp.where(col == pos[:, k:k + 1], gates[:, k:k + 1], weights)
    y = jnp.dot(weights.astype(BF16), _unpack_halves(buf[slot]), preferred_element_type=F32)
    o_ref[...] = _layer_norm(alpha * h_ref[...] + y, g_ref[...], b_ref[...])


def _combine(h1_flat, ys, pos_t, gates_t, tables, ln_g, ln_b, alpha, tm):
    n_tok, d = h1_flat.shape
    n_t = n_tok // tm
    tc, ls, base = tables
    return pl.pallas_call(
        functools.partial(_combine_kernel, tm=tm, alpha=alpha),
        grid_spec=pltpu.PrefetchScalarGridSpec(
            num_scalar_prefetch=3,
            grid=(n_t,),
            in_specs=[
                pl.BlockSpec((tm, TOP_K), lambda i, *_: (i, 0)),
                pl.BlockSpec((tm, TOP_K), lambda i, *_: (i, 0)),
                pl.BlockSpec((tm, d), lambda i, *_: (i, 0)),
                pl.BlockSpec(memory_space=pl.ANY),
                pl.BlockSpec((1, d), lambda i, *_: (0, 0)),
                pl.BlockSpec((1, d), lambda i, *_: (0, 0)),
            ],
            out_specs=pl.BlockSpec((tm, d), lambda i, *_: (i, 0)),
            scratch_shapes=[pltpu.VMEM((2, _sorted_rows(tm), d // 2), I32), pltpu.SemaphoreType.DMA((2,))],
        ),
        out_shape=jax.ShapeDtypeStruct((n_tok, d), F32),
        compiler_params=pltpu.CompilerParams(
            dimension_semantics=("arbitrary",), vmem_limit_bytes=VMEM_LIMIT),
        name="moe_combine",
    )(tc, ls, base, pos_t, gates_t, h1_flat, ys, ln_g.reshape(1, d), ln_b.reshape(1, d))


def _moe(h1, idx, gates, rank, tile_counts, layer, w_gate_up, b_gate_up, w_down, b_down, ln_g, ln_b, alpha):
    bsz, s, d = h1.shape
    n_tok = bsz * s
    tm = _tile(s, TOKEN_TILE)
    n_t = n_tok // tm
    tmb = EXPERT_ROWS
    n_rows = n_tok * TOP_K + n_t * N_EXPERTS * (RUN_ALIGN - 1) + N_EXPERTS * tmb
    n_rows = (n_rows + tmb - 1) // tmb * tmb
    nb = n_rows // tmb

    tile_counts = (tile_counts + RUN_ALIGN - 1) // RUN_ALIGN * RUN_ALIGN
    counts = jnp.sum(tile_counts, axis=0)
    padded = (counts + tmb - 1) // tmb * tmb
    pad_end = jnp.cumsum(padded).astype(I32)
    pad_start = pad_end - padded
    base = pad_start[None, :] + jnp.cumsum(tile_counts, axis=0) - tile_counts
    local = jnp.cumsum(tile_counts, axis=1) - tile_counts
    experts = jnp.arange(N_EXPERTS, dtype=I32)
    local_tok = jnp.repeat(local, tm, axis=0)
    pos = rank + jnp.sum(jnp.where(idx[:, :, None] == experts, local_tok[None], 0), axis=-1).astype(I32)
    block_start = jnp.arange(nb, dtype=I32) * tmb
    block_expert = jnp.minimum(
        jnp.sum((block_start[:, None] >= pad_end[None, :]).astype(I32), axis=1), N_EXPERTS - 1).astype(I32)
    n_active = (pad_end[-1:] // tmb).astype(I32)
    tables = (tile_counts.reshape(-1).astype(I32), local.reshape(-1).astype(I32), base.reshape(-1).astype(I32))

    h1_flat = h1.reshape(n_tok, d)
    xs = _dispatch(h1_flat, pos, tables, pad_start + counts, pad_end, n_active, n_rows, tm, tmb)
    merge = lambda w: w.reshape((w.shape[0] * w.shape[1],) + w.shape[2:])
    ys = _experts(xs, block_expert + layer * N_EXPERTS, n_active, merge(w_gate_up), merge(b_gate_up),
                  merge(w_down), merge(b_down), tmb)
    out = _combine(h1_flat, ys, pos.T, gates.T, tables, ln_g, ln_b, alpha, tm)
    return out.reshape(bsz, s, d)


@jax.jit
def _forward(x, mem, mem_ln_g, mem_ln_b, w_mem_kv, w_in_conv, conv_dw_w, conv_dw_b, conv_ln_g,
             conv_ln_b, w_in_sb, w_mix_out, ln_mix_g, ln_mix_b, w_router, b_router, w_gate_up,
             b_gate_up, w_down, b_down, ln_moe_g, ln_moe_b):
    depth = w_mix_out.shape[0]
    alpha = float((2 * depth) ** 0.25)
    c = conv_dw_w.shape[-1]
    kx, vx = _mem_kv(mem, mem_ln_g, mem_ln_b, w_mem_kv.astype(BF16))

    h = x
    for i in range(depth):
        j = i // 2
        if i % 2 == 0:
            y_mix, qm = _conv_mix(h, w_in_conv[j].astype(BF16), conv_dw_w[j], conv_dw_b[j],
                                  conv_ln_g[j], conv_ln_b[j])
            qm_src, qm_block = qm, 0
        else:
            u = _sb_proj(h, w_in_sb[j].astype(BF16), c)
            y_mix = _sb_attn(u, c)
            qm_src, qm_block = u, (3 * c) // MEM_WIDTH
        h1, idx, gates, rank, cnt = _mix_out(
            h, y_mix, qm_src, qm_block, kx, vx, w_mix_out[i].astype(BF16), ln_mix_g[i], ln_mix_b[i],
            w_router[i].T.astype(BF16), b_router[i], alpha)
        h = _moe(h1, idx, gates, rank, cnt[:, :, 0], i, w_gate_up, b_gate_up, w_down, b_down,
                 ln_moe_g[i], ln_moe_b[i], alpha)
    return h


def kernel(x, mem, mem_ln_g, mem_ln_b, w_mem_kv, w_in_conv, conv_dw_w, conv_dw_b, conv_ln_g, conv_ln_b,
           w_in_sb, w_mix_out, ln_mix_g, ln_mix_b, w_router, b_router, w_gate_up, b_gate_up, w_down,
           b_down, ln_moe_g, ln_moe_b):
    return _forward(x, mem, mem_ln_g, mem_ln_b, w_mem_kv, w_in_conv, conv_dw_w, conv_dw_b, conv_ln_g,
                    conv_ln_b, w_in_sb, w_mix_out, ln_mix_g, ln_mix_b, w_router, b_router, w_gate_up,
                    b_gate_up, w_down, b_down, ln_moe_g, ln_moe_b)
```

```python
import functools

import jax
import jax.numpy as jnp
from jax import lax
from jax.experimental import pallas as pl
from jax.experimental.pallas import tpu as pltpu

F32 = jnp.float32
BF16 = jnp.bfloat16
I32 = jnp.int32

HEAD_DIM = 64
MEM_HEADS = 4
MEM_WIDTH = MEM_HEADS * HEAD_DIM
CONV_WIDTH = 31
N_EXPERTS = 32
TOP_K = 4
SWIGLU_LIMIT = 7.0
SWIGLU_ALPHA = 1.702
LN_EPS = 1e-5

LANES = 128
CONV_HALO = 32
CONV_STRIDE = 4
CONV_ROWS = 8 * CONV_STRIDE
SB_SKIP_BELOW = -104.0
SB_PAIRS_PER_STEP = 3
TOKEN_TILE = 512
EXPERT_ROWS = 256
SORT_ROWS = 256
RUN_ALIGN = 8


def _sorted_rows(tm):
    rows = TOP_K * tm + N_EXPERTS * RUN_ALIGN
    return (rows + SORT_ROWS - 1) // SORT_ROWS * SORT_ROWS
WEIGHT_DMA_PARTS = 4
VMEM_LIMIT = 48 * 1024 * 1024
HIGH_HALF = -65536


def _layer_norm(x, g, b):
    mu = jnp.mean(x, axis=-1, keepdims=True)
    xc = x - mu
    var = jnp.mean(xc * xc, axis=-1, keepdims=True)
    return xc * lax.rsqrt(var + LN_EPS) * g + b


def _sigmoid(x):
    return 1.0 / (1.0 + jnp.exp(-x))


def _tile(n, want):
    t = min(n, want)
    assert n % t == 0, (n, t)
    return t


def _pack_halves(x):
    n = x.shape[1] // 2
    lo = lax.bitcast_convert_type(x[:, :n], I32)
    hi = lax.bitcast_convert_type(x[:, n:], I32)
    return jnp.bitwise_or(jnp.bitwise_and(hi, HIGH_HALF), lax.shift_right_logical(lo, 16))


def _unpack_halves(p):
    lo = lax.bitcast_convert_type(lax.shift_left(p, 16), F32)
    hi = lax.bitcast_convert_type(jnp.bitwise_and(p, HIGH_HALF), F32)
    return jnp.concatenate([lo, hi], axis=1).astype(BF16)


def _mem_kv_kernel(mem_ref, g_ref, b_ref, w_ref, kx_ref, vx_ref):
    n_mem = mem_ref.shape[0]
    xn = _layer_norm(mem_ref[...], g_ref[...], b_ref[...])
    kv = jnp.dot(xn.astype(BF16), w_ref[...], preferred_element_type=F32)
    k_t = (kv[:, :MEM_WIDTH] * (HEAD_DIM ** -0.5)).T
    v = kv[:, MEM_WIDTH:]
    k_head = lax.broadcasted_iota(I32, (MEM_WIDTH, n_mem), 0) // HEAD_DIM
    v_head = lax.broadcasted_iota(I32, (n_mem, MEM_WIDTH), 1) // HEAD_DIM
    for hd in range(MEM_HEADS):
        kx_ref[:, hd * n_mem:(hd + 1) * n_mem] = jnp.where(k_head == hd, k_t, 0.0).astype(BF16)
        vx_ref[hd * n_mem:(hd + 1) * n_mem, :] = jnp.where(v_head == hd, v, 0.0).astype(BF16)


def _mem_kv(mem, g, b, w_bf):
    bsz, n_mem, d = mem.shape
    return pl.pallas_call(
        _mem_kv_kernel,
        grid=(bsz,),
        in_specs=[
            pl.BlockSpec((None, n_mem, d), lambda i: (i, 0, 0)),
            pl.BlockSpec((1, d), lambda i: (0, 0)),
            pl.BlockSpec((1, d), lambda i: (0, 0)),
            pl.BlockSpec((d, 2 * MEM_WIDTH), lambda i: (0, 0)),
        ],
        out_specs=[
            pl.BlockSpec((None, MEM_WIDTH, MEM_HEADS * n_mem), lambda i: (i, 0, 0)),
            pl.BlockSpec((None, MEM_HEADS * n_mem, MEM_WIDTH), lambda i: (i, 0, 0)),
        ],
        out_shape=[
            jax.ShapeDtypeStruct((bsz, MEM_WIDTH, MEM_HEADS * n_mem), BF16),
            jax.ShapeDtypeStruct((bsz, MEM_HEADS * n_mem, MEM_WIDTH), BF16),
        ],
        name="mem_kv",
    )(mem, g.reshape(1, d), b.reshape(1, d), w_bf)


def _conv_mix_kernel(h_ref, w_ref, dww_ref, dwb_ref, g_ref, b_ref, y_ref, qm_ref, gbuf, cbuf, *, tm, c):
    n_groups = c // LANES

    @pl.when(pl.program_id(1) == 0)
    def _():
        gbuf[:, 0:CONV_HALO, :] = jnp.zeros((n_groups, CONV_HALO, LANES), F32)

    @pl.when(pl.program_id(1) > 0)
    def _():
        gbuf[:, 0:CONV_HALO, :] = gbuf[:, tm:tm + CONV_HALO, :]

    u = jnp.dot(h_ref[...].astype(BF16), w_ref[...], preferred_element_type=F32)
    glu = u[:, :c] * _sigmoid(u[:, c:2 * c])
    for grp in range(n_groups):
        gbuf[grp, CONV_HALO:, :] = glu[:, grp * LANES:(grp + 1) * LANES]
    qm_ref[...] = u[:, 2 * c:].astype(BF16)

    first_tap = CONV_HALO - (CONV_WIDTH - 1)

    def rows(i, carry):
        r0 = pl.multiple_of(i * CONV_ROWS, CONV_ROWS)
        for grp in range(n_groups):
            src = gbuf.at[grp, pl.ds(r0, CONV_ROWS + CONV_HALO), :]
            dst = cbuf.at[grp, pl.ds(r0, CONV_ROWS), :]
            lanes = slice(grp * LANES, (grp + 1) * LANES)
            taps = [jnp.broadcast_to(dww_ref[j:j + 1, lanes], (8, LANES)) for j in range(CONV_WIDTH)]
            bias = jnp.broadcast_to(dwb_ref[:, lanes], (8, LANES))
            for r in range(CONV_STRIDE):
                acc = bias
                for j in range(CONV_WIDTH):
                    acc = acc + src[pl.ds(first_tap + j + r, 8, stride=CONV_STRIDE), :] * taps[j]
                dst[pl.ds(r, 8, stride=CONV_STRIDE), :] = acc
        conv = jnp.concatenate([cbuf[grp, pl.ds(r0, CONV_ROWS), :] for grp in range(n_groups)], axis=-1)
        yn = _layer_norm(conv, g_ref[...], b_ref[...])
        y_ref[pl.ds(r0, CONV_ROWS), :] = (yn * _sigmoid(yn)).astype(BF16)
        return carry

    lax.fori_loop(0, tm // CONV_ROWS, rows, 0)


def _conv_mix(h, w_bf, dw_w, dw_b, ln_g, ln_b):
    bsz, s, d = h.shape
    c = dw_w.shape[1]
    tm = _tile(s, TOKEN_TILE)
    n_out = w_bf.shape[1]
    row = lambda a: a.reshape(1, -1)
    return pl.pallas_call(
        functools.partial(_conv_mix_kernel, tm=tm, c=c),
        grid=(bsz, s // tm),
        in_specs=[
            pl.BlockSpec((None, tm, d), lambda b, i: (b, i, 0)),
            pl.BlockSpec((d, n_out), lambda b, i: (0, 0)),
            pl.BlockSpec((CONV_WIDTH, c), lambda b, i: (0, 0)),
            pl.BlockSpec((1, c), lambda b, i: (0, 0)),
            pl.BlockSpec((1, c), lambda b, i: (0, 0)),
            pl.BlockSpec((1, c), lambda b, i: (0, 0)),
        ],
        out_specs=[
            pl.BlockSpec((None, tm, c), lambda b, i: (b, i, 0)),
            pl.BlockSpec((None, tm, MEM_WIDTH), lambda b, i: (b, i, 0)),
        ],
        out_shape=[
            jax.ShapeDtypeStruct((bsz, s, c), BF16),
            jax.ShapeDtypeStruct((bsz, s, MEM_WIDTH), BF16),
        ],
        scratch_shapes=[pltpu.VMEM((c // LANES, tm + CONV_HALO, LANES), F32),
                        pltpu.VMEM((c // LANES, tm, LANES), F32)],
        compiler_params=pltpu.CompilerParams(
            dimension_semantics=("arbitrary", "arbitrary"), vmem_limit_bytes=VMEM_LIMIT),
        name="conv_mix",
    )(h, w_bf, dw_w, row(dw_b), row(ln_g), row(ln_b))


def _sb_proj_kernel(h_ref, w_ref, u_ref, *, c):
    u = jnp.dot(h_ref[...].astype(BF16), w_ref[...], preferred_element_type=F32)
    u_ref[:, :c] = (u[:, :c] * (HEAD_DIM ** -0.5)).astype(BF16)
    u_ref[:, c:] = u[:, c:].astype(BF16)


def _sb_proj(h, w_bf, c):
    bsz, s, d = h.shape
    tm = _tile(s, TOKEN_TILE)
    n_out = w_bf.shape[1]
    return pl.pallas_call(
        functools.partial(_sb_proj_kernel, c=c),
        grid=(bsz, s // tm),
        in_specs=[
            pl.BlockSpec((None, tm, d), lambda b, i: (b, i, 0)),
            pl.BlockSpec((d, n_out), lambda b, i: (0, 0)),
        ],
        out_specs=pl.BlockSpec((None, tm, n_out), lambda b, i: (b, i, 0)),
        out_shape=jax.ShapeDtypeStruct((bsz, s, n_out), BF16),
        compiler_params=pltpu.CompilerParams(
            dimension_semantics=("arbitrary", "arbitrary"), vmem_limit_bytes=VMEM_LIMIT),
        name="sb_proj",
    )(h, w_bf)


def _sb_attn_kernel(q_ref, k_ref, v_ref, o_ref, *, tq, tk, n_pairs):
    qi = pl.program_id(2)
    n_heads = 2 * n_pairs
    lane = lax.broadcasted_iota(I32, (tq, LANES), 1)
    k_lane = lax.broadcasted_iota(I32, (tk, LANES), 1)
    q_heads = []
    for pr in range(n_pairs):
        q2 = q_ref[:, pr * LANES:(pr + 1) * LANES]
        q_heads += [jnp.where(lane < HEAD_DIM, q2, jnp.zeros_like(q2)),
                    jnp.where(lane >= HEAD_DIM, q2, jnp.zeros_like(q2))]
    later = (lax.broadcasted_iota(I32, (tk, tk), 0) > lax.broadcasted_iota(I32, (tk, tk), 1)).astype(BF16)
    q_pos = qi * tq + lax.broadcasted_iota(I32, (tq, tk), 0)
    k_off = lax.broadcasted_iota(I32, (tq, tk), 1)

    def block(kb, carries, accs, masked):
        start = pl.multiple_of(kb * tk, tk)
        new_c, new_a = [], []
        for pr in range(n_pairs):
            k2 = k_ref[pl.ds(start, tk), pr * LANES:(pr + 1) * LANES]
            v2 = v_ref[pl.ds(start, tk), pr * LANES:(pr + 1) * LANES]
            acc = accs[pr]
            for half in range(2):
                hd = 2 * pr + half
                z = lax.dot_general(q_heads[hd], k2, (((1,), (1,)), ((), ())), preferred_element_type=F32)
                sp = jnp.maximum(z, 0.0) + jnp.log1p(jnp.exp(-jnp.abs(z)))
                log_not = -sp
                log_beta = z - sp
                if masked:
                    causal = (kb * tk + k_off) < q_pos
                    log_not = jnp.where(causal, log_not, 0.0)
                hi = log_not.astype(BF16)
                lo = (log_not - hi.astype(F32)).astype(BF16)
                tail = (jnp.dot(hi, later, preferred_element_type=F32)
                        + jnp.dot(lo, later, preferred_element_type=F32) + carries[hd])
                w = jnp.exp(log_beta + tail)
                if masked:
                    w = jnp.where(causal, w, 0.0)
                in_head = (k_lane < HEAD_DIM) if half == 0 else (k_lane >= HEAD_DIM)
                v_head = jnp.where(in_head, v2, jnp.zeros_like(v2))
                acc = acc + jnp.dot(w.astype(BF16), v_head, preferred_element_type=F32)
                new_c.append(carries[hd] + jnp.sum(log_not, axis=1, keepdims=True))
            new_a.append(acc)
        return new_c, new_a

    carries = [jnp.zeros((tq, 1), F32)] * n_heads
    accs = [jnp.zeros((tq, LANES), F32)] * n_pairs
    n_diag = tq // tk
    for j in range(n_diag):
        carries, accs = block(qi * n_diag + (n_diag - 1 - j), carries, accs, True)

    def live(cs):
        top = cs[0]
        for cc in cs[1:]:
            top = jnp.maximum(top, cc)
        return (jnp.max(top) > SB_SKIP_BELOW).astype(I32)

    def cond(state):
        kb, go = state[0], state[1]
        return jnp.logical_and(kb >= 0, go > 0)

    def body(state):
        kb = state[0]
        cs, acs = block(kb, list(state[2:2 + n_heads]), list(state[2 + n_heads:]), False)
        return (kb - 1, live(cs), *cs, *acs)

    state = lax.while_loop(cond, body, (qi * n_diag - 1, live(carries), *carries, *accs))
    o_ref[...] = jnp.concatenate(state[2 + n_heads:], axis=1).astype(BF16)


def _sb_attn(u, c):
    bsz, s, _ = u.shape
    tq = _tile(s, 256)
    tk = _tile(tq, 128)
    width = SB_PAIRS_PER_STEP * LANES
    n_groups = c // width
    return pl.pallas_call(
        functools.partial(_sb_attn_kernel, tq=tq, tk=tk, n_pairs=SB_PAIRS_PER_STEP),
        grid=(bsz, n_groups, s // tq),
        in_specs=[
            pl.BlockSpec((None, tq, width), lambda b, p, i: (b, i, p)),
            pl.BlockSpec((None, s, width), lambda b, p, i: (b, 0, n_groups + p)),
            pl.BlockSpec((None, s, width), lambda b, p, i: (b, 0, 2 * n_groups + p)),
        ],
        out_specs=pl.BlockSpec((None, tq, width), lambda b, p, i: (b, i, p)),
        out_shape=jax.ShapeDtypeStruct((bsz, s, c), BF16),
        compiler_params=pltpu.CompilerParams(
            dimension_semantics=("arbitrary", "arbitrary", "arbitrary"), vmem_limit_bytes=VMEM_LIMIT),
        name="sb_attn",
    )(u, u, u)


def _mix_out_kernel(h_ref, y_ref, qm_ref, kx_ref, vx_ref, wo_ref, g_ref, b_ref, wr_ref, br_ref,
                    h1_ref, idx_ref, gate_ref, rank_ref, cnt_ref, *, tm, c, n_mem, alpha):
    s = jnp.dot(qm_ref[...], kx_ref[...], preferred_element_type=F32)
    probs = []
    for hd in range(MEM_HEADS):
        sh = s[:, hd * n_mem:(hd + 1) * n_mem]
        e = jnp.exp(sh - jnp.max(sh, axis=-1, keepdims=True))
        probs.append((e / jnp.sum(e, axis=-1, keepdims=True)).astype(BF16))
    y_mem = jnp.dot(jnp.concatenate(probs, axis=-1), vx_ref[...], preferred_element_type=F32)

    y = (jnp.dot(y_ref[...], wo_ref[0:c, :], preferred_element_type=F32)
         + jnp.dot(y_mem.astype(BF16), wo_ref[c:, :], preferred_element_type=F32))
    h1 = _layer_norm(alpha * h_ref[...] + y, g_ref[...], b_ref[...])
    h1_ref[...] = h1

    logits = lax.dot_general(wr_ref[...], h1.astype(BF16), (((1,), (1,)), ((), ())),
                             preferred_element_type=F32) + br_ref[...]
    e_iota = lax.broadcasted_iota(I32, (N_EXPERTS, tm), 0)
    vals, sels = [], []
    for _ in range(TOP_K):
        m = jnp.max(logits, axis=0, keepdims=True)
        sel = jnp.min(jnp.where(logits == m, e_iota, N_EXPERTS), axis=0, keepdims=True)
        vals.append(m)
        sels.append(sel)
        logits = jnp.where(e_iota == sel, -jnp.inf, logits)
    exps = [jnp.exp(v - vals[0]) for v in vals]
    denom = exps[0] + exps[1] + exps[2] + exps[3]
    gate_ref[...] = jnp.concatenate([e / denom for e in exps], axis=0)
    idx_ref[...] = jnp.concatenate(sels, axis=0)

    onehot = jnp.zeros((N_EXPERTS, tm), F32)
    for sel in sels:
        onehot = onehot + (e_iota == sel).astype(F32)
    earlier = (lax.broadcasted_iota(I32, (tm, tm), 0) < lax.broadcasted_iota(I32, (tm, tm), 1)).astype(BF16)
    rank_e = jnp.dot(onehot.astype(BF16), earlier, preferred_element_type=F32)
    rank_ref[...] = jnp.concatenate(
        [jnp.sum(jnp.where(e_iota == sel, rank_e, 0.0), axis=0, keepdims=True) for sel in sels],
        axis=0).astype(I32)
    cnt_ref[...] = jnp.sum(onehot, axis=1, keepdims=True).astype(I32)


def _mix_out(h, y_mix, qm_src, qm_block, kx, vx, wo_bf, ln_g, ln_b, wr_t_bf, b_router, alpha):
    bsz, s, d = h.shape
    c = y_mix.shape[-1]
    n_mem = vx.shape[1] // MEM_HEADS
    tm = _tile(s, TOKEN_TILE)
    n_s = s // tm
    n_tok = bsz * s
    tok_spec = pl.BlockSpec((TOP_K, tm), lambda b, i: (0, b * n_s + i))
    return pl.pallas_call(
        functools.partial(_mix_out_kernel, tm=tm, c=c, n_mem=n_mem, alpha=alpha),
        grid=(bsz, n_s),
        in_specs=[
            pl.BlockSpec((None, tm, d), lambda b, i: (b, i, 0)),
            pl.BlockSpec((None, tm, c), lambda b, i: (b, i, 0)),
            pl.BlockSpec((None, tm, MEM_WIDTH), lambda b, i: (b, i, qm_block)),
            pl.BlockSpec((None, MEM_WIDTH, MEM_HEADS * n_mem), lambda b, i: (b, 0, 0)),
            pl.BlockSpec((None, MEM_HEADS * n_mem, MEM_WIDTH), lambda b, i: (b, 0, 0)),
            pl.BlockSpec((c + MEM_WIDTH, d), lambda b, i: (0, 0)),
            pl.BlockSpec((1, d), lambda b, i: (0, 0)),
            pl.BlockSpec((1, d), lambda b, i: (0, 0)),
            pl.BlockSpec((N_EXPERTS, d), lambda b, i: (0, 0)),
            pl.BlockSpec((N_EXPERTS, 1), lambda b, i: (0, 0)),
        ],
        out_specs=[
            pl.BlockSpec((None, tm, d), lambda b, i: (b, i, 0)),
            tok_spec, tok_spec, tok_spec,
            pl.BlockSpec((None, N_EXPERTS, 1), lambda b, i: (b * n_s + i, 0, 0)),
        ],
        out_shape=[
            jax.ShapeDtypeStruct((bsz, s, d), F32),
            jax.ShapeDtypeStruct((TOP_K, n_tok), I32),
            jax.ShapeDtypeStruct((TOP_K, n_tok), F32),
            jax.ShapeDtypeStruct((TOP_K, n_tok), I32),
            jax.ShapeDtypeStruct((bsz * n_s, N_EXPERTS, 1), I32),
        ],
        compiler_params=pltpu.CompilerParams(
            dimension_semantics=("arbitrary", "arbitrary"), vmem_limit_bytes=VMEM_LIMIT),
        name="mix_out",
    )(h, y_mix, qm_src, kx, vx, wo_bf, ln_g.reshape(1, d), ln_b.reshape(1, d), wr_t_bf,
      b_router.reshape(N_EXPERTS, 1))


def _copy_rows(src_ref, src_row, dst_ref, dst_row, count, sem, max_bit, wait):
    for bit in range(max_bit, RUN_ALIGN.bit_length() - 2, -1):
        size = 1 << bit
        off = lax.shift_left(lax.shift_right_logical(count, bit + 1), bit + 1)

        @pl.when(jnp.bitwise_and(lax.shift_right_logical(count, bit), 1) == 1)
        def _():
            cp = pltpu.make_async_copy(src_ref.at[pl.ds(pl.multiple_of(src_row + off, RUN_ALIGN), size)],
                                       dst_ref.at[pl.ds(pl.multiple_of(dst_row + off, RUN_ALIGN), size)], sem)
            if wait:
                cp.wait()
            else:
                cp.start()


def _tile_runs(tile, tc_ref, ls_ref, base_ref, local_ref, global_ref, sem, max_bit, to_global, wait):
    def per_expert(e, carry):
        k = tile * N_EXPERTS + e
        if to_global:
            _copy_rows(local_ref, ls_ref[k], global_ref, base_ref[k], tc_ref[k], sem, max_bit, wait)
        else:
            _copy_rows(global_ref, base_ref[k], local_ref, ls_ref[k], tc_ref[k], sem, max_bit, wait)
        return carry

    lax.fori_loop(0, N_EXPERTS, per_expert, 0)


def _dispatch_kernel(tc_ref, ls_ref, base_ref, fill_lo_ref, fill_hi_ref, na_ref,
                     pos_ref, h_ref, xs_ref, buf, zero_ref, sems, fill_sem, *, tm, tmb):
    t = pl.program_id(0)
    n_t = pl.num_programs(0)
    slot = lax.rem(t, 2)
    max_bit = tm.bit_length() - 1

    def runs(tile, slot_, wait):
        _tile_runs(tile, tc_ref, ls_ref, base_ref, buf.at[slot_], xs_ref, sems.at[slot_], max_bit, True, wait)

    @pl.when(t >= 2)
    def _():
        runs(t - 2, slot, True)

    x = h_ref[...].astype(BF16)
    pos = pos_ref[...]

    def sort_rows(i, carry):
        r0 = pl.multiple_of(i * SORT_ROWS, SORT_ROWS)
        row = r0 + lax.broadcasted_iota(I32, (SORT_ROWS, tm), 0)
        perm = jnp.zeros((SORT_ROWS, tm), F32)
        for k in range(TOP_K):
            perm = jnp.where(row == pos[k:k + 1, :], 1.0, perm)
        rows = jnp.dot(perm.astype(BF16), x, preferred_element_type=F32)
        buf[slot, pl.ds(r0, SORT_ROWS), :] = _pack_halves(rows)
        return carry

    lax.fori_loop(0, _sorted_rows(tm) // SORT_ROWS, sort_rows, 0)
    runs(t, slot, False)

    @pl.when(t == 0)
    def _():
        zero_ref[...] = jnp.zeros_like(zero_ref)
        pad_bit = tmb.bit_length() - 2

        def pad_rows(wait):
            def per_expert(e, carry):
                _copy_rows(zero_ref, 0, xs_ref, fill_lo_ref[e], fill_hi_ref[e] - fill_lo_ref[e],
                           fill_sem, pad_bit, wait)
                return carry
            lax.fori_loop(0, N_EXPERTS, per_expert, 0)

        def block_copy(blk):
            return pltpu.make_async_copy(zero_ref, xs_ref.at[pl.ds(blk * tmb, tmb)], fill_sem)

        n_blocks = xs_ref.shape[0] // tmb
        pad_rows(False)
        lax.fori_loop(na_ref[0], n_blocks, lambda blk, cc: (block_copy(blk).start(), cc)[1], 0)
        pad_rows(True)
        lax.fori_loop(na_ref[0], n_blocks, lambda blk, cc: (block_copy(blk).wait(), cc)[1], 0)

    @pl.when(t == n_t - 1)
    def _():
        @pl.when(t >= 1)
        def _():
            runs(t - 1, 1 - slot, True)
        runs(t, slot, True)


def _dispatch(h1_flat, pos, tables, fill_lo, fill_hi, n_active, n_rows, tm, tmb):
    n_tok, d = h1_flat.shape
    n_t = n_tok // tm
    tc, ls, base = tables
    return pl.pallas_call(
        functools.partial(_dispatch_kernel, tm=tm, tmb=tmb),
        grid_spec=pltpu.PrefetchScalarGridSpec(
            num_scalar_prefetch=6,
            grid=(n_t,),
            in_specs=[
                pl.BlockSpec((TOP_K, tm), lambda i, *_: (0, i)),
                pl.BlockSpec((tm, d), lambda i, *_: (i, 0)),
            ],
            out_specs=pl.BlockSpec(memory_space=pl.ANY),
            scratch_shapes=[
                pltpu.VMEM((2, _sorted_rows(tm), d // 2), I32),
                pltpu.VMEM((tmb, d // 2), I32),
                pltpu.SemaphoreType.DMA((2,)),
                pltpu.SemaphoreType.DMA,
            ],
        ),
        out_shape=jax.ShapeDtypeStruct((n_rows, d // 2), I32),
        compiler_params=pltpu.CompilerParams(
            dimension_semantics=("arbitrary",), vmem_limit_bytes=VMEM_LIMIT),
        name="moe_dispatch",
    )(tc, ls, base, fill_lo, fill_hi, n_active, pos, h1_flat)


def _expert_kernel(ord_ref, seq_ref, nseq_ref, na_ref, xs_ref, wgu_hbm, bgu_ref, wd_hbm, bd_ref, ys_ref,
                   wgu_f32, wd_f32, wgu_bf, wd_bf, sems, *, f):
    i = pl.program_id(0)
    n_active = na_ref[0]
    j = jnp.minimum(i, n_active - 1)
    prev = jnp.maximum(j - 1, 0)
    o = ord_ref[j]
    new_expert = jnp.logical_or(i == 0, o != ord_ref[prev])

    def fetch(o_, slot, wait):
        e = seq_ref[o_]
        copies = []
        for src, dst, parts in ((wgu_hbm, wgu_f32, WEIGHT_DMA_PARTS), (wd_hbm, wd_f32, WEIGHT_DMA_PARTS // 2)):
            rows = src.shape[1] // parts
            for p in range(parts):
                copies.append(pltpu.make_async_copy(src.at[e, pl.ds(p * rows, rows)],
                                                    dst.at[slot, pl.ds(p * rows, rows)], sems.at[slot]))
        for cp in copies:
            if wait:
                cp.wait()
            else:
                cp.start()

    @pl.when(i == 0)
    def _():
        fetch(0, 0, False)

    @pl.when(jnp.logical_and(i < n_active, new_expert))
    def _():
        slot = lax.rem(o, 2)
        fetch(o, slot, True)

        @pl.when(o + 1 < nseq_ref[0])
        def _():
            fetch(o + 1, 1 - slot, False)

        rows = 128
        for r in range(0, wgu_bf.shape[0], rows):
            wgu_bf[r:r + rows, :] = wgu_f32[slot, r:r + rows, :].astype(BF16)
        for r in range(0, wd_bf.shape[0], rows):
            wd_bf[r:r + rows, :] = wd_f32[slot, r:r + rows, :].astype(BF16)

    @pl.when(i < n_active)
    def _():
        x = _unpack_halves(xs_ref[...])
        gu = jnp.dot(x, wgu_bf[...], preferred_element_type=F32) + bgu_ref[...]
        gate = jnp.minimum(gu[:, :f], SWIGLU_LIMIT)
        up = jnp.clip(gu[:, f:], -SWIGLU_LIMIT, SWIGLU_LIMIT)
        act = (up + 1.0) * gate * _sigmoid(SWIGLU_ALPHA * gate)
        y = jnp.dot(act.astype(BF16), wd_bf[...], preferred_element_type=F32) + bd_ref[...]
        ys_ref[...] = _pack_halves(y.astype(BF16).astype(F32))

    @pl.when(i >= n_active)
    def _():
        ys_ref[...] = jnp.zeros_like(ys_ref)


def _experts(xs, block_ord, expert_seq, n_seq, n_active, w_gate_up, b_gate_up, w_down, b_down, tmb):
    n_rows, half = xs.shape
    n_e, d, f2 = w_gate_up.shape
    f = f2 // 2
    nb = n_rows // tmb

    def blk(i, refs):
        return jnp.minimum(i, refs[3][0] - 1)

    def expert(i, refs):
        return refs[1][refs[0][blk(i, refs)]]

    return pl.pallas_call(
        functools.partial(_expert_kernel, f=f),
        grid_spec=pltpu.PrefetchScalarGridSpec(
            num_scalar_prefetch=4,
            grid=(nb,),
            in_specs=[
                pl.BlockSpec((tmb, half), lambda i, *refs: (blk(i, refs), 0)),
                pl.BlockSpec(memory_space=pl.ANY),
                pl.BlockSpec((None, 1, f2), lambda i, *refs: (expert(i, refs), 0, 0)),
                pl.BlockSpec(memory_space=pl.ANY),
                pl.BlockSpec((None, 1, d), lambda i, *refs: (expert(i, refs), 0, 0)),
            ],
            out_specs=pl.BlockSpec((tmb, half), lambda i, *refs: (i, 0)),
            scratch_shapes=[
                pltpu.VMEM((2, d, f2), F32), pltpu.VMEM((2, f, d), F32),
                pltpu.VMEM((d, f2), BF16), pltpu.VMEM((f, d), BF16),
                pltpu.SemaphoreType.DMA((2,)),
            ],
        ),
        out_shape=jax.ShapeDtypeStruct((n_rows, half), I32),
        compiler_params=pltpu.CompilerParams(
            dimension_semantics=("arbitrary",), vmem_limit_bytes=VMEM_LIMIT),
        name="moe_experts",
    )(block_ord, expert_seq, n_seq, n_active, xs, w_gate_up, b_gate_up.reshape(n_e, 1, f2), w_down,
      b_down.reshape(n_e, 1, d))


def _combine_kernel(tc_ref, ls_ref, base_ref, pos_ref, gate_ref, h_ref, ys_ref, g_ref, b_ref, o_ref,
                    buf, sems, *, tm, alpha):
    t = pl.program_id(0)
    n_t = pl.num_programs(0)
    slot = lax.rem(t, 2)
    max_bit = tm.bit_length() - 1

    def runs(tile, slot_, wait):
        _tile_runs(tile, tc_ref, ls_ref, base_ref, buf.at[slot_], ys_ref, sems.at[slot_], max_bit, False, wait)

    @pl.when(t == 0)
    def _():
        buf[...] = jnp.zeros_like(buf)
        runs(t, slot, False)

    @pl.when(t + 1 < n_t)
    def _():
        runs(t + 1, 1 - slot, False)

    runs(t, slot, True)

    pos = pos_ref[...]
    gates = gate_ref[...]
    col = lax.broadcasted_iota(I32, (tm, _sorted_rows(tm)), 1)
    weights = jnp.zeros((tm, _sorted_rows(tm)), F32)
    for k in range(TOP_K):
        weights = jnp.where(col == pos[:, k:k + 1], gates[:, k:k + 1], weights)
    y = jnp.dot(weights.astype(BF16), _unpack_halves(buf[slot]), preferred_element_type=F32)
    o_ref[...] = _layer_norm(alpha * h_ref[...] + y, g_ref[...], b_ref[...])


def _combine(h1_flat, ys, pos_t, gates_t, tables, ln_g, ln_b, alpha, tm):
    n_tok, d = h1_flat.shape
    n_t = n_tok // tm
    tc, ls, base = tables
    return pl.pallas_call(
        functools.partial(_combine_kernel, tm=tm, alpha=alpha),
        grid_spec=pltpu.PrefetchScalarGridSpec(
            num_scalar_prefetch=3,
            grid=(n_t,),
            in_specs=[
                pl.BlockSpec((tm, TOP_K), lambda i, *_: (i, 0)),
                pl.BlockSpec((tm, TOP_K), lambda i, *_: (i, 0)),
                pl.BlockSpec((tm, d), lambda i, *_: (i, 0)),
                pl.BlockSpec(memory_space=pl.ANY),
                pl.BlockSpec((1, d), lambda i, *_: (0, 0)),
                pl.BlockSpec((1, d), lambda i, *_: (0, 0)),
            ],
            out_specs=pl.BlockSpec((tm, d), lambda i, *_: (i, 0)),
            scratch_shapes=[pltpu.VMEM((2, _sorted_rows(tm), d // 2), I32), pltpu.SemaphoreType.DMA((2,))],
        ),
        out_shape=jax.ShapeDtypeStruct((n_tok, d), F32),
        compiler_params=pltpu.CompilerParams(
            dimension_semantics=("arbitrary",), vmem_limit_bytes=VMEM_LIMIT),
        name="moe_combine",
    )(tc, ls, base, pos_t, gates_t, h1_flat, ys, ln_g.reshape(1, d), ln_b.reshape(1, d))


def _moe(h1, idx, gates, rank, tile_counts, layer, w_gate_up, b_gate_up, w_down, b_down, ln_g, ln_b, alpha):
    bsz, s, d = h1.shape
    n_tok = bsz * s
    tm = _tile(s, TOKEN_TILE)
    n_t = n_tok // tm
    tmb = EXPERT_ROWS
    n_rows = n_tok * TOP_K + n_t * N_EXPERTS * (RUN_ALIGN - 1) + N_EXPERTS * tmb
    n_rows = (n_rows + tmb - 1) // tmb * tmb
    nb = n_rows // tmb

    tile_counts = (tile_counts + RUN_ALIGN - 1) // RUN_ALIGN * RUN_ALIGN
    counts = jnp.sum(tile_counts, axis=0)
    padded = (counts + tmb - 1) // tmb * tmb
    pad_end = jnp.cumsum(padded).astype(I32)
    pad_start = pad_end - padded
    base = pad_start[None, :] + jnp.cumsum(tile_counts, axis=0) - tile_counts
    local = jnp.cumsum(tile_counts, axis=1) - tile_counts
    experts = jnp.arange(N_EXPERTS, dtype=I32)
    local_tok = jnp.repeat(local, tm, axis=0)
    pos = rank + jnp.sum(jnp.where(idx[:, :, None] == experts, local_tok[None], 0), axis=-1).astype(I32)
    block_start = jnp.arange(nb, dtype=I32) * tmb
    block_expert = jnp.minimum(
        jnp.sum((block_start[:, None] >= pad_end[None, :]).astype(I32), axis=1), N_EXPERTS - 1).astype(I32)
    n_active = (pad_end[-1:] // tmb).astype(I32)
    tables = (tile_counts.reshape(-1).astype(I32), local.reshape(-1).astype(I32), base.reshape(-1).astype(I32))

    h1_flat = h1.reshape(n_tok, d)
    xs = _dispatch(h1_flat, pos, tables, pad_start + counts, pad_end, n_active, n_rows, tm, tmb)
    merge = lambda w: w.reshape((w.shape[0] * w.shape[1],) + w.shape[2:])
    present = padded > 0
    expert_ord = jnp.cumsum(present.astype(I32)) - 1
    expert_seq = (jnp.argsort(jnp.logical_not(present), stable=True) + layer * N_EXPERTS).astype(I32)
    n_seq = jnp.sum(present.astype(I32)).reshape(1)
    ys = _experts(xs, expert_ord[block_expert].astype(I32), expert_seq, n_seq, n_active, merge(w_gate_up),
                  merge(b_gate_up), merge(w_down), merge(b_down), tmb)
    out = _combine(h1_flat, ys, pos.T, gates.T, tables, ln_g, ln_b, alpha, tm)
    return out.reshape(bsz, s, d)


@jax.jit
def _forward(x, mem, mem_ln_g, mem_ln_b, w_mem_kv, w_in_conv, conv_dw_w, conv_dw_b, conv_ln_g,
             conv_ln_b, w_in_sb, w_mix_out, ln_mix_g, ln_mix_b, w_router, b_router, w_gate_up,
             b_gate_up, w_down, b_down, ln_moe_g, ln_moe_b):
    depth = w_mix_out.shape[0]
    alpha = float((2 * depth) ** 0.25)
    c = conv_dw_w.shape[-1]
    kx, vx = _mem_kv(mem, mem_ln_g, mem_ln_b, w_mem_kv.astype(BF16))

    h = x
    for i in range(depth):
        j = i // 2
        if i % 2 == 0:
            y_mix, qm = _conv_mix(h, w_in_conv[j].astype(BF16), conv_dw_w[j], conv_dw_b[j],
                                  conv_ln_g[j], conv_ln_b[j])
            qm_src, qm_block = qm, 0
        else:
            u = _sb_proj(h, w_in_sb[j].astype(BF16), c)
            y_mix = _sb_attn(u, c)
            qm_src, qm_block = u, (3 * c) // MEM_WIDTH
        h1, idx, gates, rank, cnt = _mix_out(
            h, y_mix, qm_src, qm_block, kx, vx, w_mix_out[i].astype(BF16), ln_mix_g[i], ln_mix_b[i],
            w_router[i].T.astype(BF16), b_router[i], alpha)
        h = _moe(h1, idx, gates, rank, cnt[:, :, 0], i, w_gate_up, b_gate_up, w_down, b_down,
                 ln_moe_g[i], ln_moe_b[i], alpha)
    return h


def kernel(x, mem, mem_ln_g, mem_ln_b, w_mem_kv, w_in_conv, conv_dw_w, conv_dw_b, conv_ln_g, conv_ln_b,
           w_in_sb, w_mix_out, ln_mix_g, ln_mix_b, w_router, b_router, w_gate_up, b_gate_up, w_down,
           b_down, ln_moe_g, ln_moe_b):
    return _forward(x, mem, mem_ln_g, mem_ln_b, w_mem_kv, w_in_conv, conv_dw_w, conv_dw_b, conv_ln_g,
                    conv_ln_b, w_in_sb, w_mix_out, ln_mix_g, ln_mix_b, w_router, b_router, w_gate_up,
                    b_gate_up, w_down, b_down, ln_moe_g, ln_moe_b)
```

```python
import functools

import jax
import jax.numpy as jnp
from jax import lax
from jax.experimental import pallas as pl
from jax.experimental.pallas import tpu as pltpu

F32 = jnp.float32
BF16 = jnp.bfloat16
I32 = jnp.int32

HEAD_DIM = 64
MEM_HEADS = 4
MEM_WIDTH = MEM_HEADS * HEAD_DIM
CONV_WIDTH = 31
N_EXPERTS = 32
TOP_K = 4
SWIGLU_LIMIT = 7.0
SWIGLU_ALPHA = 1.702
LN_EPS = 1e-5

LANES = 128
CONV_HALO = 32
CONV_STRIDE = 4
CONV_ROWS = 8 * CONV_STRIDE
SB_SKIP_BELOW = -104.0
SB_PAIRS_PER_STEP = 3
SB_ROW_STEP = 64
TOKEN_TILE = 512
EXPERT_ROWS = 1024
EXPERT_SUB = 256
SORT_ROWS = 256
RUN_ALIGN = 8


def _sorted_rows(tm):
    rows = TOP_K * tm + N_EXPERTS * RUN_ALIGN
    return (rows + SORT_ROWS - 1) // SORT_ROWS * SORT_ROWS
WEIGHT_DMA_PARTS = 4
VMEM_LIMIT = 48 * 1024 * 1024
EXPERT_VMEM_LIMIT = 56 * 1024 * 1024
HIGH_HALF = -65536


def _layer_norm(x, g, b):
    mu = jnp.mean(x, axis=-1, keepdims=True)
    xc = x - mu
    var = jnp.mean(xc * xc, axis=-1, keepdims=True)
    return xc * lax.rsqrt(var + LN_EPS) * g + b


def _sigmoid(x):
    return 1.0 / (1.0 + jnp.exp(-x))


def _tile(n, want):
    t = min(n, want)
    assert n % t == 0, (n, t)
    return t


def _pack_halves(x):
    n = x.shape[1] // 2
    lo = lax.bitcast_convert_type(x[:, :n], I32)
    hi = lax.bitcast_convert_type(x[:, n:], I32)
    return jnp.bitwise_or(jnp.bitwise_and(hi, HIGH_HALF), lax.shift_right_logical(lo, 16))


def _unpack_halves(p):
    lo = lax.bitcast_convert_type(lax.shift_left(p, 16), F32)
    hi = lax.bitcast_convert_type(jnp.bitwise_and(p, HIGH_HALF), F32)
    return jnp.concatenate([lo, hi], axis=1).astype(BF16)


def _mem_kv_kernel(mem_ref, g_ref, b_ref, w_ref, kx_ref, vx_ref):
    n_mem = mem_ref.shape[0]
    xn = _layer_norm(mem_ref[...], g_ref[...], b_ref[...])
    kv = jnp.dot(xn.astype(BF16), w_ref[...], preferred_element_type=F32)
    k_t = (kv[:, :MEM_WIDTH] * (HEAD_DIM ** -0.5)).T
    v = kv[:, MEM_WIDTH:]
    k_head = lax.broadcasted_iota(I32, (MEM_WIDTH, n_mem), 0) // HEAD_DIM
    v_head = lax.broadcasted_iota(I32, (n_mem, MEM_WIDTH), 1) // HEAD_DIM
    for hd in range(MEM_HEADS):
        kx_ref[:, hd * n_mem:(hd + 1) * n_mem] = jnp.where(k_head == hd, k_t, 0.0).astype(BF16)
        vx_ref[hd * n_mem:(hd + 1) * n_mem, :] = jnp.where(v_head == hd, v, 0.0).astype(BF16)


def _mem_kv(mem, g, b, w_bf):
    bsz, n_mem, d = mem.shape
    return pl.pallas_call(
        _mem_kv_kernel,
        grid=(bsz,),
        in_specs=[
            pl.BlockSpec((None, n_mem, d), lambda i: (i, 0, 0)),
            pl.BlockSpec((1, d), lambda i: (0, 0)),
            pl.BlockSpec((1, d), lambda i: (0, 0)),
            pl.BlockSpec((d, 2 * MEM_WIDTH), lambda i: (0, 0)),
        ],
        out_specs=[
            pl.BlockSpec((None, MEM_WIDTH, MEM_HEADS * n_mem), lambda i: (i, 0, 0)),
            pl.BlockSpec((None, MEM_HEADS * n_mem, MEM_WIDTH), lambda i: (i, 0, 0)),
        ],
        out_shape=[
            jax.ShapeDtypeStruct((bsz, MEM_WIDTH, MEM_HEADS * n_mem), BF16),
            jax.ShapeDtypeStruct((bsz, MEM_HEADS * n_mem, MEM_WIDTH), BF16),
        ],
        name="mem_kv",
    )(mem, g.reshape(1, d), b.reshape(1, d), w_bf)


def _conv_mix_kernel(h_ref, w_ref, dww_ref, dwb_ref, g_ref, b_ref, y_ref, qm_ref, gbuf, cbuf, *, tm, c):
    n_groups = c // LANES

    @pl.when(pl.program_id(1) == 0)
    def _():
        gbuf[:, 0:CONV_HALO, :] = jnp.zeros((n_groups, CONV_HALO, LANES), F32)

    @pl.when(pl.program_id(1) > 0)
    def _():
        gbuf[:, 0:CONV_HALO, :] = gbuf[:, tm:tm + CONV_HALO, :]

    u = jnp.dot(h_ref[...].astype(BF16), w_ref[...], preferred_element_type=F32)
    glu = u[:, :c] * _sigmoid(u[:, c:2 * c])
    for grp in range(n_groups):
        gbuf[grp, CONV_HALO:, :] = glu[:, grp * LANES:(grp + 1) * LANES]
    qm_ref[...] = u[:, 2 * c:].astype(BF16)

    first_tap = CONV_HALO - (CONV_WIDTH - 1)

    def rows(i, carry):
        r0 = pl.multiple_of(i * CONV_ROWS, CONV_ROWS)
        for grp in range(n_groups):
            src = gbuf.at[grp, pl.ds(r0, CONV_ROWS + CONV_HALO), :]
            dst = cbuf.at[grp, pl.ds(r0, CONV_ROWS), :]
            lanes = slice(grp * LANES, (grp + 1) * LANES)
            taps = [jnp.broadcast_to(dww_ref[j:j + 1, lanes], (8, LANES)) for j in range(CONV_WIDTH)]
            bias = jnp.broadcast_to(dwb_ref[:, lanes], (8, LANES))
            for r in range(CONV_STRIDE):
                acc = bias
                for j in range(CONV_WIDTH):
                    acc = acc + src[pl.ds(first_tap + j + r, 8, stride=CONV_STRIDE), :] * taps[j]
                dst[pl.ds(r, 8, stride=CONV_STRIDE), :] = acc
        conv = jnp.concatenate([cbuf[grp, pl.ds(r0, CONV_ROWS), :] for grp in range(n_groups)], axis=-1)
        yn = _layer_norm(conv, g_ref[...], b_ref[...])
        y_ref[pl.ds(r0, CONV_ROWS), :] = (yn * _sigmoid(yn)).astype(BF16)
        return carry

    lax.fori_loop(0, tm // CONV_ROWS, rows, 0)


def _conv_mix(h, w_bf, dw_w, dw_b, ln_g, ln_b):
    bsz, s, d = h.shape
    c = dw_w.shape[1]
    tm = _tile(s, TOKEN_TILE)
    n_out = w_bf.shape[1]
    row = lambda a: a.reshape(1, -1)
    return pl.pallas_call(
        functools.partial(_conv_mix_kernel, tm=tm, c=c),
        grid=(bsz, s // tm),
        in_specs=[
            pl.BlockSpec((None, tm, d), lambda b, i: (b, i, 0)),
            pl.BlockSpec((d, n_out), lambda b, i: (0, 0)),
            pl.BlockSpec((CONV_WIDTH, c), lambda b, i: (0, 0)),
            pl.BlockSpec((1, c), lambda b, i: (0, 0)),
            pl.BlockSpec((1, c), lambda b, i: (0, 0)),
            pl.BlockSpec((1, c), lambda b, i: (0, 0)),
        ],
        out_specs=[
            pl.BlockSpec((None, tm, c), lambda b, i: (b, i, 0)),
            pl.BlockSpec((None, tm, MEM_WIDTH), lambda b, i: (b, i, 0)),
        ],
        out_shape=[
            jax.ShapeDtypeStruct((bsz, s, c), BF16),
            jax.ShapeDtypeStruct((bsz, s, MEM_WIDTH), BF16),
        ],
        scratch_shapes=[pltpu.VMEM((c // LANES, tm + CONV_HALO, LANES), F32),
                        pltpu.VMEM((c // LANES, tm, LANES), F32)],
        compiler_params=pltpu.CompilerParams(
            dimension_semantics=("arbitrary", "arbitrary"), vmem_limit_bytes=VMEM_LIMIT),
        name="conv_mix",
    )(h, w_bf, dw_w, row(dw_b), row(ln_g), row(ln_b))


def _sb_proj_kernel(h_ref, w_ref, u_ref, *, c):
    u = jnp.dot(h_ref[...].astype(BF16), w_ref[...], preferred_element_type=F32)
    u_ref[:, :c] = (u[:, :c] * (HEAD_DIM ** -0.5)).astype(BF16)
    u_ref[:, c:] = u[:, c:].astype(BF16)


def _sb_proj(h, w_bf, c):
    bsz, s, d = h.shape
    tm = _tile(s, TOKEN_TILE)
    n_out = w_bf.shape[1]
    return pl.pallas_call(
        functools.partial(_sb_proj_kernel, c=c),
        grid=(bsz, s // tm),
        in_specs=[
            pl.BlockSpec((None, tm, d), lambda b, i: (b, i, 0)),
            pl.BlockSpec((d, n_out), lambda b, i: (0, 0)),
        ],
        out_specs=pl.BlockSpec((None, tm, n_out), lambda b, i: (b, i, 0)),
        out_shape=jax.ShapeDtypeStruct((bsz, s, n_out), BF16),
        compiler_params=pltpu.CompilerParams(
            dimension_semantics=("arbitrary", "arbitrary"), vmem_limit_bytes=VMEM_LIMIT),
        name="sb_proj",
    )(h, w_bf)


def _sb_attn_kernel(q_ref, k_ref, v_ref, o_ref, *, tq, tk, n_pairs):
    qi = pl.program_id(2)
    n_heads = 2 * n_pairs
    lane = lax.broadcasted_iota(I32, (tq, LANES), 1)
    k_lane = lax.broadcasted_iota(I32, (tk, LANES), 1)
    q_heads = []
    for pr in range(n_pairs):
        q2 = q_ref[:, pr * LANES:(pr + 1) * LANES]
        q_heads += [jnp.where(lane < HEAD_DIM, q2, jnp.zeros_like(q2)),
                    jnp.where(lane >= HEAD_DIM, q2, jnp.zeros_like(q2))]
    later = (lax.broadcasted_iota(I32, (tk, tk), 0) > lax.broadcasted_iota(I32, (tk, tk), 1)).astype(BF16)

    def block(kb, carries, accs, masked, r0, r1):
        start = pl.multiple_of(kb * tk, tk)
        part = lambda x: x[r0:r1]
        def merge(old, new):
            pieces = ([old[:r0]] if r0 > 0 else []) + [new] + ([old[r1:]] if r1 < tq else [])
            return pieces[0] if len(pieces) == 1 else jnp.concatenate(pieces, axis=0)

        new_c, new_a = [], []
        for pr in range(n_pairs):
            k2 = k_ref[pl.ds(start, tk), pr * LANES:(pr + 1) * LANES]
            v2 = v_ref[pl.ds(start, tk), pr * LANES:(pr + 1) * LANES]
            acc = part(accs[pr])
            for half in range(2):
                hd = 2 * pr + half
                z = lax.dot_general(part(q_heads[hd]), k2, (((1,), (1,)), ((), ())),
                                    preferred_element_type=F32)
                sp = jnp.maximum(z, 0.0) + jnp.log1p(jnp.exp(-jnp.abs(z)))
                log_not = -sp
                log_beta = z - sp
                if masked:
                    causal = ((kb * tk + lax.broadcasted_iota(I32, (r1 - r0, tk), 1))
                              < (qi * tq + r0 + lax.broadcasted_iota(I32, (r1 - r0, tk), 0)))
                    log_not = jnp.where(causal, log_not, 0.0)
                hi = log_not.astype(BF16)
                lo = (log_not - hi.astype(F32)).astype(BF16)
                tail = (jnp.dot(hi, later, preferred_element_type=F32)
                        + jnp.dot(lo, later, preferred_element_type=F32) + part(carries[hd]))
                w = jnp.exp(log_beta + tail)
                if masked:
                    w = jnp.where(causal, w, 0.0)
                in_head = (k_lane < HEAD_DIM) if half == 0 else (k_lane >= HEAD_DIM)
                v_head = jnp.where(in_head, v2, jnp.zeros_like(v2))
                acc = acc + jnp.dot(w.astype(BF16), v_head, preferred_element_type=F32)
                new_c.append(merge(carries[hd], part(carries[hd]) + jnp.sum(log_not, axis=1, keepdims=True)))
            new_a.append(merge(accs[pr], acc))
        return new_c, new_a

    assert tk == LANES
    carries = [jnp.zeros((tq, tk), F32)] * n_heads
    accs = [jnp.zeros((tq, LANES), F32)] * n_pairs
    n_diag = tq // tk
    for j in range(n_diag - 1, -1, -1):
        carries, accs = block(qi * n_diag + j, carries, accs, True, j * tk, tq)

    row_count = 1 + lax.broadcasted_iota(I32, (tq, tk), 0)

    def live_rows(cs):
        top = cs[0]
        for cc in cs[1:]:
            top = jnp.maximum(top, cc)
        return jnp.max(jnp.where(top > SB_SKIP_BELOW, row_count, 0))

    def cond(state):
        kb, n_live = state[0], state[1]
        return jnp.logical_and(kb >= 0, n_live > 0)

    n_steps = tq // SB_ROW_STEP

    def body(state):
        kb, n_live = state[0], state[1]
        cs, acs = list(state[2:2 + n_heads]), list(state[2 + n_heads:])
        variants = [functools.partial(block, kb, cs, acs, False, 0, (v + 1) * SB_ROW_STEP)
                    for v in range(n_steps)]
        flat = lambda fn: (lambda: tuple(x for part_ in fn() for x in part_))
        out = lax.switch((n_live - 1) // SB_ROW_STEP, [flat(fn) for fn in variants])
        cs, acs = list(out[:n_heads]), list(out[n_heads:])
        return (kb - 1, live_rows(cs), *cs, *acs)

    state = lax.while_loop(cond, body, (qi * n_diag - 1, live_rows(carries), *carries, *accs))
    o_ref[...] = jnp.concatenate(state[2 + n_heads:], axis=1).astype(BF16)


def _sb_attn(u, c):
    bsz, s, _ = u.shape
    tq = _tile(s, 256)
    tk = _tile(tq, 128)
    width = SB_PAIRS_PER_STEP * LANES
    n_groups = c // width
    return pl.pallas_call(
        functools.partial(_sb_attn_kernel, tq=tq, tk=tk, n_pairs=SB_PAIRS_PER_STEP),
        grid=(bsz, n_groups, s // tq),
        in_specs=[
            pl.BlockSpec((None, tq, width), lambda b, p, i: (b, i, p)),
            pl.BlockSpec((None, s, width), lambda b, p, i: (b, 0, n_groups + p)),
            pl.BlockSpec((None, s, width), lambda b, p, i: (b, 0, 2 * n_groups + p)),
        ],
        out_specs=pl.BlockSpec((None, tq, width), lambda b, p, i: (b, i, p)),
        out_shape=jax.ShapeDtypeStruct((bsz, s, c), BF16),
        compiler_params=pltpu.CompilerParams(
            dimension_semantics=("arbitrary", "arbitrary", "arbitrary"), vmem_limit_bytes=VMEM_LIMIT),
        name="sb_attn",
    )(u, u, u)


def _mix_out_kernel(h_ref, y_ref, qm_ref, kx_ref, vx_ref, wo_ref, g_ref, b_ref, wr_ref, br_ref,
                    h1_ref, idx_ref, gate_ref, rank_ref, cnt_ref, *, tm, c, n_mem, alpha):
    s = jnp.dot(qm_ref[...], kx_ref[...], preferred_element_type=F32)
    probs = []
    for hd in range(MEM_HEADS):
        sh = s[:, hd * n_mem:(hd + 1) * n_mem]
        e = jnp.exp(sh - jnp.max(sh, axis=-1, keepdims=True))
        probs.append((e / jnp.sum(e, axis=-1, keepdims=True)).astype(BF16))
    y_mem = jnp.dot(jnp.concatenate(probs, axis=-1), vx_ref[...], preferred_element_type=F32)

    y = (jnp.dot(y_ref[...], wo_ref[0:c, :], preferred_element_type=F32)
         + jnp.dot(y_mem.astype(BF16), wo_ref[c:, :], preferred_element_type=F32))
    h1 = _layer_norm(alpha * h_ref[...] + y, g_ref[...], b_ref[...])
    h1_ref[...] = h1

    logits = lax.dot_general(wr_ref[...], h1.astype(BF16), (((1,), (1,)), ((), ())),
                             preferred_element_type=F32) + br_ref[...]
    e_iota = lax.broadcasted_iota(I32, (N_EXPERTS, tm), 0)
    vals, sels = [], []
    for _ in range(TOP_K):
        m = jnp.max(logits, axis=0, keepdims=True)
        sel = jnp.min(jnp.where(logits == m, e_iota, N_EXPERTS), axis=0, keepdims=True)
        vals.append(m)
        sels.append(sel)
        logits = jnp.where(e_iota == sel, -jnp.inf, logits)
    exps = [jnp.exp(v - vals[0]) for v in vals]
    denom = exps[0] + exps[1] + exps[2] + exps[3]
    gate_ref[...] = jnp.concatenate([e / denom for e in exps], axis=0)
    idx_ref[...] = jnp.concatenate(sels, axis=0)

    onehot = jnp.zeros((N_EXPERTS, tm), F32)
    for sel in sels:
        onehot = onehot + (e_iota == sel).astype(F32)
    earlier = (lax.broadcasted_iota(I32, (tm, tm), 0) < lax.broadcasted_iota(I32, (tm, tm), 1)).astype(BF16)
    rank_e = jnp.dot(onehot.astype(BF16), earlier, preferred_element_type=F32)
    rank_ref[...] = jnp.concatenate(
        [jnp.sum(jnp.where(e_iota == sel, rank_e, 0.0), axis=0, keepdims=True) for sel in sels],
        axis=0).astype(I32)
    cnt_ref[...] = jnp.sum(onehot, axis=1, keepdims=True).astype(I32)


def _mix_out(h, y_mix, qm_src, qm_block, kx, vx, wo_bf, ln_g, ln_b, wr_t_bf, b_router, alpha):
    bsz, s, d = h.shape
    c = y_mix.shape[-1]
    n_mem = vx.shape[1] // MEM_HEADS
    tm = _tile(s, TOKEN_TILE)
    n_s = s // tm
    n_tok = bsz * s
    tok_spec = pl.BlockSpec((TOP_K, tm), lambda b, i: (0, b * n_s + i))
    return pl.pallas_call(
        functools.partial(_mix_out_kernel, tm=tm, c=c, n_mem=n_mem, alpha=alpha),
        grid=(bsz, n_s),
        in_specs=[
            pl.BlockSpec((None, tm, d), lambda b, i: (b, i, 0)),
            pl.BlockSpec((None, tm, c), lambda b, i: (b, i, 0)),
            pl.BlockSpec((None, tm, MEM_WIDTH), lambda b, i: (b, i, qm_block)),
            pl.BlockSpec((None, MEM_WIDTH, MEM_HEADS * n_mem), lambda b, i: (b, 0, 0)),
            pl.BlockSpec((None, MEM_HEADS * n_mem, MEM_WIDTH), lambda b, i: (b, 0, 0)),
            pl.BlockSpec((c + MEM_WIDTH, d), lambda b, i: (0, 0)),
            pl.BlockSpec((1, d), lambda b, i: (0, 0)),
            pl.BlockSpec((1, d), lambda b, i: (0, 0)),
            pl.BlockSpec((N_EXPERTS, d), lambda b, i: (0, 0)),
            pl.BlockSpec((N_EXPERTS, 1), lambda b, i: (0, 0)),
        ],
        out_specs=[
            pl.BlockSpec((None, tm, d), lambda b, i: (b, i, 0)),
            tok_spec, tok_spec, tok_spec,
            pl.BlockSpec((None, N_EXPERTS, 1), lambda b, i: (b * n_s + i, 0, 0)),
        ],
        out_shape=[
            jax.ShapeDtypeStruct((bsz, s, d), F32),
            jax.ShapeDtypeStruct((TOP_K, n_tok), I32),
            jax.ShapeDtypeStruct((TOP_K, n_tok), F32),
            jax.ShapeDtypeStruct((TOP_K, n_tok), I32),
            jax.ShapeDtypeStruct((bsz * n_s, N_EXPERTS, 1), I32),
        ],
        compiler_params=pltpu.CompilerParams(
            dimension_semantics=("arbitrary", "arbitrary"), vmem_limit_bytes=VMEM_LIMIT),
        name="mix_out",
    )(h, y_mix, qm_src, kx, vx, wo_bf, ln_g.reshape(1, d), ln_b.reshape(1, d), wr_t_bf,
      b_router.reshape(N_EXPERTS, 1))


def _copy_rows(src_ref, src_row, dst_ref, dst_row, count, sem, max_bit, wait):
    for bit in range(max_bit, RUN_ALIGN.bit_length() - 2, -1):
        size = 1 << bit
        off = lax.shift_left(lax.shift_right_logical(count, bit + 1), bit + 1)

        @pl.when(jnp.bitwise_and(lax.shift_right_logical(count, bit), 1) == 1)
        def _():
            cp = pltpu.make_async_copy(src_ref.at[pl.ds(pl.multiple_of(src_row + off, RUN_ALIGN), size)],
                                       dst_ref.at[pl.ds(pl.multiple_of(dst_row + off, RUN_ALIGN), size)], sem)
            if wait:
                cp.wait()
            else:
                cp.start()


def _tile_runs(tile, tc_ref, ls_ref, base_ref, local_ref, global_ref, sem, max_bit, to_global, wait):
    def per_expert(e, carry):
        k = tile * N_EXPERTS + e
        if to_global:
            _copy_rows(local_ref, ls_ref[k], global_ref, base_ref[k], tc_ref[k], sem, max_bit, wait)
        else:
            _copy_rows(global_ref, base_ref[k], local_ref, ls_ref[k], tc_ref[k], sem, max_bit, wait)
        return carry

    lax.fori_loop(0, N_EXPERTS, per_expert, 0)


def _dispatch_kernel(tc_ref, ls_ref, base_ref, fill_lo_ref, fill_hi_ref, na_ref,
                     pos_ref, h_ref, xs_ref, buf, zero_ref, sems, fill_sem, *, tm, tmb):
    t = pl.program_id(0)
    n_t = pl.num_programs(0)
    slot = lax.rem(t, 2)
    max_bit = tm.bit_length() - 1

    def runs(tile, slot_, wait):
        _tile_runs(tile, tc_ref, ls_ref, base_ref, buf.at[slot_], xs_ref, sems.at[slot_], max_bit, True, wait)

    @pl.when(t >= 2)
    def _():
        runs(t - 2, slot, True)

    x = h_ref[...].astype(BF16)
    pos = pos_ref[...]

    def sort_rows(i, carry):
        r0 = pl.multiple_of(i * SORT_ROWS, SORT_ROWS)
        row = r0 + lax.broadcasted_iota(I32, (SORT_ROWS, tm), 0)
        perm = jnp.zeros((SORT_ROWS, tm), F32)
        for k in range(TOP_K):
            perm = jnp.where(row == pos[k:k + 1, :], 1.0, perm)
        rows = jnp.dot(perm.astype(BF16), x, preferred_element_type=F32)
        buf[slot, pl.ds(r0, SORT_ROWS), :] = _pack_halves(rows)
        return carry

    lax.fori_loop(0, _sorted_rows(tm) // SORT_ROWS, sort_rows, 0)
    runs(t, slot, False)

    def zero_fill(wait):
        pad_bit = tmb.bit_length() - 2

        def per_expert(e, carry):
            _copy_rows(zero_ref, 0, xs_ref, fill_lo_ref[e], fill_hi_ref[e] - fill_lo_ref[e],
                       fill_sem, pad_bit, wait)
            return carry

        def per_block(blk, carry):
            cp = pltpu.make_async_copy(zero_ref, xs_ref.at[pl.ds(blk * tmb, tmb)], fill_sem)
            if wait:
                cp.wait()
            else:
                cp.start()
            return carry

        lax.fori_loop(0, N_EXPERTS, per_expert, 0)
        lax.fori_loop(na_ref[0], xs_ref.shape[0] // tmb, per_block, 0)

    @pl.when(t == 0)
    def _():
        zero_ref[...] = jnp.zeros_like(zero_ref)
        zero_fill(False)

    @pl.when(t == n_t - 1)
    def _():
        @pl.when(t >= 1)
        def _():
            runs(t - 1, 1 - slot, True)
        runs(t, slot, True)
        zero_fill(True)


def _dispatch(h1_flat, pos, tables, fill_lo, fill_hi, n_active, n_rows, tm, tmb):
    n_tok, d = h1_flat.shape
    n_t = n_tok // tm
    tc, ls, base = tables
    return pl.pallas_call(
        functools.partial(_dispatch_kernel, tm=tm, tmb=tmb),
        grid_spec=pltpu.PrefetchScalarGridSpec(
            num_scalar_prefetch=6,
            grid=(n_t,),
            in_specs=[
                pl.BlockSpec((TOP_K, tm), lambda i, *_: (0, i)),
                pl.BlockSpec((tm, d), lambda i, *_: (i, 0)),
            ],
            out_specs=pl.BlockSpec(memory_space=pl.ANY),
            scratch_shapes=[
                pltpu.VMEM((2, _sorted_rows(tm), d // 2), I32),
                pltpu.VMEM((tmb, d // 2), I32),
                pltpu.SemaphoreType.DMA((2,)),
                pltpu.SemaphoreType.DMA,
            ],
        ),
        out_shape=jax.ShapeDtypeStruct((n_rows, d // 2), I32),
        compiler_params=pltpu.CompilerParams(
            dimension_semantics=("arbitrary",), vmem_limit_bytes=VMEM_LIMIT),
        name="moe_dispatch",
    )(tc, ls, base, fill_lo, fill_hi, n_active, pos, h1_flat)


def _expert_kernel(ord_ref, seq_ref, nseq_ref, na_ref, valid_ref, xs_ref, wgu_hbm, bgu_ref, wd_hbm, bd_ref, ys_ref,
                   wgu_f32, wd_f32, wgu_bf, wd_bf, sems, *, f):
    i = pl.program_id(0)
    n_active = na_ref[0]
    j = jnp.minimum(i, n_active - 1)
    prev = jnp.maximum(j - 1, 0)
    o = ord_ref[j]
    new_expert = jnp.logical_or(i == 0, o != ord_ref[prev])

    def fetch(o_, slot, wait):
        e = seq_ref[o_]
        copies = []
        for src, dst, parts in ((wgu_hbm, wgu_f32, WEIGHT_DMA_PARTS), (wd_hbm, wd_f32, WEIGHT_DMA_PARTS // 2)):
            rows = src.shape[1] // parts
            for p in range(parts):
                copies.append(pltpu.make_async_copy(src.at[e, pl.ds(p * rows, rows)],
                                                    dst.at[slot, pl.ds(p * rows, rows)], sems.at[slot]))
        for cp in copies:
            if wait:
                cp.wait()
            else:
                cp.start()

    @pl.when(i == 0)
    def _():
        fetch(0, 0, False)

    @pl.when(jnp.logical_and(i < n_active, new_expert))
    def _():
        slot = lax.rem(o, 2)
        fetch(o, slot, True)

        @pl.when(o + 1 < nseq_ref[0])
        def _():
            fetch(o + 1, 1 - slot, False)

        rows = 128
        for r in range(0, wgu_bf.shape[0], rows):
            wgu_bf[r:r + rows, :] = wgu_f32[slot, r:r + rows, :].astype(BF16)
        for r in range(0, wd_bf.shape[0], rows):
            wd_bf[r:r + rows, :] = wd_f32[slot, r:r + rows, :].astype(BF16)

    valid = valid_ref[i]
    for r in range(0, xs_ref.shape[0], EXPERT_SUB):
        @pl.when(valid > r)
        def _():
            x = _unpack_halves(xs_ref[r:r + EXPERT_SUB, :])
            gu = jnp.dot(x, wgu_bf[...], preferred_element_type=F32) + bgu_ref[...]
            gate = jnp.minimum(gu[:, :f], SWIGLU_LIMIT)
            up = jnp.clip(gu[:, f:], -SWIGLU_LIMIT, SWIGLU_LIMIT)
            act = (up + 1.0) * gate * _sigmoid(SWIGLU_ALPHA * gate)
            y = jnp.dot(act.astype(BF16), wd_bf[...], preferred_element_type=F32) + bd_ref[...]
            ys_ref[r:r + EXPERT_SUB, :] = _pack_halves(y.astype(BF16).astype(F32))

        @pl.when(valid <= r)
        def _():
            ys_ref[r:r + EXPERT_SUB, :] = jnp.zeros((EXPERT_SUB, ys_ref.shape[1]), I32)


def _experts(xs, block_ord, expert_seq, n_seq, n_active, block_valid, w_gate_up, b_gate_up, w_down, b_down, tmb):
    n_rows, half = xs.shape
    n_e, d, f2 = w_gate_up.shape
    f = f2 // 2
    nb = n_rows // tmb

    def blk(i, refs):
        return jnp.minimum(i, refs[3][0] - 1)

    def expert(i, refs):
        return refs[1][refs[0][blk(i, refs)]]

    return pl.pallas_call(
        functools.partial(_expert_kernel, f=f),
        grid_spec=pltpu.PrefetchScalarGridSpec(
            num_scalar_prefetch=5,
            grid=(nb,),
            in_specs=[
                pl.BlockSpec((tmb, half), lambda i, *refs: (blk(i, refs), 0)),
                pl.BlockSpec(memory_space=pl.ANY),
                pl.BlockSpec((None, 1, f2), lambda i, *refs: (expert(i, refs), 0, 0)),
                pl.BlockSpec(memory_space=pl.ANY),
                pl.BlockSpec((None, 1, d), lambda i, *refs: (expert(i, refs), 0, 0)),
            ],
            out_specs=pl.BlockSpec((tmb, half), lambda i, *refs: (i, 0)),
            scratch_shapes=[
                pltpu.VMEM((2, d, f2), F32), pltpu.VMEM((2, f, d), F32),
                pltpu.VMEM((d, f2), BF16), pltpu.VMEM((f, d), BF16),
                pltpu.SemaphoreType.DMA((2,)),
            ],
        ),
        out_shape=jax.ShapeDtypeStruct((n_rows, half), I32),
        compiler_params=pltpu.CompilerParams(
            dimension_semantics=("arbitrary",), vmem_limit_bytes=EXPERT_VMEM_LIMIT),
        name="moe_experts",
    )(block_ord, expert_seq, n_seq, n_active, block_valid, xs, w_gate_up, b_gate_up.reshape(n_e, 1, f2), w_down,
      b_down.reshape(n_e, 1, d))


def _combine_kernel(tc_ref, ls_ref, base_ref, pos_ref, gate_ref, h_ref, ys_ref, g_ref, b_ref, o_ref,
                    buf, sems, *, tm, alpha):
    t = pl.program_id(0)
    n_t = pl.num_programs(0)
    slot = lax.rem(t, 2)
    max_bit = tm.bit_length() - 1

    def runs(tile, slot_, wait):
        _tile_runs(tile, tc_ref, ls_ref, base_ref, buf.at[slot_], ys_ref, sems.at[slot_], max_bit, False, wait)

    @pl.when(t == 0)
    def _():
        buf[...] = jnp.zeros_like(buf)
        runs(t, slot, False)

    @pl.when(t + 1 < n_t)
    def _():
        runs(t + 1, 1 - slot, False)

    runs(t, slot, True)

    pos = pos_ref[...]
    gates = gate_ref[...]
    col = lax.broadcasted_iota(I32, (tm, _sorted_rows(tm)), 1)
    weights = jnp.zeros((tm, _sorted_rows(tm)), F32)
    for k in range(TOP_K):
        weights = jnp.where(col == pos[:, k:k + 1], gates[:, k:k + 1], weights)
    y = jnp.dot(weights.astype(BF16), _unpack_halves(buf[slot]), preferred_element_type=F32)
    o_ref[...] = _layer_norm(alpha * h_ref[...] + y, g_ref[...], b_ref[...])


def _combine(h1_flat, ys, pos_t, gates_t, tables, ln_g, ln_b, alpha, tm):
    n_tok, d = h1_flat.shape
    n_t = n_tok // tm
    tc, ls, base = tables
    return pl.pallas_call(
        functools.partial(_combine_kernel, tm=tm, alpha=alpha),
        grid_spec=pltpu.PrefetchScalarGridSpec(
            num_scalar_prefetch=3,
            grid=(n_t,),
            in_specs=[
                pl.BlockSpec((tm, TOP_K), lambda i, *_: (i, 0)),
                pl.BlockSpec((tm, TOP_K), lambda i, *_: (i, 0)),
                pl.BlockSpec((tm, d), lambda i, *_: (i, 0)),
                pl.BlockSpec(memory_space=pl.ANY),
                pl.BlockSpec((1, d), lambda i, *_: (0, 0)),
                pl.BlockSpec((1, d), lambda i, *_: (0, 0)),
            ],
            out_specs=pl.BlockSpec((tm, d), lambda i, *_: (i, 0)),
            scratch_shapes=[pltpu.VMEM((2, _sorted_rows(tm), d // 2), I32), pltpu.SemaphoreType.DMA((2,))],
        ),
        out_shape=jax.ShapeDtypeStruct((n_tok, d), F32),
        compiler_params=pltpu.CompilerParams(
            dimension_semantics=("arbitrary",), vmem_limit_bytes=VMEM_LIMIT),
        name="moe_combine",
    )(tc, ls, base, pos_t, gates_t, h1_flat, ys, ln_g.reshape(1, d), ln_b.reshape(1, d))


def _moe(h1, idx, gates, rank, tile_counts, layer, w_gate_up, b_gate_up, w_down, b_down, ln_g, ln_b, alpha):
    bsz, s, d = h1.shape
    n_tok = bsz * s
    tm = _tile(s, TOKEN_TILE)
    n_t = n_tok // tm
    tmb = EXPERT_ROWS
    n_rows = n_tok * TOP_K + n_t * N_EXPERTS * (RUN_ALIGN - 1) + N_EXPERTS * tmb
    n_rows = (n_rows + tmb - 1) // tmb * tmb
    nb = n_rows // tmb

    tile_counts = (tile_counts + RUN_ALIGN - 1) // RUN_ALIGN * RUN_ALIGN
    counts = jnp.sum(tile_counts, axis=0)
    padded = (counts + tmb - 1) // tmb * tmb
    pad_end = jnp.cumsum(padded).astype(I32)
    pad_start = pad_end - padded
    base = pad_start[None, :] + jnp.cumsum(tile_counts, axis=0) - tile_counts
    local = jnp.cumsum(tile_counts, axis=1) - tile_counts
    experts = jnp.arange(N_EXPERTS, dtype=I32)
    local_tok = jnp.repeat(local, tm, axis=0)
    pos = rank + jnp.sum(jnp.where(idx[:, :, None] == experts, local_tok[None], 0), axis=-1).astype(I32)
    block_start = jnp.arange(nb, dtype=I32) * tmb
    block_expert = jnp.minimum(
        jnp.sum((block_start[:, None] >= pad_end[None, :]).astype(I32), axis=1), N_EXPERTS - 1).astype(I32)
    n_active = (pad_end[-1:] // tmb).astype(I32)
    tables = (tile_counts.reshape(-1).astype(I32), local.reshape(-1).astype(I32), base.reshape(-1).astype(I32))

    h1_flat = h1.reshape(n_tok, d)
    xs = _dispatch(h1_flat, pos, tables, pad_start + counts, pad_end, n_active, n_rows, tm, tmb)
    merge = lambda w: w.reshape((w.shape[0] * w.shape[1],) + w.shape[2:])
    present = padded > 0
    expert_ord = jnp.cumsum(present.astype(I32)) - 1
    expert_seq = (jnp.argsort(jnp.logical_not(present), stable=True) + layer * N_EXPERTS).astype(I32)
    n_seq = jnp.sum(present.astype(I32)).reshape(1)
    block_valid = jnp.clip((pad_start + counts)[block_expert] - block_start, 0, tmb).astype(I32)
    ys = _experts(xs, expert_ord[block_expert].astype(I32), expert_seq, n_seq, n_active, block_valid,
                  merge(w_gate_up), merge(b_gate_up), merge(w_down), merge(b_down), tmb)
    out = _combine(h1_flat, ys, pos.T, gates.T, tables, ln_g, ln_b, alpha, tm)
    return out.reshape(bsz, s, d)


@jax.jit
def _forward(x, mem, mem_ln_g, mem_ln_b, w_mem_kv, w_in_conv, conv_dw_w, conv_dw_b, conv_ln_g,
             conv_ln_b, w_in_sb, w_mix_out, ln_mix_g, ln_mix_b, w_router, b_router, w_gate_up,
             b_gate_up, w_down, b_down, ln_moe_g, ln_moe_b):
    depth = w_mix_out.shape[0]
    alpha = float((2 * depth) ** 0.25)
    c = conv_dw_w.shape[-1]
    kx, vx = _mem_kv(mem, mem_ln_g, mem_ln_b, w_mem_kv.astype(BF16))

    h = x
    for i in range(depth):
        j = i // 2
        if i % 2 == 0:
            y_mix, qm = _conv_mix(h, w_in_conv[j].astype(BF16), conv_dw_w[j], conv_dw_b[j],
                                  conv_ln_g[j], conv_ln_b[j])
            qm_src, qm_block = qm, 0
        else:
            u = _sb_proj(h, w_in_sb[j].astype(BF16), c)
            y_mix = _sb_attn(u, c)
            qm_src, qm_block = u, (3 * c) // MEM_WIDTH
        h1, idx, gates, rank, cnt = _mix_out(
            h, y_mix, qm_src, qm_block, kx, vx, w_mix_out[i].astype(BF16), ln_mix_g[i], ln_mix_b[i],
            w_router[i].T.astype(BF16), b_router[i], alpha)
        h = _moe(h1, idx, gates, rank, cnt[:, :, 0], i, w_gate_up, b_gate_up, w_down, b_down,
                 ln_moe_g[i], ln_moe_b[i], alpha)
    return h


def kernel(x, mem, mem_ln_g, mem_ln_b, w_mem_kv, w_in_conv, conv_dw_w, conv_dw_b, conv_ln_g, conv_ln_b,
           w_in_sb, w_mix_out, ln_mix_g, ln_mix_b, w_router, b_router, w_gate_up, b_gate_up, w_down,
           b_down, ln_moe_g, ln_moe_b):
    return _forward(x, mem, mem_ln_g, mem_ln_b, w_mem_kv, w_in_conv, conv_dw_w, conv_dw_b, conv_ln_g,
                    conv_ln_b, w_in_sb, w_mix_out, ln_mix_g, ln_mix_b, w_router, b_router, w_gate_up,
                    b_gate_up, w_down, b_down, ln_moe_g, ln_moe_b)
```

```python
import functools

import jax
import jax.numpy as jnp
from jax import lax
from jax.experimental import pallas as pl
from jax.experimental.pallas import tpu as pltpu

F32 = jnp.float32
BF16 = jnp.bfloat16
I32 = jnp.int32

HEAD_DIM = 64
MEM_HEADS = 4
MEM_WIDTH = MEM_HEADS * HEAD_DIM
CONV_WIDTH = 31
N_EXPERTS = 32
TOP_K = 4
SWIGLU_LIMIT = 7.0
SWIGLU_ALPHA = 1.702
LN_EPS = 1e-5

LANES = 128
CONV_HALO = 32
CONV_STRIDE = 4
CONV_ROWS = 8 * CONV_STRIDE
SB_SKIP_BELOW = -104.0
LOG2_E = 1.4426950408889634
SB_PAIRS_PER_STEP = 3
SB_QUERY_ROWS = 256
SB_KEY_ROWS = 128
TOKEN_TILE = 512
EXPERT_ROWS = 1024
EXPERT_SUB = 256
SORT_ROWS = 256
RUN_ALIGN = 8


def _sorted_rows(tm):
    rows = TOP_K * tm + N_EXPERTS * RUN_ALIGN
    return (rows + SORT_ROWS - 1) // SORT_ROWS * SORT_ROWS
WEIGHT_DMA_PARTS = 4
VMEM_LIMIT = 48 * 1024 * 1024
EXPERT_VMEM_LIMIT = 56 * 1024 * 1024
HIGH_HALF = -65536


def _layer_norm(x, g, b):
    mu = jnp.mean(x, axis=-1, keepdims=True)
    xc = x - mu
    var = jnp.mean(xc * xc, axis=-1, keepdims=True)
    return xc * lax.rsqrt(var + LN_EPS) * g + b


def _sigmoid(x):
    return 1.0 / (1.0 + jnp.exp(-x))


def _tile(n, want):
    t = min(n, want)
    assert n % t == 0, (n, t)
    return t


def _pack_halves(x):
    n = x.shape[1] // 2
    lo = lax.bitcast_convert_type(x[:, :n], I32)
    hi = lax.bitcast_convert_type(x[:, n:], I32)
    return jnp.bitwise_or(jnp.bitwise_and(hi, HIGH_HALF), lax.shift_right_logical(lo, 16))


def _unpack_halves(p):
    lo = lax.bitcast_convert_type(lax.shift_left(p, 16), F32)
    hi = lax.bitcast_convert_type(jnp.bitwise_and(p, HIGH_HALF), F32)
    return jnp.concatenate([lo, hi], axis=1).astype(BF16)


def _mem_kv_kernel(mem_ref, g_ref, b_ref, w_ref, kx_ref, vx_ref):
    n_mem = mem_ref.shape[0]
    xn = _layer_norm(mem_ref[...], g_ref[...], b_ref[...])
    kv = jnp.dot(xn.astype(BF16), w_ref[...], preferred_element_type=F32)
    k_t = (kv[:, :MEM_WIDTH] * (HEAD_DIM ** -0.5)).T
    v = kv[:, MEM_WIDTH:]
    k_head = lax.broadcasted_iota(I32, (MEM_WIDTH, n_mem), 0) // HEAD_DIM
    v_head = lax.broadcasted_iota(I32, (n_mem, MEM_WIDTH), 1) // HEAD_DIM
    for hd in range(MEM_HEADS):
        kx_ref[:, hd * n_mem:(hd + 1) * n_mem] = jnp.where(k_head == hd, k_t, 0.0).astype(BF16)
        vx_ref[hd * n_mem:(hd + 1) * n_mem, :] = jnp.where(v_head == hd, v, 0.0).astype(BF16)


def _mem_kv(mem, g, b, w_bf):
    bsz, n_mem, d = mem.shape
    return pl.pallas_call(
        _mem_kv_kernel,
        grid=(bsz,),
        in_specs=[
            pl.BlockSpec((None, n_mem, d), lambda i: (i, 0, 0)),
            pl.BlockSpec((1, d), lambda i: (0, 0)),
            pl.BlockSpec((1, d), lambda i: (0, 0)),
            pl.BlockSpec((d, 2 * MEM_WIDTH), lambda i: (0, 0)),
        ],
        out_specs=[
            pl.BlockSpec((None, MEM_WIDTH, MEM_HEADS * n_mem), lambda i: (i, 0, 0)),
            pl.BlockSpec((None, MEM_HEADS * n_mem, MEM_WIDTH), lambda i: (i, 0, 0)),
        ],
        out_shape=[
            jax.ShapeDtypeStruct((bsz, MEM_WIDTH, MEM_HEADS * n_mem), BF16),
            jax.ShapeDtypeStruct((bsz, MEM_HEADS * n_mem, MEM_WIDTH), BF16),
        ],
        name="mem_kv",
    )(mem, g.reshape(1, d), b.reshape(1, d), w_bf)


def _conv_mix_kernel(h_ref, w_ref, dww_ref, dwb_ref, g_ref, b_ref, y_ref, qm_ref, gbuf, cbuf, *, tm, c):
    n_groups = c // LANES

    @pl.when(pl.program_id(1) == 0)
    def _():
        gbuf[:, 0:CONV_HALO, :] = jnp.zeros((n_groups, CONV_HALO, LANES), F32)

    @pl.when(pl.program_id(1) > 0)
    def _():
        gbuf[:, 0:CONV_HALO, :] = gbuf[:, tm:tm + CONV_HALO, :]

    u = jnp.dot(h_ref[...].astype(BF16), w_ref[...], preferred_element_type=F32)
    glu = u[:, :c] * _sigmoid(u[:, c:2 * c])
    for grp in range(n_groups):
        gbuf[grp, CONV_HALO:, :] = glu[:, grp * LANES:(grp + 1) * LANES]
    qm_ref[...] = u[:, 2 * c:].astype(BF16)

    first_tap = CONV_HALO - (CONV_WIDTH - 1)

    def rows(i, carry):
        r0 = pl.multiple_of(i * CONV_ROWS, CONV_ROWS)
        for grp in range(n_groups):
            src = gbuf.at[grp, pl.ds(r0, CONV_ROWS + CONV_HALO), :]
            dst = cbuf.at[grp, pl.ds(r0, CONV_ROWS), :]
            lanes = slice(grp * LANES, (grp + 1) * LANES)
            taps = [jnp.broadcast_to(dww_ref[j:j + 1, lanes], (8, LANES)) for j in range(CONV_WIDTH)]
            bias = jnp.broadcast_to(dwb_ref[:, lanes], (8, LANES))
            for r in range(CONV_STRIDE):
                acc = bias
                for j in range(CONV_WIDTH):
                    acc = acc + src[pl.ds(first_tap + j + r, 8, stride=CONV_STRIDE), :] * taps[j]
                dst[pl.ds(r, 8, stride=CONV_STRIDE), :] = acc
        conv = jnp.concatenate([cbuf[grp, pl.ds(r0, CONV_ROWS), :] for grp in range(n_groups)], axis=-1)
        yn = _layer_norm(conv, g_ref[...], b_ref[...])
        y_ref[pl.ds(r0, CONV_ROWS), :] = (yn * _sigmoid(yn)).astype(BF16)
        return carry

    lax.fori_loop(0, tm // CONV_ROWS, rows, 0)


def _conv_mix(h, w_bf, dw_w, dw_b, ln_g, ln_b):
    bsz, s, d = h.shape
    c = dw_w.shape[1]
    tm = _tile(s, TOKEN_TILE)
    n_out = w_bf.shape[1]
    row = lambda a: a.reshape(1, -1)
    return pl.pallas_call(
        functools.partial(_conv_mix_kernel, tm=tm, c=c),
        grid=(bsz, s // tm),
        in_specs=[
            pl.BlockSpec((None, tm, d), lambda b, i: (b, i, 0)),
            pl.BlockSpec((d, n_out), lambda b, i: (0, 0)),
            pl.BlockSpec((CONV_WIDTH, c), lambda b, i: (0, 0)),
            pl.BlockSpec((1, c), lambda b, i: (0, 0)),
            pl.BlockSpec((1, c), lambda b, i: (0, 0)),
            pl.BlockSpec((1, c), lambda b, i: (0, 0)),
        ],
        out_specs=[
            pl.BlockSpec((None, tm, c), lambda b, i: (b, i, 0)),
            pl.BlockSpec((None, tm, MEM_WIDTH), lambda b, i: (b, i, 0)),
        ],
        out_shape=[
            jax.ShapeDtypeStruct((bsz, s, c), BF16),
            jax.ShapeDtypeStruct((bsz, s, MEM_WIDTH), BF16),
        ],
        scratch_shapes=[pltpu.VMEM((c // LANES, tm + CONV_HALO, LANES), F32),
                        pltpu.VMEM((c // LANES, tm, LANES), F32)],
        compiler_params=pltpu.CompilerParams(
            dimension_semantics=("arbitrary", "arbitrary"), vmem_limit_bytes=VMEM_LIMIT),
        name="conv_mix",
    )(h, w_bf, dw_w, row(dw_b), row(ln_g), row(ln_b))


def _sb_proj_kernel(h_ref, w_ref, u_ref, *, c):
    u = jnp.dot(h_ref[...].astype(BF16), w_ref[...], preferred_element_type=F32)
    u_ref[:, :c] = (u[:, :c] * (HEAD_DIM ** -0.5)).astype(BF16)
    u_ref[:, c:] = u[:, c:].astype(BF16)


def _sb_proj(h, w_bf, c):
    bsz, s, d = h.shape
    tm = _tile(s, TOKEN_TILE)
    n_out = w_bf.shape[1]
    return pl.pallas_call(
        functools.partial(_sb_proj_kernel, c=c),
        grid=(bsz, s // tm),
        in_specs=[
            pl.BlockSpec((None, tm, d), lambda b, i: (b, i, 0)),
            pl.BlockSpec((d, n_out), lambda b, i: (0, 0)),
        ],
        out_specs=pl.BlockSpec((None, tm, n_out), lambda b, i: (b, i, 0)),
        out_shape=jax.ShapeDtypeStruct((bsz, s, n_out), BF16),
        compiler_params=pltpu.CompilerParams(
            dimension_semantics=("arbitrary", "arbitrary"), vmem_limit_bytes=VMEM_LIMIT),
        name="sb_proj",
    )(h, w_bf)


def _sb_attn_kernel(q_ref, k_ref, v_ref, o_ref, *, tq, tk, n_pairs):
    qi = pl.program_id(2)
    n_heads = 2 * n_pairs
    lane = lax.broadcasted_iota(I32, (tq, LANES), 1)
    k_lane = lax.broadcasted_iota(I32, (tk, LANES), 1)
    q_heads = []
    for pr in range(n_pairs):
        q2 = q_ref[:, pr * LANES:(pr + 1) * LANES]
        q_heads += [jnp.where(lane < HEAD_DIM, q2, jnp.zeros_like(q2)),
                    jnp.where(lane >= HEAD_DIM, q2, jnp.zeros_like(q2))]
    later = (lax.broadcasted_iota(I32, (tk, tk), 0) > lax.broadcasted_iota(I32, (tk, tk), 1)).astype(BF16)

    q_pos = qi * tq + lax.broadcasted_iota(I32, (tq, tk), 0)
    k_off = lax.broadcasted_iota(I32, (tq, tk), 1)

    def block(kb, carries, accs, masked):
        start = pl.multiple_of(kb * tk, tk)
        if masked:
            causal = (kb * tk + k_off) < q_pos
        log_nots, log_betas, parts = [], [], []
        for pr in range(n_pairs):
            k2 = k_ref[pl.ds(start, tk), pr * LANES:(pr + 1) * LANES]
            z2 = lax.dot_general(jnp.concatenate(q_heads[2 * pr:2 * pr + 2], axis=0), k2,
                                 (((1,), (1,)), ((), ())), preferred_element_type=F32)
            for half in range(2):
                z = z2[half * tq:(half + 1) * tq] * LOG2_E
                sp = jnp.maximum(z, 0.0) + jnp.log2(1.0 + jnp.exp2(-jnp.abs(z)))
                log_not = -sp
                if masked:
                    log_not = jnp.where(causal, log_not, 0.0)
                hi = log_not.astype(BF16)
                parts += [hi, (log_not - hi.astype(F32)).astype(BF16)]
                log_nots.append(log_not)
                log_betas.append(z - sp)
        sums = jnp.dot(jnp.concatenate(parts, axis=0), later, preferred_element_type=F32)
        new_c, new_a = [], []
        for pr in range(n_pairs):
            v2 = v_ref[pl.ds(start, tk), pr * LANES:(pr + 1) * LANES]
            ws = []
            for half in range(2):
                hd = 2 * pr + half
                tail = sums[2 * hd * tq:(2 * hd + 1) * tq] + sums[(2 * hd + 1) * tq:(2 * hd + 2) * tq] + carries[hd]
                w = jnp.exp2(log_betas[hd] + tail)
                if masked:
                    w = jnp.where(causal, w, 0.0)
                ws.append(w.astype(BF16))
                new_c.append(carries[hd] + jnp.sum(log_nots[hd], axis=1, keepdims=True))
            v_rows = jnp.concatenate([jnp.where(k_lane < HEAD_DIM, v2, jnp.zeros_like(v2)),
                                      jnp.where(k_lane >= HEAD_DIM, v2, jnp.zeros_like(v2))], axis=0)
            new_a.append(accs[pr] + jnp.dot(jnp.concatenate(ws, axis=1), v_rows, preferred_element_type=F32))
        return new_c, new_a

    carries = [jnp.zeros((tq, 1), F32)] * n_heads
    accs = [jnp.zeros((tq, LANES), F32)] * n_pairs
    n_diag = tq // tk
    for j in range(n_diag):
        carries, accs = block(qi * n_diag + (n_diag - 1 - j), carries, accs, True)

    def live(cs):
        top = cs[0]
        for cc in cs[1:]:
            top = jnp.maximum(top, cc)
        return (jnp.max(top) > SB_SKIP_BELOW * LOG2_E).astype(I32)

    def cond(state):
        kb, go = state[0], state[1]
        return jnp.logical_and(kb >= 0, go > 0)

    def body(state):
        kb = state[0]
        cs, acs = block(kb, list(state[2:2 + n_heads]), list(state[2 + n_heads:]), False)
        return (kb - 1, live(cs), *cs, *acs)

    state = lax.while_loop(cond, body, (qi * n_diag - 1, live(carries), *carries, *accs))
    o_ref[...] = jnp.concatenate(state[2 + n_heads:], axis=1).astype(BF16)


def _sb_attn(u, c):
    bsz, s, _ = u.shape
    tq = _tile(s, SB_QUERY_ROWS)
    tk = _tile(tq, SB_KEY_ROWS)
    width = SB_PAIRS_PER_STEP * LANES
    n_groups = c // width
    return pl.pallas_call(
        functools.partial(_sb_attn_kernel, tq=tq, tk=tk, n_pairs=SB_PAIRS_PER_STEP),
        grid=(bsz, n_groups, s // tq),
        in_specs=[
            pl.BlockSpec((None, tq, width), lambda b, p, i: (b, i, p)),
            pl.BlockSpec((None, s, width), lambda b, p, i: (b, 0, n_groups + p)),
            pl.BlockSpec((None, s, width), lambda b, p, i: (b, 0, 2 * n_groups + p)),
        ],
        out_specs=pl.BlockSpec((None, tq, width), lambda b, p, i: (b, i, p)),
        out_shape=jax.ShapeDtypeStruct((bsz, s, c), BF16),
        compiler_params=pltpu.CompilerParams(
            dimension_semantics=("arbitrary", "arbitrary", "arbitrary"), vmem_limit_bytes=VMEM_LIMIT),
        name="sb_attn",
    )(u, u, u)


def _mix_out_kernel(h_ref, y_ref, qm_ref, kx_ref, vx_ref, wo_ref, g_ref, b_ref, wr_ref, br_ref,
                    h1_ref, idx_ref, gate_ref, rank_ref, cnt_ref, *, tm, c, n_mem, alpha):
    s = jnp.dot(qm_ref[...], kx_ref[...], preferred_element_type=F32)
    probs = []
    for hd in range(MEM_HEADS):
        sh = s[:, hd * n_mem:(hd + 1) * n_mem]
        e = jnp.exp(sh - jnp.max(sh, axis=-1, keepdims=True))
        probs.append((e / jnp.sum(e, axis=-1, keepdims=True)).astype(BF16))
    y_mem = jnp.dot(jnp.concatenate(probs, axis=-1), vx_ref[...], preferred_element_type=F32)

    y = (jnp.dot(y_ref[...], wo_ref[0:c, :], preferred_element_type=F32)
         + jnp.dot(y_mem.astype(BF16), wo_ref[c:, :], preferred_element_type=F32))
    h1 = _layer_norm(alpha * h_ref[...] + y, g_ref[...], b_ref[...])
    h1_ref[...] = h1

    logits = lax.dot_general(wr_ref[...], h1.astype(BF16), (((1,), (1,)), ((), ())),
                             preferred_element_type=F32) + br_ref[...]
    e_iota = lax.broadcasted_iota(I32, (N_EXPERTS, tm), 0)
    vals, sels = [], []
    for _ in range(TOP_K):
        m = jnp.max(logits, axis=0, keepdims=True)
        sel = jnp.min(jnp.where(logits == m, e_iota, N_EXPERTS), axis=0, keepdims=True)
        vals.append(m)
        sels.append(sel)
        logits = jnp.where(e_iota == sel, -jnp.inf, logits)
    exps = [jnp.exp(v - vals[0]) for v in vals]
    denom = exps[0] + exps[1] + exps[2] + exps[3]
    gate_ref[...] = jnp.concatenate([e / denom for e in exps], axis=0)
    idx_ref[...] = jnp.concatenate(sels, axis=0)

    onehot = jnp.zeros((N_EXPERTS, tm), F32)
    for sel in sels:
        onehot = onehot + (e_iota == sel).astype(F32)
    earlier = (lax.broadcasted_iota(I32, (tm, tm), 0) < lax.broadcasted_iota(I32, (tm, tm), 1)).astype(BF16)
    rank_e = jnp.dot(onehot.astype(BF16), earlier, preferred_element_type=F32)
    rank_ref[...] = jnp.concatenate(
        [jnp.sum(jnp.where(e_iota == sel, rank_e, 0.0), axis=0, keepdims=True) for sel in sels],
        axis=0).astype(I32)
    cnt_ref[...] = jnp.sum(onehot, axis=1, keepdims=True).astype(I32)


def _mix_out(h, y_mix, qm_src, qm_block, kx, vx, wo_bf, ln_g, ln_b, wr_t_bf, b_router, alpha):
    bsz, s, d = h.shape
    c = y_mix.shape[-1]
    n_mem = vx.shape[1] // MEM_HEADS
    tm = _tile(s, TOKEN_TILE)
    n_s = s // tm
    n_tok = bsz * s
    tok_spec = pl.BlockSpec((TOP_K, tm), lambda b, i: (0, b * n_s + i))
    return pl.pallas_call(
        functools.partial(_mix_out_kernel, tm=tm, c=c, n_mem=n_mem, alpha=alpha),
        grid=(bsz, n_s),
        in_specs=[
            pl.BlockSpec((None, tm, d), lambda b, i: (b, i, 0)),
            pl.BlockSpec((None, tm, c), lambda b, i: (b, i, 0)),
            pl.BlockSpec((None, tm, MEM_WIDTH), lambda b, i: (b, i, qm_block)),
            pl.BlockSpec((None, MEM_WIDTH, MEM_HEADS * n_mem), lambda b, i: (b, 0, 0)),
            pl.BlockSpec((None, MEM_HEADS * n_mem, MEM_WIDTH), lambda b, i: (b, 0, 0)),
            pl.BlockSpec((c + MEM_WIDTH, d), lambda b, i: (0, 0)),
            pl.BlockSpec((1, d), lambda b, i: (0, 0)),
            pl.BlockSpec((1, d), lambda b, i: (0, 0)),
            pl.BlockSpec((N_EXPERTS, d), lambda b, i: (0, 0)),
            pl.BlockSpec((N_EXPERTS, 1), lambda b, i: (0, 0)),
        ],
        out_specs=[
            pl.BlockSpec((None, tm, d), lambda b, i: (b, i, 0)),
            tok_spec, tok_spec, tok_spec,
            pl.BlockSpec((None, N_EXPERTS, 1), lambda b, i: (b * n_s + i, 0, 0)),
        ],
        out_shape=[
            jax.ShapeDtypeStruct((bsz, s, d), F32),
            jax.ShapeDtypeStruct((TOP_K, n_tok), I32),
            jax.ShapeDtypeStruct((TOP_K, n_tok), F32),
            jax.ShapeDtypeStruct((TOP_K, n_tok), I32),
            jax.ShapeDtypeStruct((bsz * n_s, N_EXPERTS, 1), I32),
        ],
        compiler_params=pltpu.CompilerParams(
            dimension_semantics=("arbitrary", "arbitrary"), vmem_limit_bytes=VMEM_LIMIT),
        name="mix_out",
    )(h, y_mix, qm_src, kx, vx, wo_bf, ln_g.reshape(1, d), ln_b.reshape(1, d), wr_t_bf,
      b_router.reshape(N_EXPERTS, 1))


def _copy_rows(src_ref, src_row, dst_ref, dst_row, count, sem, max_bit, wait):
    for bit in range(max_bit, RUN_ALIGN.bit_length() - 2, -1):
        size = 1 << bit
        off = lax.shift_left(lax.shift_right_logical(count, bit + 1), bit + 1)

        @pl.when(jnp.bitwise_and(lax.shift_right_logical(count, bit), 1) == 1)
        def _():
            cp = pltpu.make_async_copy(src_ref.at[pl.ds(pl.multiple_of(src_row + off, RUN_ALIGN), size)],
                                       dst_ref.at[pl.ds(pl.multiple_of(dst_row + off, RUN_ALIGN), size)], sem)
            if wait:
                cp.wait()
            else:
                cp.start()


def _tile_runs(tile, tc_ref, ls_ref, base_ref, local_ref, global_ref, sem, max_bit, to_global, wait):
    def per_expert(e, carry):
        k = tile * N_EXPERTS + e
        if to_global:
            _copy_rows(local_ref, ls_ref[k], global_ref, base_ref[k], tc_ref[k], sem, max_bit, wait)
        else:
            _copy_rows(global_ref, base_ref[k], local_ref, ls_ref[k], tc_ref[k], sem, max_bit, wait)
        return carry

    lax.fori_loop(0, N_EXPERTS, per_expert, 0)


def _dispatch_kernel(tc_ref, ls_ref, base_ref, fill_lo_ref, fill_hi_ref, na_ref,
                     pos_ref, h_ref, xs_ref, buf, zero_ref, sems, fill_sem, *, tm, tmb):
    t = pl.program_id(0)
    n_t = pl.num_programs(0)
    slot = lax.rem(t, 2)
    max_bit = tm.bit_length() - 1

    def runs(tile, slot_, wait):
        _tile_runs(tile, tc_ref, ls_ref, base_ref, buf.at[slot_], xs_ref, sems.at[slot_], max_bit, True, wait)

    @pl.when(t >= 2)
    def _():
        runs(t - 2, slot, True)

    x = h_ref[...].astype(BF16)
    pos = pos_ref[...]

    def sort_rows(i, carry):
        r0 = pl.multiple_of(i * SORT_ROWS, SORT_ROWS)
        row = r0 + lax.broadcasted_iota(I32, (SORT_ROWS, tm), 0)
        perm = jnp.zeros((SORT_ROWS, tm), F32)
        for k in range(TOP_K):
            perm = jnp.where(row == pos[k:k + 1, :], 1.0, perm)
        rows = jnp.dot(perm.astype(BF16), x, preferred_element_type=F32)
        buf[slot, pl.ds(r0, SORT_ROWS), :] = _pack_halves(rows)
        return carry

    lax.fori_loop(0, _sorted_rows(tm) // SORT_ROWS, sort_rows, 0)
    runs(t, slot, False)

    def zero_fill(wait):
        pad_bit = tmb.bit_length() - 2

        def per_expert(e, carry):
            _copy_rows(zero_ref, 0, xs_ref, fill_lo_ref[e], fill_hi_ref[e] - fill_lo_ref[e],
                       fill_sem, pad_bit, wait)
            return carry

        def per_block(blk, carry):
            cp = pltpu.make_async_copy(zero_ref, xs_ref.at[pl.ds(blk * tmb, tmb)], fill_sem)
            if wait:
                cp.wait()
            else:
                cp.start()
            return carry

        lax.fori_loop(0, N_EXPERTS, per_expert, 0)
        lax.fori_loop(na_ref[0], xs_ref.shape[0] // tmb, per_block, 0)

    @pl.when(t == 0)
    def _():
        zero_ref[...] = jnp.zeros_like(zero_ref)
        zero_fill(False)

    @pl.when(t == n_t - 1)
    def _():
        @pl.when(t >= 1)
        def _():
            runs(t - 1, 1 - slot, True)
        runs(t, slot, True)
        zero_fill(True)


def _dispatch(h1_flat, pos, tables, fill_lo, fill_hi, n_active, n_rows, tm, tmb):
    n_tok, d = h1_flat.shape
    n_t = n_tok // tm
    tc, ls, base = tables
    return pl.pallas_call(
        functools.partial(_dispatch_kernel, tm=tm, tmb=tmb),
        grid_spec=pltpu.PrefetchScalarGridSpec(
            num_scalar_prefetch=6,
            grid=(n_t,),
            in_specs=[
                pl.BlockSpec((TOP_K, tm), lambda i, *_: (0, i)),
                pl.BlockSpec((tm, d), lambda i, *_: (i, 0)),
            ],
            out_specs=pl.BlockSpec(memory_space=pl.ANY),
            scratch_shapes=[
                pltpu.VMEM((2, _sorted_rows(tm), d // 2), I32),
                pltpu.VMEM((tmb, d // 2), I32),
                pltpu.SemaphoreType.DMA((2,)),
                pltpu.SemaphoreType.DMA,
            ],
        ),
        out_shape=jax.ShapeDtypeStruct((n_rows, d // 2), I32),
        compiler_params=pltpu.CompilerParams(
            dimension_semantics=("arbitrary",), vmem_limit_bytes=VMEM_LIMIT),
        name="moe_dispatch",
    )(tc, ls, base, fill_lo, fill_hi, n_active, pos, h1_flat)


def _expert_kernel(ord_ref, seq_ref, nseq_ref, na_ref, valid_ref, xs_ref, wgu_hbm, bgu_ref, wd_hbm, bd_ref, ys_ref,
                   wgu_f32, wd_f32, wgu_bf, wd_bf, sems, *, f):
    i = pl.program_id(0)
    n_active = na_ref[0]
    j = jnp.minimum(i, n_active - 1)
    prev = jnp.maximum(j - 1, 0)
    o = ord_ref[j]
    new_expert = jnp.logical_or(i == 0, o != ord_ref[prev])

    def fetch(o_, slot, wait):
        e = seq_ref[o_]
        copies = []
        for src, dst, parts in ((wgu_hbm, wgu_f32, WEIGHT_DMA_PARTS), (wd_hbm, wd_f32, WEIGHT_DMA_PARTS // 2)):
            rows = src.shape[1] // parts
            for p in range(parts):
                copies.append(pltpu.make_async_copy(src.at[e, pl.ds(p * rows, rows)],
                                                    dst.at[slot, pl.ds(p * rows, rows)], sems.at[slot]))
        for cp in copies:
            if wait:
                cp.wait()
            else:
                cp.start()

    @pl.when(i == 0)
    def _():
        fetch(0, 0, False)

    @pl.when(jnp.logical_and(i < n_active, new_expert))
    def _():
        slot = lax.rem(o, 2)
        fetch(o, slot, True)

        @pl.when(o + 1 < nseq_ref[0])
        def _():
            fetch(o + 1, 1 - slot, False)

        rows = 128
        for r in range(0, wgu_bf.shape[0], rows):
            wgu_bf[r:r + rows, :] = wgu_f32[slot, r:r + rows, :].astype(BF16)
        for r in range(0, wd_bf.shape[0], rows):
            wd_bf[r:r + rows, :] = wd_f32[slot, r:r + rows, :].astype(BF16)

    valid = valid_ref[i]
    for r in range(0, xs_ref.shape[0], EXPERT_SUB):
        @pl.when(valid > r)
        def _():
            x = _unpack_halves(xs_ref[r:r + EXPERT_SUB, :])
            gu = jnp.dot(x, wgu_bf[...], preferred_element_type=F32) + bgu_ref[...]
            gate = jnp.minimum(gu[:, :f], SWIGLU_LIMIT)
            up = jnp.clip(gu[:, f:], -SWIGLU_LIMIT, SWIGLU_LIMIT)
            act = (up + 1.0) * gate * _sigmoid(SWIGLU_ALPHA * gate)
            y = jnp.dot(act.astype(BF16), wd_bf[...], preferred_element_type=F32) + bd_ref[...]
            ys_ref[r:r + EXPERT_SUB, :] = _pack_halves(y.astype(BF16).astype(F32))

        @pl.when(valid <= r)
        def _():
            ys_ref[r:r + EXPERT_SUB, :] = jnp.zeros((EXPERT_SUB, ys_ref.shape[1]), I32)


def _experts(xs, block_ord, expert_seq, n_seq, n_active, block_valid, w_gate_up, b_gate_up, w_down, b_down, tmb):
    n_rows, half = xs.shape
    n_e, d, f2 = w_gate_up.shape
    f = f2 // 2
    nb = n_rows // tmb

    def blk(i, refs):
        return jnp.minimum(i, refs[3][0] - 1)

    def expert(i, refs):
        return refs[1][refs[0][blk(i, refs)]]

    return pl.pallas_call(
        functools.partial(_expert_kernel, f=f),
        grid_spec=pltpu.PrefetchScalarGridSpec(
            num_scalar_prefetch=5,
            grid=(nb,),
            in_specs=[
                pl.BlockSpec((tmb, half), lambda i, *refs: (blk(i, refs), 0)),
                pl.BlockSpec(memory_space=pl.ANY),
                pl.BlockSpec((None, 1, f2), lambda i, *refs: (expert(i, refs), 0, 0)),
                pl.BlockSpec(memory_space=pl.ANY),
                pl.BlockSpec((None, 1, d), lambda i, *refs: (expert(i, refs), 0, 0)),
            ],
            out_specs=pl.BlockSpec((tmb, half), lambda i, *refs: (i, 0)),
            scratch_shapes=[
                pltpu.VMEM((2, d, f2), F32), pltpu.VMEM((2, f, d), F32),
                pltpu.VMEM((d, f2), BF16), pltpu.VMEM((f, d), BF16),
                pltpu.SemaphoreType.DMA((2,)),
            ],
        ),
        out_shape=jax.ShapeDtypeStruct((n_rows, half), I32),
        compiler_params=pltpu.CompilerParams(
            dimension_semantics=("arbitrary",), vmem_limit_bytes=EXPERT_VMEM_LIMIT),
        name="moe_experts",
    )(block_ord, expert_seq, n_seq, n_active, block_valid, xs, w_gate_up, b_gate_up.reshape(n_e, 1, f2), w_down,
      b_down.reshape(n_e, 1, d))


def _combine_kernel(tc_ref, ls_ref, base_ref, pos_ref, gate_ref, h_ref, ys_ref, g_ref, b_ref, o_ref,
                    buf, sems, *, tm, alpha):
    t = pl.program_id(0)
    n_t = pl.num_programs(0)
    slot = lax.rem(t, 2)
    max_bit = tm.bit_length() - 1

    def runs(tile, slot_, wait):
        _tile_runs(tile, tc_ref, ls_ref, base_ref, buf.at[slot_], ys_ref, sems.at[slot_], max_bit, False, wait)

    @pl.when(t == 0)
    def _():
        buf[...] = jnp.zeros_like(buf)
        runs(t, slot, False)

    @pl.when(t + 1 < n_t)
    def _():
        runs(t + 1, 1 - slot, False)

    runs(t, slot, True)

    pos = pos_ref[...]
    gates = gate_ref[...]
    col = lax.broadcasted_iota(I32, (tm, _sorted_rows(tm)), 1)
    weights = jnp.zeros((tm, _sorted_rows(tm)), F32)
    for k in range(TOP_K):
        weights = jnp.where(col == pos[:, k:k + 1], gates[:, k:k + 1], weights)
    y = jnp.dot(weights.astype(BF16), _unpack_halves(buf[slot]), preferred_element_type=F32)
    o_ref[...] = _layer_norm(alpha * h_ref[...] + y, g_ref[...], b_ref[...])


def _combine(h1_flat, ys, pos_t, gates_t, tables, ln_g, ln_b, alpha, tm):
    n_tok, d = h1_flat.shape
    n_t = n_tok // tm
    tc, ls, base = tables
    return pl.pallas_call(
        functools.partial(_combine_kernel, tm=tm, alpha=alpha),
        grid_spec=pltpu.PrefetchScalarGridSpec(
            num_scalar_prefetch=3,
            grid=(n_t,),
            in_specs=[
                pl.BlockSpec((tm, TOP_K), lambda i, *_: (i, 0)),
                pl.BlockSpec((tm, TOP_K), lambda i, *_: (i, 0)),
                pl.BlockSpec((tm, d), lambda i, *_: (i, 0)),
                pl.BlockSpec(memory_space=pl.ANY),
                pl.BlockSpec((1, d), lambda i, *_: (0, 0)),
                pl.BlockSpec((1, d), lambda i, *_: (0, 0)),
            ],
            out_specs=pl.BlockSpec((tm, d), lambda i, *_: (i, 0)),
            scratch_shapes=[pltpu.VMEM((2, _sorted_rows(tm), d // 2), I32), pltpu.SemaphoreType.DMA((2,))],
        ),
        out_shape=jax.ShapeDtypeStruct((n_tok, d), F32),
        compiler_params=pltpu.CompilerParams(
            dimension_semantics=("arbitrary",), vmem_limit_bytes=VMEM_LIMIT),
        name="moe_combine",
    )(tc, ls, base, pos_t, gates_t, h1_flat, ys, ln_g.reshape(1, d), ln_b.reshape(1, d))


def _moe(h1, idx, gates, rank, tile_counts, layer, w_gate_up, b_gate_up, w_down, b_down, ln_g, ln_b, alpha):
    bsz, s, d = h1.shape
    n_tok = bsz * s
    tm = _tile(s, TOKEN_TILE)
    n_t = n_tok // tm
    tmb = EXPERT_ROWS
    n_rows = n_tok * TOP_K + n_t * N_EXPERTS * (RUN_ALIGN - 1) + N_EXPERTS * tmb
    n_rows = (n_rows + tmb - 1) // tmb * tmb
    nb = n_rows // tmb

    tile_counts = (tile_counts + RUN_ALIGN - 1) // RUN_ALIGN * RUN_ALIGN
    counts = jnp.sum(tile_counts, axis=0)
    padded = (counts + tmb - 1) // tmb * tmb
    pad_end = jnp.cumsum(padded).astype(I32)
    pad_start = pad_end - padded
    base = pad_start[None, :] + jnp.cumsum(tile_counts, axis=0) - tile_counts
    local = jnp.cumsum(tile_counts, axis=1) - tile_counts
    experts = jnp.arange(N_EXPERTS, dtype=I32)
    local_tok = jnp.repeat(local, tm, axis=0)
    pos = rank + jnp.sum(jnp.where(idx[:, :, None] == experts, local_tok[None], 0), axis=-1).astype(I32)
    block_start = jnp.arange(nb, dtype=I32) * tmb
    block_expert = jnp.minimum(
        jnp.sum((block_start[:, None] >= pad_end[None, :]).astype(I32), axis=1), N_EXPERTS - 1).astype(I32)
    n_active = (pad_end[-1:] // tmb).astype(I32)
    tables = (tile_counts.reshape(-1).astype(I32), local.reshape(-1).astype(I32), base.reshape(-1).astype(I32))

    h1_flat = h1.reshape(n_tok, d)
    xs = _dispatch(h1_flat, pos, tables, pad_start + counts, pad_end, n_active, n_rows, tm, tmb)
    merge = lambda w: w.reshape((w.shape[0] * w.shape[1],) + w.shape[2:])
    present = padded > 0
    expert_ord = jnp.cumsum(present.astype(I32)) - 1
    expert_seq = (jnp.argsort(jnp.logical_not(present), stable=True) + layer * N_EXPERTS).astype(I32)
    n_seq = jnp.sum(present.astype(I32)).reshape(1)
    block_valid = jnp.clip((pad_start + counts)[block_expert] - block_start, 0, tmb).astype(I32)
    ys = _experts(xs, expert_ord[block_expert].astype(I32), expert_seq, n_seq, n_active, block_valid,
                  merge(w_gate_up), merge(b_gate_up), merge(w_down), merge(b_down), tmb)
    out = _combine(h1_flat, ys, pos.T, gates.T, tables, ln_g, ln_b, alpha, tm)
    return out.reshape(bsz, s, d)


@jax.jit
def _forward(x, mem, mem_ln_g, mem_ln_b, w_mem_kv, w_in_conv, conv_dw_w, conv_dw_b, conv_ln_g,
             conv_ln_b, w_in_sb, w_mix_out, ln_mix_g, ln_mix_b, w_router, b_router, w_gate_up,
             b_gate_up, w_down, b_down, ln_moe_g, ln_moe_b):
    depth = w_mix_out.shape[0]
    alpha = float((2 * depth) ** 0.25)
    c = conv_dw_w.shape[-1]
    kx, vx = _mem_kv(mem, mem_ln_g, mem_ln_b, w_mem_kv.astype(BF16))

    h = x
    for i in range(depth):
        j = i // 2
        if i % 2 == 0:
            y_mix, qm = _conv_mix(h, w_in_conv[j].astype(BF16), conv_dw_w[j], conv_dw_b[j],
                                  conv_ln_g[j], conv_ln_b[j])
            qm_src, qm_block = qm, 0
        else:
            u = _sb_proj(h, w_in_sb[j].astype(BF16), c)
            y_mix = _sb_attn(u, c)
            qm_src, qm_block = u, (3 * c) // MEM_WIDTH
        h1, idx, gates, rank, cnt = _mix_out(
            h, y_mix, qm_src, qm_block, kx, vx, w_mix_out[i].astype(BF16), ln_mix_g[i], ln_mix_b[i],
            w_router[i].T.astype(BF16), b_router[i], alpha)
        h = _moe(h1, idx, gates, rank, cnt[:, :, 0], i, w_gate_up, b_gate_up, w_down, b_down,
                 ln_moe_g[i], ln_moe_b[i], alpha)
    return h


def kernel(x, mem, mem_ln_g, mem_ln_b, w_mem_kv, w_in_conv, conv_dw_w, conv_dw_b, conv_ln_g, conv_ln_b,
           w_in_sb, w_mix_out, ln_mix_g, ln_mix_b, w_router, b_router, w_gate_up, b_gate_up, w_down,
           b_down, ln_moe_g, ln_moe_b):
    return _forward(x, mem, mem_ln_g, mem_ln_b, w_mem_kv, w_in_conv, conv_dw_w, conv_dw_b, conv_ln_g,
                    conv_ln_b, w_in_sb, w_mix_out, ln_mix_g, ln_mix_b, w_router, b_router, w_gate_up,
                    b_gate_up, w_down, b_down, ln_moe_g, ln_moe_b)
```

```python
import functools

import jax
import jax.numpy as jnp
from jax import lax
from jax.experimental import pallas as pl
from jax.experimental.pallas import tpu as pltpu

F32 = jnp.float32
BF16 = jnp.bfloat16
I32 = jnp.int32

HEAD_DIM = 64
MEM_HEADS = 4
MEM_WIDTH = MEM_HEADS * HEAD_DIM
CONV_WIDTH = 31
N_EXPERTS = 32
TOP_K = 4
SWIGLU_LIMIT = 7.0
SWIGLU_ALPHA = 1.702
LN_EPS = 1e-5

LANES = 128
CONV_HALO = 32
CONV_STRIDE = 4
CONV_ROWS = 8 * CONV_STRIDE
SB_SKIP_BELOW = -104.0
LOG2_E = 1.4426950408889634
SB_PAIRS_PER_STEP = 3
SB_QUERY_ROWS = 256
SB_KEY_ROWS = 128
TOKEN_TILE = 512
EXPERT_ROWS = 1024
EXPERT_SUB = 256
SORT_ROWS = 256
RUN_ALIGN = 8


def _sorted_rows(tm):
    rows = TOP_K * tm + N_EXPERTS * RUN_ALIGN
    return (rows + SORT_ROWS - 1) // SORT_ROWS * SORT_ROWS
WEIGHT_DMA_PARTS = 4
VMEM_LIMIT = 48 * 1024 * 1024
EXPERT_VMEM_LIMIT = 56 * 1024 * 1024
HIGH_HALF = -65536


def _layer_norm(x, g, b):
    mu = jnp.mean(x, axis=-1, keepdims=True)
    xc = x - mu
    var = jnp.mean(xc * xc, axis=-1, keepdims=True)
    return xc * lax.rsqrt(var + LN_EPS) * g + b


def _sigmoid(x):
    return 1.0 / (1.0 + jnp.exp(-x))


def _tile(n, want):
    t = min(n, want)
    assert n % t == 0, (n, t)
    return t


def _pack_halves(x):
    n = x.shape[1] // 2
    lo = lax.bitcast_convert_type(x[:, :n], I32)
    hi = lax.bitcast_convert_type(x[:, n:], I32)
    return jnp.bitwise_or(jnp.bitwise_and(hi, HIGH_HALF), lax.shift_right_logical(lo, 16))


def _unpack_halves(p):
    lo = lax.bitcast_convert_type(lax.shift_left(p, 16), F32)
    hi = lax.bitcast_convert_type(jnp.bitwise_and(p, HIGH_HALF), F32)
    return jnp.concatenate([lo, hi], axis=1).astype(BF16)


def _mem_kv_kernel(mem_ref, g_ref, b_ref, w_ref, kx_ref, vx_ref):
    n_mem = mem_ref.shape[0]
    xn = _layer_norm(mem_ref[...], g_ref[...], b_ref[...])
    kv = jnp.dot(xn.astype(BF16), w_ref[...], preferred_element_type=F32)
    k_t = (kv[:, :MEM_WIDTH] * (HEAD_DIM ** -0.5)).T
    v = kv[:, MEM_WIDTH:]
    k_head = lax.broadcasted_iota(I32, (MEM_WIDTH, n_mem), 0) // HEAD_DIM
    v_head = lax.broadcasted_iota(I32, (n_mem, MEM_WIDTH), 1) // HEAD_DIM
    for hd in range(MEM_HEADS):
        kx_ref[:, hd * n_mem:(hd + 1) * n_mem] = jnp.where(k_head == hd, k_t, 0.0).astype(BF16)
        vx_ref[hd * n_mem:(hd + 1) * n_mem, :] = jnp.where(v_head == hd, v, 0.0).astype(BF16)


def _mem_kv(mem, g, b, w_bf):
    bsz, n_mem, d = mem.shape
    return pl.pallas_call(
        _mem_kv_kernel,
        grid=(bsz,),
        in_specs=[
            pl.BlockSpec((None, n_mem, d), lambda i: (i, 0, 0)),
            pl.BlockSpec((1, d), lambda i: (0, 0)),
            pl.BlockSpec((1, d), lambda i: (0, 0)),
            pl.BlockSpec((d, 2 * MEM_WIDTH), lambda i: (0, 0)),
        ],
        out_specs=[
            pl.BlockSpec((None, MEM_WIDTH, MEM_HEADS * n_mem), lambda i: (i, 0, 0)),
            pl.BlockSpec((None, MEM_HEADS * n_mem, MEM_WIDTH), lambda i: (i, 0, 0)),
        ],
        out_shape=[
            jax.ShapeDtypeStruct((bsz, MEM_WIDTH, MEM_HEADS * n_mem), BF16),
            jax.ShapeDtypeStruct((bsz, MEM_HEADS * n_mem, MEM_WIDTH), BF16),
        ],
        name="mem_kv",
    )(mem, g.reshape(1, d), b.reshape(1, d), w_bf)


def _conv_mix_kernel(h_ref, w_ref, dww_ref, dwb_ref, g_ref, b_ref, y_ref, qm_ref, gbuf, cbuf, *, tm, c):
    n_groups = c // LANES

    @pl.when(pl.program_id(1) == 0)
    def _():
        gbuf[:, 0:CONV_HALO, :] = jnp.zeros((n_groups, CONV_HALO, LANES), F32)

    @pl.when(pl.program_id(1) > 0)
    def _():
        gbuf[:, 0:CONV_HALO, :] = gbuf[:, tm:tm + CONV_HALO, :]

    u = jnp.dot(h_ref[...].astype(BF16), w_ref[...], preferred_element_type=F32)
    glu = u[:, :c] * _sigmoid(u[:, c:2 * c])
    for grp in range(n_groups):
        gbuf[grp, CONV_HALO:, :] = glu[:, grp * LANES:(grp + 1) * LANES]
    qm_ref[...] = u[:, 2 * c:].astype(BF16)

    first_tap = CONV_HALO - (CONV_WIDTH - 1)

    def rows(i, carry):
        r0 = pl.multiple_of(i * CONV_ROWS, CONV_ROWS)
        for grp in range(n_groups):
            src = gbuf.at[grp, pl.ds(r0, CONV_ROWS + CONV_HALO), :]
            dst = cbuf.at[grp, pl.ds(r0, CONV_ROWS), :]
            lanes = slice(grp * LANES, (grp + 1) * LANES)
            taps = [jnp.broadcast_to(dww_ref[j:j + 1, lanes], (8, LANES)) for j in range(CONV_WIDTH)]
            bias = jnp.broadcast_to(dwb_ref[:, lanes], (8, LANES))
            for r in range(CONV_STRIDE):
                acc = bias
                for j in range(CONV_WIDTH):
                    acc = acc + src[pl.ds(first_tap + j + r, 8, stride=CONV_STRIDE), :] * taps[j]
                dst[pl.ds(r, 8, stride=CONV_STRIDE), :] = acc
        conv = jnp.concatenate([cbuf[grp, pl.ds(r0, CONV_ROWS), :] for grp in range(n_groups)], axis=-1)
        yn = _layer_norm(conv, g_ref[...], b_ref[...])
        y_ref[pl.ds(r0, CONV_ROWS), :] = (yn * _sigmoid(yn)).astype(BF16)
        return carry

    lax.fori_loop(0, tm // CONV_ROWS, rows, 0)


def _conv_mix(h, w_bf, dw_w, dw_b, ln_g, ln_b):
    bsz, s, d = h.shape
    c = dw_w.shape[1]
    tm = _tile(s, TOKEN_TILE)
    n_out = w_bf.shape[1]
    row = lambda a: a.reshape(1, -1)
    return pl.pallas_call(
        functools.partial(_conv_mix_kernel, tm=tm, c=c),
        grid=(bsz, s // tm),
        in_specs=[
            pl.BlockSpec((None, tm, d), lambda b, i: (b, i, 0)),
            pl.BlockSpec((d, n_out), lambda b, i: (0, 0)),
            pl.BlockSpec((CONV_WIDTH, c), lambda b, i: (0, 0)),
            pl.BlockSpec((1, c), lambda b, i: (0, 0)),
            pl.BlockSpec((1, c), lambda b, i: (0, 0)),
            pl.BlockSpec((1, c), lambda b, i: (0, 0)),
        ],
        out_specs=[
            pl.BlockSpec((None, tm, c), lambda b, i: (b, i, 0)),
            pl.BlockSpec((None, tm, MEM_WIDTH), lambda b, i: (b, i, 0)),
        ],
        out_shape=[
            jax.ShapeDtypeStruct((bsz, s, c), BF16),
            jax.ShapeDtypeStruct((bsz, s, MEM_WIDTH), BF16),
        ],
        scratch_shapes=[pltpu.VMEM((c // LANES, tm + CONV_HALO, LANES), F32),
                        pltpu.VMEM((c // LANES, tm, LANES), F32)],
        compiler_params=pltpu.CompilerParams(
            dimension_semantics=("arbitrary", "arbitrary"), vmem_limit_bytes=VMEM_LIMIT),
        name="conv_mix",
    )(h, w_bf, dw_w, row(dw_b), row(ln_g), row(ln_b))


def _sb_proj_kernel(h_ref, w_ref, u_ref, *, c):
    u = jnp.dot(h_ref[...].astype(BF16), w_ref[...], preferred_element_type=F32)
    u_ref[:, :c] = (u[:, :c] * (HEAD_DIM ** -0.5)).astype(BF16)
    u_ref[:, c:] = u[:, c:].astype(BF16)


def _sb_proj(h, w_bf, c):
    bsz, s, d = h.shape
    tm = _tile(s, TOKEN_TILE)
    n_out = w_bf.shape[1]
    return pl.pallas_call(
        functools.partial(_sb_proj_kernel, c=c),
        grid=(bsz, s // tm),
        in_specs=[
            pl.BlockSpec((None, tm, d), lambda b, i: (b, i, 0)),
            pl.BlockSpec((d, n_out), lambda b, i: (0, 0)),
        ],
        out_specs=pl.BlockSpec((None, tm, n_out), lambda b, i: (b, i, 0)),
        out_shape=jax.ShapeDtypeStruct((bsz, s, n_out), BF16),
        compiler_params=pltpu.CompilerParams(
            dimension_semantics=("arbitrary", "arbitrary"), vmem_limit_bytes=VMEM_LIMIT),
        name="sb_proj",
    )(h, w_bf)


def _sb_attn_kernel(q_ref, k_ref, v_ref, o_ref, *, tq, tk, n_pairs):
    qi = pl.program_id(2)
    n_heads = 2 * n_pairs
    lane = lax.broadcasted_iota(I32, (tq, LANES), 1)
    k_lane = lax.broadcasted_iota(I32, (tk, LANES), 1)
    q_heads = []
    for pr in range(n_pairs):
        q2 = q_ref[:, pr * LANES:(pr + 1) * LANES]
        q_heads += [jnp.where(lane < HEAD_DIM, q2, jnp.zeros_like(q2)),
                    jnp.where(lane >= HEAD_DIM, q2, jnp.zeros_like(q2))]
    later = (lax.broadcasted_iota(I32, (tk, tk), 0) > lax.broadcasted_iota(I32, (tk, tk), 1)).astype(BF16)

    q_pos = qi * tq + lax.broadcasted_iota(I32, (tq, tk), 0)
    k_off = lax.broadcasted_iota(I32, (tq, tk), 1)

    def block(kb, carries, accs, masked):
        start = pl.multiple_of(kb * tk, tk)
        if masked:
            causal = (kb * tk + k_off) < q_pos
        log_nots, log_betas, parts = [], [], []
        for pr in range(n_pairs):
            k2 = k_ref[pl.ds(start, tk), pr * LANES:(pr + 1) * LANES]
            z2 = lax.dot_general(jnp.concatenate(q_heads[2 * pr:2 * pr + 2], axis=0), k2,
                                 (((1,), (1,)), ((), ())), preferred_element_type=F32)
            for half in range(2):
                z = z2[half * tq:(half + 1) * tq] * LOG2_E
                sp = jnp.maximum(z, 0.0) + jnp.log2(1.0 + jnp.exp2(-jnp.abs(z)))
                log_not = -sp
                if masked:
                    log_not = jnp.where(causal, log_not, 0.0)
                hi = log_not.astype(BF16)
                parts += [hi, (log_not - hi.astype(F32)).astype(BF16)]
                log_nots.append(log_not)
                log_betas.append(z - sp)
        sums = jnp.dot(jnp.concatenate(parts, axis=0), later, preferred_element_type=F32)
        new_c, new_a = [], []
        for pr in range(n_pairs):
            v2 = v_ref[pl.ds(start, tk), pr * LANES:(pr + 1) * LANES]
            ws = []
            for half in range(2):
                hd = 2 * pr + half
                tail = sums[2 * hd * tq:(2 * hd + 1) * tq] + sums[(2 * hd + 1) * tq:(2 * hd + 2) * tq] + carries[hd]
                w = jnp.exp2(log_betas[hd] + tail)
                if masked:
                    w = jnp.where(causal, w, 0.0)
                ws.append(w.astype(BF16))
                new_c.append(carries[hd] + jnp.sum(log_nots[hd], axis=1, keepdims=True))
            v_rows = jnp.concatenate([jnp.where(k_lane < HEAD_DIM, v2, jnp.zeros_like(v2)),
                                      jnp.where(k_lane >= HEAD_DIM, v2, jnp.zeros_like(v2))], axis=0)
            new_a.append(accs[pr] + jnp.dot(jnp.concatenate(ws, axis=1), v_rows, preferred_element_type=F32))
        return new_c, new_a

    carries = [jnp.zeros((tq, 1), F32)] * n_heads
    accs = [jnp.zeros((tq, LANES), F32)] * n_pairs
    n_diag = tq // tk
    for j in range(n_diag):
        carries, accs = block(qi * n_diag + (n_diag - 1 - j), carries, accs, True)

    def live(cs):
        top = cs[0]
        for cc in cs[1:]:
            top = jnp.maximum(top, cc)
        return (jnp.max(top) > SB_SKIP_BELOW * LOG2_E).astype(I32)

    def cond(state):
        kb, go = state[0], state[1]
        return jnp.logical_and(kb >= 0, go > 0)

    def body(state):
        kb = state[0]
        cs, acs = block(kb, list(state[2:2 + n_heads]), list(state[2 + n_heads:]), False)
        return (kb - 1, live(cs), *cs, *acs)

    state = lax.while_loop(cond, body, (qi * n_diag - 1, live(carries), *carries, *accs))
    o_ref[...] = jnp.concatenate(state[2 + n_heads:], axis=1).astype(BF16)


def _sb_attn(u, c):
    bsz, s, _ = u.shape
    tq = _tile(s, SB_QUERY_ROWS)
    tk = _tile(tq, SB_KEY_ROWS)
    width = SB_PAIRS_PER_STEP * LANES
    n_groups = c // width
    return pl.pallas_call(
        functools.partial(_sb_attn_kernel, tq=tq, tk=tk, n_pairs=SB_PAIRS_PER_STEP),
        grid=(bsz, n_groups, s // tq),
        in_specs=[
            pl.BlockSpec((None, tq, width), lambda b, p, i: (b, i, p)),
            pl.BlockSpec((None, s, width), lambda b, p, i: (b, 0, n_groups + p)),
            pl.BlockSpec((None, s, width), lambda b, p, i: (b, 0, 2 * n_groups + p)),
        ],
        out_specs=pl.BlockSpec((None, tq, width), lambda b, p, i: (b, i, p)),
        out_shape=jax.ShapeDtypeStruct((bsz, s, c), BF16),
        compiler_params=pltpu.CompilerParams(
            dimension_semantics=("arbitrary", "arbitrary", "arbitrary"), vmem_limit_bytes=VMEM_LIMIT),
        name="sb_attn",
    )(u, u, u)


def _mix_out_kernel(h_ref, y_ref, qm_ref, kx_ref, vx_ref, wo_ref, g_ref, b_ref, wr_ref, br_ref,
                    h1_ref, idx_ref, gate_ref, rank_ref, cnt_ref, *, tm, c, n_mem, alpha):
    s = jnp.dot(qm_ref[...], kx_ref[...], preferred_element_type=F32)
    probs = []
    for hd in range(MEM_HEADS):
        sh = s[:, hd * n_mem:(hd + 1) * n_mem]
        e = jnp.exp(sh - jnp.max(sh, axis=-1, keepdims=True))
        probs.append((e / jnp.sum(e, axis=-1, keepdims=True)).astype(BF16))
    y_mem = jnp.dot(jnp.concatenate(probs, axis=-1), vx_ref[...], preferred_element_type=F32)

    y = (jnp.dot(y_ref[...], wo_ref[0:c, :], preferred_element_type=F32)
         + jnp.dot(y_mem.astype(BF16), wo_ref[c:, :], preferred_element_type=F32))
    h1 = _layer_norm(alpha * h_ref[...] + y, g_ref[...], b_ref[...])
    h1_ref[...] = h1

    logits = lax.dot_general(wr_ref[...], h1.astype(BF16), (((1,), (1,)), ((), ())),
                             preferred_element_type=F32) + br_ref[...]
    e_iota = lax.broadcasted_iota(I32, (N_EXPERTS, tm), 0)
    vals, sels = [], []
    for _ in range(TOP_K):
        m = jnp.max(logits, axis=0, keepdims=True)
        sel = jnp.min(jnp.where(logits == m, e_iota, N_EXPERTS), axis=0, keepdims=True)
        vals.append(m)
        sels.append(sel)
        logits = jnp.where(e_iota == sel, -jnp.inf, logits)
    exps = [jnp.exp(v - vals[0]) for v in vals]
    denom = exps[0] + exps[1] + exps[2] + exps[3]
    gate_ref[...] = jnp.concatenate([e / denom for e in exps], axis=0)
    idx_ref[...] = jnp.concatenate(sels, axis=0)

    onehot = jnp.zeros((N_EXPERTS, tm), F32)
    for sel in sels:
        onehot = onehot + (e_iota == sel).astype(F32)
    earlier = (lax.broadcasted_iota(I32, (tm, tm), 0) < lax.broadcasted_iota(I32, (tm, tm), 1)).astype(BF16)
    rank_e = jnp.dot(onehot.astype(BF16), earlier, preferred_element_type=F32)
    rank_ref[...] = jnp.concatenate(
        [jnp.sum(jnp.where(e_iota == sel, rank_e, 0.0), axis=0, keepdims=True) for sel in sels],
        axis=0).astype(I32)
    cnt_ref[...] = jnp.sum(onehot, axis=1, keepdims=True).astype(I32)


def _mix_out(h, y_mix, qm_src, qm_block, kx, vx, wo_bf, ln_g, ln_b, wr_t_bf, b_router, alpha):
    bsz, s, d = h.shape
    c = y_mix.shape[-1]
    n_mem = vx.shape[1] // MEM_HEADS
    tm = _tile(s, TOKEN_TILE)
    n_s = s // tm
    n_tok = bsz * s
    tok_spec = pl.BlockSpec((TOP_K, tm), lambda b, i: (0, b * n_s + i))
    return pl.pallas_call(
        functools.partial(_mix_out_kernel, tm=tm, c=c, n_mem=n_mem, alpha=alpha),
        grid=(bsz, n_s),
        in_specs=[
            pl.BlockSpec((None, tm, d), lambda b, i: (b, i, 0)),
            pl.BlockSpec((None, tm, c), lambda b, i: (b, i, 0)),
            pl.BlockSpec((None, tm, MEM_WIDTH), lambda b, i: (b, i, qm_block)),
            pl.BlockSpec((None, MEM_WIDTH, MEM_HEADS * n_mem), lambda b, i: (b, 0, 0)),
            pl.BlockSpec((None, MEM_HEADS * n_mem, MEM_WIDTH), lambda b, i: (b, 0, 0)),
            pl.BlockSpec((c + MEM_WIDTH, d), lambda b, i: (0, 0)),
            pl.BlockSpec((1, d), lambda b, i: (0, 0)),
            pl.BlockSpec((1, d), lambda b, i: (0, 0)),
            pl.BlockSpec((N_EXPERTS, d), lambda b, i: (0, 0)),
            pl.BlockSpec((N_EXPERTS, 1), lambda b, i: (0, 0)),
        ],
        out_specs=[
            pl.BlockSpec((None, tm, d), lambda b, i: (b, i, 0)),
            tok_spec, tok_spec, tok_spec,
            pl.BlockSpec((None, N_EXPERTS, 1), lambda b, i: (b * n_s + i, 0, 0)),
        ],
        out_shape=[
            jax.ShapeDtypeStruct((bsz, s, d), F32),
            jax.ShapeDtypeStruct((TOP_K, n_tok), I32),
            jax.ShapeDtypeStruct((TOP_K, n_tok), F32),
            jax.ShapeDtypeStruct((TOP_K, n_tok), I32),
            jax.ShapeDtypeStruct((bsz * n_s, N_EXPERTS, 1), I32),
        ],
        compiler_params=pltpu.CompilerParams(
            dimension_semantics=("arbitrary", "arbitrary"), vmem_limit_bytes=VMEM_LIMIT),
        name="mix_out",
    )(h, y_mix, qm_src, kx, vx, wo_bf, ln_g.reshape(1, d), ln_b.reshape(1, d), wr_t_bf,
      b_router.reshape(N_EXPERTS, 1))


def _copy_rows(src_ref, src_row, dst_ref, dst_row, count, sem, max_bit, wait):
    for bit in range(max_bit, RUN_ALIGN.bit_length() - 2, -1):
        size = 1 << bit
        off = lax.shift_left(lax.shift_right_logical(count, bit + 1), bit + 1)

        @pl.when(jnp.bitwise_and(lax.shift_right_logical(count, bit), 1) == 1)
        def _():
            cp = pltpu.make_async_copy(src_ref.at[pl.ds(pl.multiple_of(src_row + off, RUN_ALIGN), size)],
                                       dst_ref.at[pl.ds(pl.multiple_of(dst_row + off, RUN_ALIGN), size)], sem)
            if wait:
                cp.wait()
            else:
                cp.start()


def _tile_runs(tile, tc_ref, ls_ref, base_ref, local_ref, global_ref, sem, max_bit, to_global, wait):
    if wait:
        last = tile * N_EXPERTS + N_EXPERTS - 1
        total = ls_ref[last] + tc_ref[last]
        total_bit = local_ref.shape[0].bit_length() - 1
        if to_global:
            _copy_rows(local_ref, 0, global_ref, 0, total, sem, total_bit, True)
        else:
            _copy_rows(global_ref, 0, local_ref, 0, total, sem, total_bit, True)
        return

    def per_expert(e, carry):
        k = tile * N_EXPERTS + e
        if to_global:
            _copy_rows(local_ref, ls_ref[k], global_ref, base_ref[k], tc_ref[k], sem, max_bit, wait)
        else:
            _copy_rows(global_ref, base_ref[k], local_ref, ls_ref[k], tc_ref[k], sem, max_bit, wait)
        return carry

    lax.fori_loop(0, N_EXPERTS, per_expert, 0)


def _dispatch_kernel(tc_ref, ls_ref, base_ref, fill_lo_ref, fill_hi_ref, na_ref,
                     pos_ref, h_ref, xs_ref, buf, zero_ref, sems, fill_sem, *, tm, tmb):
    t = pl.program_id(0)
    n_t = pl.num_programs(0)
    slot = lax.rem(t, 2)
    max_bit = tm.bit_length() - 1

    def runs(tile, slot_, wait):
        _tile_runs(tile, tc_ref, ls_ref, base_ref, buf.at[slot_], xs_ref, sems.at[slot_], max_bit, True, wait)

    @pl.when(t >= 2)
    def _():
        runs(t - 2, slot, True)

    x = h_ref[...].astype(BF16)
    pos = pos_ref[...]

    def sort_rows(i, carry):
        r0 = pl.multiple_of(i * SORT_ROWS, SORT_ROWS)
        row = r0 + lax.broadcasted_iota(I32, (SORT_ROWS, tm), 0)
        perm = jnp.zeros((SORT_ROWS, tm), F32)
        for k in range(TOP_K):
            perm = jnp.where(row == pos[k:k + 1, :], 1.0, perm)
        rows = jnp.dot(perm.astype(BF16), x, preferred_element_type=F32)
        buf[slot, pl.ds(r0, SORT_ROWS), :] = _pack_halves(rows)
        return carry

    lax.fori_loop(0, _sorted_rows(tm) // SORT_ROWS, sort_rows, 0)
    runs(t, slot, False)

    def zero_fill(wait):
        pad_bit = tmb.bit_length() - 2

        def per_expert(e, carry):
            _copy_rows(zero_ref, 0, xs_ref, fill_lo_ref[e], fill_hi_ref[e] - fill_lo_ref[e],
                       fill_sem, pad_bit, wait)
            return carry

        def per_block(blk, carry):
            cp = pltpu.make_async_copy(zero_ref, xs_ref.at[pl.ds(blk * tmb, tmb)], fill_sem)
            if wait:
                cp.wait()
            else:
                cp.start()
            return carry

        lax.fori_loop(0, N_EXPERTS, per_expert, 0)
        lax.fori_loop(na_ref[0], xs_ref.shape[0] // tmb, per_block, 0)

    @pl.when(t == 0)
    def _():
        zero_ref[...] = jnp.zeros_like(zero_ref)
        zero_fill(False)

    @pl.when(t == n_t - 1)
    def _():
        @pl.when(t >= 1)
        def _():
            runs(t - 1, 1 - slot, True)
        runs(t, slot, True)
        zero_fill(True)


def _dispatch(h1_flat, pos, tables, fill_lo, fill_hi, n_active, n_rows, tm, tmb):
    n_tok, d = h1_flat.shape
    n_t = n_tok // tm
    tc, ls, base = tables
    return pl.pallas_call(
        functools.partial(_dispatch_kernel, tm=tm, tmb=tmb),
        grid_spec=pltpu.PrefetchScalarGridSpec(
            num_scalar_prefetch=6,
            grid=(n_t,),
            in_specs=[
                pl.BlockSpec((TOP_K, tm), lambda i, *_: (0, i)),
                pl.BlockSpec((tm, d), lambda i, *_: (i, 0)),
            ],
            out_specs=pl.BlockSpec(memory_space=pl.ANY),
            scratch_shapes=[
                pltpu.VMEM((2, _sorted_rows(tm), d // 2), I32),
                pltpu.VMEM((tmb, d // 2), I32),
                pltpu.SemaphoreType.DMA((2,)),
                pltpu.SemaphoreType.DMA,
            ],
        ),
        out_shape=jax.ShapeDtypeStruct((n_rows, d // 2), I32),
        compiler_params=pltpu.CompilerParams(
            dimension_semantics=("arbitrary",), vmem_limit_bytes=VMEM_LIMIT),
        name="moe_dispatch",
    )(tc, ls, base, fill_lo, fill_hi, n_active, pos, h1_flat)


def _expert_kernel(ord_ref, seq_ref, nseq_ref, na_ref, valid_ref, xs_ref, wgu_hbm, bgu_ref, wd_hbm, bd_ref, ys_ref,
                   wgu_f32, wd_f32, wgu_bf, wd_bf, sems, *, f):
    i = pl.program_id(0)
    n_active = na_ref[0]
    j = jnp.minimum(i, n_active - 1)
    prev = jnp.maximum(j - 1, 0)
    o = ord_ref[j]
    new_expert = jnp.logical_or(i == 0, o != ord_ref[prev])

    def fetch(o_, slot, wait):
        e = seq_ref[o_]
        copies = []
        for src, dst, parts in ((wgu_hbm, wgu_f32, WEIGHT_DMA_PARTS), (wd_hbm, wd_f32, WEIGHT_DMA_PARTS // 2)):
            rows = src.shape[1] // parts
            for p in range(parts):
                copies.append(pltpu.make_async_copy(src.at[e, pl.ds(p * rows, rows)],
                                                    dst.at[slot, pl.ds(p * rows, rows)], sems.at[slot]))
        for cp in copies:
            if wait:
                cp.wait()
            else:
                cp.start()

    @pl.when(i == 0)
    def _():
        fetch(0, 0, False)

    @pl.when(jnp.logical_and(i < n_active, new_expert))
    def _():
        slot = lax.rem(o, 2)
        fetch(o, slot, True)

        @pl.when(o + 1 < nseq_ref[0])
        def _():
            fetch(o + 1, 1 - slot, False)

        rows = 128
        for r in range(0, wgu_bf.shape[0], rows):
            wgu_bf[r:r + rows, :] = wgu_f32[slot, r:r + rows, :].astype(BF16)
        for r in range(0, wd_bf.shape[0], rows):
            wd_bf[r:r + rows, :] = wd_f32[slot, r:r + rows, :].astype(BF16)

    valid = valid_ref[i]

    def rows_pass(r, n):
        x = _unpack_halves(xs_ref[r:r + n, :])
        gu = jnp.dot(x, wgu_bf[...], preferred_element_type=F32) + bgu_ref[...]
        gate = jnp.minimum(gu[:, :f], SWIGLU_LIMIT)
        up = jnp.clip(gu[:, f:], -SWIGLU_LIMIT, SWIGLU_LIMIT)
        act = (up + 1.0) * gate * _sigmoid(SWIGLU_ALPHA * gate)
        y = jnp.dot(act.astype(BF16), wd_bf[...], preferred_element_type=F32) + bd_ref[...]
        ys_ref[r:r + n, :] = _pack_halves(y.astype(BF16).astype(F32))

    def zero_rows(r, n):
        ys_ref[r:r + n, :] = jnp.zeros((n, ys_ref.shape[1]), I32)

    for r in range(0, xs_ref.shape[0], 2 * EXPERT_SUB):
        @pl.when(valid > r + EXPERT_SUB)
        def _():
            rows_pass(r, 2 * EXPERT_SUB)

        @pl.when(jnp.logical_and(valid > r, valid <= r + EXPERT_SUB))
        def _():
            rows_pass(r, EXPERT_SUB)
            zero_rows(r + EXPERT_SUB, EXPERT_SUB)

        @pl.when(valid <= r)
        def _():
            zero_rows(r, 2 * EXPERT_SUB)


def _experts(xs, block_ord, expert_seq, n_seq, n_active, block_valid, w_gate_up, b_gate_up, w_down, b_down, tmb):
    n_rows, half = xs.shape
    n_e, d, f2 = w_gate_up.shape
    f = f2 // 2
    nb = n_rows // tmb

    def blk(i, refs):
        return jnp.minimum(i, refs[3][0] - 1)

    def expert(i, refs):
        return refs[1][refs[0][blk(i, refs)]]

    return pl.pallas_call(
        functools.partial(_expert_kernel, f=f),
        grid_spec=pltpu.PrefetchScalarGridSpec(
            num_scalar_prefetch=5,
            grid=(nb,),
            in_specs=[
                pl.BlockSpec((tmb, half), lambda i, *refs: (blk(i, refs), 0)),
                pl.BlockSpec(memory_space=pl.ANY),
                pl.BlockSpec((None, 1, f2), lambda i, *refs: (expert(i, refs), 0, 0)),
                pl.BlockSpec(memory_space=pl.ANY),
                pl.BlockSpec((None, 1, d), lambda i, *refs: (expert(i, refs), 0, 0)),
            ],
            out_specs=pl.BlockSpec((tmb, half), lambda i, *refs: (i, 0)),
            scratch_shapes=[
                pltpu.VMEM((2, d, f2), F32), pltpu.VMEM((2, f, d), F32),
                pltpu.VMEM((d, f2), BF16), pltpu.VMEM((f, d), BF16),
                pltpu.SemaphoreType.DMA((2,)),
            ],
        ),
        out_shape=jax.ShapeDtypeStruct((n_rows, half), I32),
        compiler_params=pltpu.CompilerParams(
            dimension_semantics=("arbitrary",), vmem_limit_bytes=EXPERT_VMEM_LIMIT),
        name="moe_experts",
    )(block_ord, expert_seq, n_seq, n_active, block_valid, xs, w_gate_up, b_gate_up.reshape(n_e, 1, f2), w_down,
      b_down.reshape(n_e, 1, d))


def _combine_kernel(tc_ref, ls_ref, base_ref, pos_ref, gate_ref, h_ref, ys_ref, g_ref, b_ref, o_ref,
                    buf, sems, *, tm, alpha):
    t = pl.program_id(0)
    n_t = pl.num_programs(0)
    slot = lax.rem(t, 2)
    max_bit = tm.bit_length() - 1

    def runs(tile, slot_, wait):
        _tile_runs(tile, tc_ref, ls_ref, base_ref, buf.at[slot_], ys_ref, sems.at[slot_], max_bit, False, wait)

    @pl.when(t == 0)
    def _():
        buf[...] = jnp.zeros_like(buf)
        runs(t, slot, False)

    @pl.when(t + 1 < n_t)
    def _():
        runs(t + 1, 1 - slot, False)

    runs(t, slot, True)

    pos = pos_ref[...]
    gates = gate_ref[...]
    col = lax.broadcasted_iota(I32, (tm, _sorted_rows(tm)), 1)
    weights = jnp.zeros((tm, _sorted_rows(tm)), F32)
    for k in range(TOP_K):
        weights = jnp.where(col == pos[:, k:k + 1], gates[:, k:k + 1], weights)
    y = jnp.dot(weights.astype(BF16), _unpack_halves(buf[slot]), preferred_element_type=F32)
    o_ref[...] = _layer_norm(alpha * h_ref[...] + y, g_ref[...], b_ref[...])


def _combine(h1_flat, ys, pos_t, gates_t, tables, ln_g, ln_b, alpha, tm):
    n_tok, d = h1_flat.shape
    n_t = n_tok // tm
    tc, ls, base = tables
    return pl.pallas_call(
        functools.partial(_combine_kernel, tm=tm, alpha=alpha),
        grid_spec=pltpu.PrefetchScalarGridSpec(
            num_scalar_prefetch=3,
            grid=(n_t,),
            in_specs=[
                pl.BlockSpec((tm, TOP_K), lambda i, *_: (i, 0)),
                pl.BlockSpec((tm, TOP_K), lambda i, *_: (i, 0)),
                pl.BlockSpec((tm, d), lambda i, *_: (i, 0)),
                pl.BlockSpec(memory_space=pl.ANY),
                pl.BlockSpec((1, d), lambda i, *_: (0, 0)),
                pl.BlockSpec((1, d), lambda i, *_: (0, 0)),
            ],
            out_specs=pl.BlockSpec((tm, d), lambda i, *_: (i, 0)),
            scratch_shapes=[pltpu.VMEM((2, _sorted_rows(tm), d // 2), I32), pltpu.SemaphoreType.DMA((2,))],
        ),
        out_shape=jax.ShapeDtypeStruct((n_tok, d), F32),
        compiler_params=pltpu.CompilerParams(
            dimension_semantics=("arbitrary",), vmem_limit_bytes=VMEM_LIMIT),
        name="moe_combine",
    )(tc, ls, base, pos_t, gates_t, h1_flat, ys, ln_g.reshape(1, d), ln_b.reshape(1, d))


def _moe(h1, idx, gates, rank, tile_counts, layer, w_gate_up, b_gate_up, w_down, b_down, ln_g, ln_b, alpha):
    bsz, s, d = h1.shape
    n_tok = bsz * s
    tm = _tile(s, TOKEN_TILE)
    n_t = n_tok // tm
    tmb = EXPERT_ROWS
    n_rows = n_tok * TOP_K + n_t * N_EXPERTS * (RUN_ALIGN - 1) + N_EXPERTS * tmb
    n_rows = (n_rows + tmb - 1) // tmb * tmb
    nb = n_rows // tmb

    tile_counts = (tile_counts + RUN_ALIGN - 1) // RUN_ALIGN * RUN_ALIGN
    counts = jnp.sum(tile_counts, axis=0)
    padded = (counts + tmb - 1) // tmb * tmb
    pad_end = jnp.cumsum(padded).astype(I32)
    pad_start = pad_end - padded
    base = pad_start[None, :] + jnp.cumsum(tile_counts, axis=0) - tile_counts
    local = jnp.cumsum(tile_counts, axis=1) - tile_counts
    experts = jnp.arange(N_EXPERTS, dtype=I32)
    local_tok = jnp.repeat(local, tm, axis=0)
    pos = rank + jnp.sum(jnp.where(idx[:, :, None] == experts, local_tok[None], 0), axis=-1).astype(I32)
    block_start = jnp.arange(nb, dtype=I32) * tmb
    block_expert = jnp.minimum(
        jnp.sum((block_start[:, None] >= pad_end[None, :]).astype(I32), axis=1), N_EXPERTS - 1).astype(I32)
    n_active = (pad_end[-1:] // tmb).astype(I32)
    tables = (tile_counts.reshape(-1).astype(I32), local.reshape(-1).astype(I32), base.reshape(-1).astype(I32))

    h1_flat = h1.reshape(n_tok, d)
    xs = _dispatch(h1_flat, pos, tables, pad_start + counts, pad_end, n_active, n_rows, tm, tmb)
    merge = lambda w: w.reshape((w.shape[0] * w.shape[1],) + w.shape[2:])
    present = padded > 0
    expert_ord = jnp.cumsum(present.astype(I32)) - 1
    expert_seq = (jnp.argsort(jnp.logical_not(present), stable=True) + layer * N_EXPERTS).astype(I32)
    n_seq = jnp.sum(present.astype(I32)).reshape(1)
    block_valid = jnp.clip((pad_start + counts)[block_expert] - block_start, 0, tmb).astype(I32)
    ys = _experts(xs, expert_ord[block_expert].astype(I32), expert_seq, n_seq, n_active, block_valid,
                  merge(w_gate_up), merge(b_gate_up), merge(w_down), merge(b_down), tmb)
    out = _combine(h1_flat, ys, pos.T, gates.T, tables, ln_g, ln_b, alpha, tm)
    return out.reshape(bsz, s, d)


@jax.jit
def _forward(x, mem, mem_ln_g, mem_ln_b, w_mem_kv, w_in_conv, conv_dw_w, conv_dw_b, conv_ln_g,
             conv_ln_b, w_in_sb, w_mix_out, ln_mix_g, ln_mix_b, w_router, b_router, w_gate_up,
             b_gate_up, w_down, b_down, ln_moe_g, ln_moe_b):
    depth = w_mix_out.shape[0]
    alpha = float((2 * depth) ** 0.25)
    c = conv_dw_w.shape[-1]
    kx, vx = _mem_kv(mem, mem_ln_g, mem_ln_b, w_mem_kv.astype(BF16))

    h = x
    for i in range(depth):
        j = i // 2
        if i % 2 == 0:
            y_mix, qm = _conv_mix(h, w_in_conv[j].astype(BF16), conv_dw_w[j], conv_dw_b[j],
                                  conv_ln_g[j], conv_ln_b[j])
            qm_src, qm_block = qm, 0
        else:
            u = _sb_proj(h, w_in_sb[j].astype(BF16), c)
            y_mix = _sb_attn(u, c)
            qm_src, qm_block = u, (3 * c) // MEM_WIDTH
        h1, idx, gates, rank, cnt = _mix_out(
            h, y_mix, qm_src, qm_block, kx, vx, w_mix_out[i].astype(BF16), ln_mix_g[i], ln_mix_b[i],
            w_router[i].T.astype(BF16), b_router[i], alpha)
        h = _moe(h1, idx, gates, rank, cnt[:, :, 0], i, w_gate_up, b_gate_up, w_down, b_down,
                 ln_moe_g[i], ln_moe_b[i], alpha)
    return h


def kernel(x, mem, mem_ln_g, mem_ln_b, w_mem_kv, w_in_conv, conv_dw_w, conv_dw_b, conv_ln_g, conv_ln_b,
           w_in_sb, w_mix_out, ln_mix_g, ln_mix_b, w_router, b_router, w_gate_up, b_gate_up, w_down,
           b_down, ln_moe_g, ln_moe_b):
    return _forward(x, mem, mem_ln_g, mem_ln_b, w_mem_kv, w_in_conv, conv_dw_w, conv_dw_b, conv_ln_g,
                    conv_ln_b, w_in_sb, w_mix_out, ln_mix_g, ln_mix_b, w_router, b_router, w_gate_up,
                    b_gate_up, w_down, b_down, ln_moe_g, ln_moe_b)
```

```python
import functools

import jax
import jax.numpy as jnp
from jax import lax
from jax.experimental import pallas as pl
from jax.experimental.pallas import tpu as pltpu

F32 = jnp.float32
BF16 = jnp.bfloat16
I32 = jnp.int32

HEAD_DIM = 64
MEM_HEADS = 4
MEM_WIDTH = MEM_HEADS * HEAD_DIM
CONV_WIDTH = 31
N_EXPERTS = 32
TOP_K = 4
SWIGLU_LIMIT = 7.0
SWIGLU_ALPHA = 1.702
LN_EPS = 1e-5

LANES = 128
CONV_HALO = 32
CONV_STRIDE = 4
CONV_ROWS = 8 * CONV_STRIDE
SB_SKIP_BELOW = -104.0
LOG2_E = 1.4426950408889634
SB_PAIRS_PER_STEP = 6
SB_QUERY_ROWS = 256
SB_KEY_ROWS = 128
TOKEN_TILE = 512
MIX_ROWS = 256
EXPERT_ROWS = 1024
EXPERT_SUB = 256
SORT_ROWS = 256
RUN_ALIGN = 8


def _sorted_rows(tm):
    rows = TOP_K * tm + N_EXPERTS * RUN_ALIGN
    return (rows + SORT_ROWS - 1) // SORT_ROWS * SORT_ROWS
WEIGHT_DMA_PARTS = 4
VMEM_LIMIT = 48 * 1024 * 1024
EXPERT_VMEM_LIMIT = 56 * 1024 * 1024
HIGH_HALF = -65536


def _layer_norm(x, g, b):
    mu = jnp.mean(x, axis=-1, keepdims=True)
    xc = x - mu
    var = jnp.mean(xc * xc, axis=-1, keepdims=True)
    return xc * lax.rsqrt(var + LN_EPS) * g + b


def _sigmoid(x):
    return 1.0 / (1.0 + jnp.exp(-x))


def _tile(n, want):
    t = min(n, want)
    assert n % t == 0, (n, t)
    return t


def _pack_halves(x):
    n = x.shape[1] // 2
    lo = lax.bitcast_convert_type(x[:, :n], I32)
    hi = lax.bitcast_convert_type(x[:, n:], I32)
    return jnp.bitwise_or(jnp.bitwise_and(hi, HIGH_HALF), lax.shift_right_logical(lo, 16))


def _unpack_halves(p):
    lo = lax.bitcast_convert_type(lax.shift_left(p, 16), F32)
    hi = lax.bitcast_convert_type(jnp.bitwise_and(p, HIGH_HALF), F32)
    return jnp.concatenate([lo, hi], axis=1).astype(BF16)


def _mem_kv_kernel(mem_ref, g_ref, b_ref, w_ref, kx_ref, vx_ref):
    n_mem = mem_ref.shape[0]
    xn = _layer_norm(mem_ref[...], g_ref[...], b_ref[...])
    kv = jnp.dot(xn.astype(BF16), w_ref[...], preferred_element_type=F32)
    k_t = (kv[:, :MEM_WIDTH] * (HEAD_DIM ** -0.5)).T
    v = kv[:, MEM_WIDTH:]
    k_head = lax.broadcasted_iota(I32, (MEM_WIDTH, n_mem), 0) // HEAD_DIM
    v_head = lax.broadcasted_iota(I32, (n_mem, MEM_WIDTH), 1) // HEAD_DIM
    for hd in range(MEM_HEADS):
        kx_ref[:, hd * n_mem:(hd + 1) * n_mem] = jnp.where(k_head == hd, k_t, 0.0).astype(BF16)
        vx_ref[hd * n_mem:(hd + 1) * n_mem, :] = jnp.where(v_head == hd, v, 0.0).astype(BF16)


def _mem_kv(mem, g, b, w_bf):
    bsz, n_mem, d = mem.shape
    return pl.pallas_call(
        _mem_kv_kernel,
        grid=(bsz,),
        in_specs=[
            pl.BlockSpec((None, n_mem, d), lambda i: (i, 0, 0)),
            pl.BlockSpec((1, d), lambda i: (0, 0)),
            pl.BlockSpec((1, d), lambda i: (0, 0)),
            pl.BlockSpec((d, 2 * MEM_WIDTH), lambda i: (0, 0)),
        ],
        out_specs=[
            pl.BlockSpec((None, MEM_WIDTH, MEM_HEADS * n_mem), lambda i: (i, 0, 0)),
            pl.BlockSpec((None, MEM_HEADS * n_mem, MEM_WIDTH), lambda i: (i, 0, 0)),
        ],
        out_shape=[
            jax.ShapeDtypeStruct((bsz, MEM_WIDTH, MEM_HEADS * n_mem), BF16),
            jax.ShapeDtypeStruct((bsz, MEM_HEADS * n_mem, MEM_WIDTH), BF16),
        ],
        name="mem_kv",
    )(mem, g.reshape(1, d), b.reshape(1, d), w_bf)


def _conv_mix_kernel(h_ref, w_ref, dww_ref, dwb_ref, g_ref, b_ref, y_ref, qm_ref, gbuf, cbuf, *, tm, c):
    n_groups = c // LANES

    @pl.when(pl.program_id(1) == 0)
    def _():
        gbuf[:, 0:CONV_HALO, :] = jnp.zeros((n_groups, CONV_HALO, LANES), F32)

    @pl.when(pl.program_id(1) > 0)
    def _():
        gbuf[:, 0:CONV_HALO, :] = gbuf[:, tm:tm + CONV_HALO, :]

    u = jnp.dot(h_ref[...].astype(BF16), w_ref[...], preferred_element_type=F32)
    glu = u[:, :c] * _sigmoid(u[:, c:2 * c])
    for grp in range(n_groups):
        gbuf[grp, CONV_HALO:, :] = glu[:, grp * LANES:(grp + 1) * LANES]
    qm_ref[...] = u[:, 2 * c:].astype(BF16)

    first_tap = CONV_HALO - (CONV_WIDTH - 1)

    def rows(i, carry):
        r0 = pl.multiple_of(i * CONV_ROWS, CONV_ROWS)
        for grp in range(n_groups):
            src = gbuf.at[grp, pl.ds(r0, CONV_ROWS + CONV_HALO), :]
            dst = cbuf.at[grp, pl.ds(r0, CONV_ROWS), :]
            lanes = slice(grp * LANES, (grp + 1) * LANES)
            taps = [jnp.broadcast_to(dww_ref[j:j + 1, lanes], (8, LANES)) for j in range(CONV_WIDTH)]
            bias = jnp.broadcast_to(dwb_ref[:, lanes], (8, LANES))
            for r in range(CONV_STRIDE):
                acc = bias
                for j in range(CONV_WIDTH):
                    acc = acc + src[pl.ds(first_tap + j + r, 8, stride=CONV_STRIDE), :] * taps[j]
                dst[pl.ds(r, 8, stride=CONV_STRIDE), :] = acc
        conv = jnp.concatenate([cbuf[grp, pl.ds(r0, CONV_ROWS), :] for grp in range(n_groups)], axis=-1)
        yn = _layer_norm(conv, g_ref[...], b_ref[...])
        y_ref[pl.ds(r0, CONV_ROWS), :] = (yn * _sigmoid(yn)).astype(BF16)
        return carry

    lax.fori_loop(0, tm // CONV_ROWS, rows, 0)


def _conv_mix(h, w_bf, dw_w, dw_b, ln_g, ln_b):
    bsz, s, d = h.shape
    c = dw_w.shape[1]
    tm = _tile(s, TOKEN_TILE)
    n_out = w_bf.shape[1]
    row = lambda a: a.reshape(1, -1)
    return pl.pallas_call(
        functools.partial(_conv_mix_kernel, tm=tm, c=c),
        grid=(bsz, s // tm),
        in_specs=[
            pl.BlockSpec((None, tm, d), lambda b, i: (b, i, 0)),
            pl.BlockSpec((d, n_out), lambda b, i: (0, 0)),
            pl.BlockSpec((CONV_WIDTH, c), lambda b, i: (0, 0)),
            pl.BlockSpec((1, c), lambda b, i: (0, 0)),
            pl.BlockSpec((1, c), lambda b, i: (0, 0)),
            pl.BlockSpec((1, c), lambda b, i: (0, 0)),
        ],
        out_specs=[
            pl.BlockSpec((None, tm, c), lambda b, i: (b, i, 0)),
            pl.BlockSpec((None, tm, MEM_WIDTH), lambda b, i: (b, i, 0)),
        ],
        out_shape=[
            jax.ShapeDtypeStruct((bsz, s, c), BF16),
            jax.ShapeDtypeStruct((bsz, s, MEM_WIDTH), BF16),
        ],
        scratch_shapes=[pltpu.VMEM((c // LANES, tm + CONV_HALO, LANES), F32),
                        pltpu.VMEM((c // LANES, tm, LANES), F32)],
        compiler_params=pltpu.CompilerParams(
            dimension_semantics=("arbitrary", "arbitrary"), vmem_limit_bytes=VMEM_LIMIT),
        name="conv_mix",
    )(h, w_bf, dw_w, row(dw_b), row(ln_g), row(ln_b))


def _sb_proj_kernel(h_ref, w_ref, u_ref, *, c):
    u = jnp.dot(h_ref[...].astype(BF16), w_ref[...], preferred_element_type=F32)
    u_ref[:, :c] = (u[:, :c] * (HEAD_DIM ** -0.5)).astype(BF16)
    u_ref[:, c:] = u[:, c:].astype(BF16)


def _sb_proj(h, w_bf, c):
    bsz, s, d = h.shape
    tm = _tile(s, TOKEN_TILE)
    n_out = w_bf.shape[1]
    return pl.pallas_call(
        functools.partial(_sb_proj_kernel, c=c),
        grid=(bsz, s // tm),
        in_specs=[
            pl.BlockSpec((None, tm, d), lambda b, i: (b, i, 0)),
            pl.BlockSpec((d, n_out), lambda b, i: (0, 0)),
        ],
        out_specs=pl.BlockSpec((None, tm, n_out), lambda b, i: (b, i, 0)),
        out_shape=jax.ShapeDtypeStruct((bsz, s, n_out), BF16),
        compiler_params=pltpu.CompilerParams(
            dimension_semantics=("arbitrary", "arbitrary"), vmem_limit_bytes=VMEM_LIMIT),
        name="sb_proj",
    )(h, w_bf)


def _sb_attn_kernel(q_ref, k_ref, v_ref, o_ref, *, tq, tk, n_pairs):
    qi = pl.program_id(2)
    n_heads = 2 * n_pairs
    lane = lax.broadcasted_iota(I32, (tq, LANES), 1)
    k_lane = lax.broadcasted_iota(I32, (tk, LANES), 1)
    q_heads = []
    for pr in range(n_pairs):
        q2 = q_ref[:, pr * LANES:(pr + 1) * LANES]
        q_heads += [jnp.where(lane < HEAD_DIM, q2, jnp.zeros_like(q2)),
                    jnp.where(lane >= HEAD_DIM, q2, jnp.zeros_like(q2))]
    later = (lax.broadcasted_iota(I32, (tk, tk), 0) > lax.broadcasted_iota(I32, (tk, tk), 1)).astype(BF16)

    q_pos = qi * tq + lax.broadcasted_iota(I32, (tq, tk), 0)
    k_off = lax.broadcasted_iota(I32, (tq, tk), 1)

    def block(kb, carries, accs, masked):
        start = pl.multiple_of(kb * tk, tk)
        if masked:
            causal = (kb * tk + k_off) < q_pos
        log_nots, log_betas, parts = [], [], []
        for pr in range(n_pairs):
            k2 = k_ref[pl.ds(start, tk), pr * LANES:(pr + 1) * LANES]
            z2 = lax.dot_general(jnp.concatenate(q_heads[2 * pr:2 * pr + 2], axis=0), k2,
                                 (((1,), (1,)), ((), ())), preferred_element_type=F32)
            for half in range(2):
                z = z2[half * tq:(half + 1) * tq] * LOG2_E
                sp = jnp.maximum(z, 0.0) + jnp.log2(1.0 + jnp.exp2(-jnp.abs(z)))
                log_not = -sp
                if masked:
                    log_not = jnp.where(causal, log_not, 0.0)
                hi = log_not.astype(BF16)
                parts += [hi, (log_not - hi.astype(F32)).astype(BF16)]
                log_nots.append(log_not)
                log_betas.append(z - sp)
        sums = jnp.dot(jnp.concatenate(parts, axis=0), later, preferred_element_type=F32)
        new_c, new_a = [], []
        for pr in range(n_pairs):
            v2 = v_ref[pl.ds(start, tk), pr * LANES:(pr + 1) * LANES]
            ws = []
            for half in range(2):
                hd = 2 * pr + half
                tail = sums[2 * hd * tq:(2 * hd + 1) * tq] + sums[(2 * hd + 1) * tq:(2 * hd + 2) * tq] + carries[hd]
                w = jnp.exp2(log_betas[hd] + tail)
                if masked:
                    w = jnp.where(causal, w, 0.0)
                ws.append(w.astype(BF16))
                new_c.append(carries[hd] + jnp.sum(log_nots[hd], axis=1, keepdims=True))
            v_rows = jnp.concatenate([jnp.where(k_lane < HEAD_DIM, v2, jnp.zeros_like(v2)),
                                      jnp.where(k_lane >= HEAD_DIM, v2, jnp.zeros_like(v2))], axis=0)
            new_a.append(accs[pr] + jnp.dot(jnp.concatenate(ws, axis=1), v_rows, preferred_element_type=F32))
        return new_c, new_a

    carries = [jnp.zeros((tq, 1), F32)] * n_heads
    accs = [jnp.zeros((tq, LANES), F32)] * n_pairs
    n_diag = tq // tk
    for j in range(n_diag):
        carries, accs = block(qi * n_diag + (n_diag - 1 - j), carries, accs, True)

    def live(cs):
        top = cs[0]
        for cc in cs[1:]:
            top = jnp.maximum(top, cc)
        return (jnp.max(top) > SB_SKIP_BELOW * LOG2_E).astype(I32)

    def cond(state):
        kb, go = state[0], state[1]
        return jnp.logical_and(kb >= 0, go > 0)

    def body(state):
        kb = state[0]
        cs, acs = block(kb, list(state[2:2 + n_heads]), list(state[2 + n_heads:]), False)
        return (kb - 1, live(cs), *cs, *acs)

    state = lax.while_loop(cond, body, (qi * n_diag - 1, live(carries), *carries, *accs))
    o_ref[...] = jnp.concatenate(state[2 + n_heads:], axis=1).astype(BF16)


def _sb_attn(u, c):
    bsz, s, _ = u.shape
    tq = _tile(s, SB_QUERY_ROWS)
    tk = _tile(tq, SB_KEY_ROWS)
    width = SB_PAIRS_PER_STEP * LANES
    n_groups = c // width
    return pl.pallas_call(
        functools.partial(_sb_attn_kernel, tq=tq, tk=tk, n_pairs=SB_PAIRS_PER_STEP),
        grid=(bsz, n_groups, s // tq),
        in_specs=[
            pl.BlockSpec((None, tq, width), lambda b, p, i: (b, i, p)),
            pl.BlockSpec((None, s, width), lambda b, p, i: (b, 0, n_groups + p), pipeline_mode=pl.Buffered(1)),
            pl.BlockSpec((None, s, width), lambda b, p, i: (b, 0, 2 * n_groups + p), pipeline_mode=pl.Buffered(1)),
        ],
        out_specs=pl.BlockSpec((None, tq, width), lambda b, p, i: (b, i, p)),
        out_shape=jax.ShapeDtypeStruct((bsz, s, c), BF16),
        compiler_params=pltpu.CompilerParams(
            dimension_semantics=("arbitrary", "arbitrary", "arbitrary"), vmem_limit_bytes=VMEM_LIMIT),
        name="sb_attn",
    )(u, u, u)


def _mix_out_kernel(h_ref, y_ref, qm_ref, kx_ref, vx_ref, wo_ref, g_ref, b_ref, wr_ref, br_ref,
                    h1_ref, idx_ref, gate_ref, rank_ref, cnt_ref, *, tm, c, n_mem, alpha):
    def rows(i, carry):
        sl = pl.ds(pl.multiple_of(i * MIX_ROWS, MIX_ROWS), MIX_ROWS)
        s = jnp.dot(qm_ref[sl, :], kx_ref[...], preferred_element_type=F32)
        probs = []
        for hd in range(MEM_HEADS):
            sh = s[:, hd * n_mem:(hd + 1) * n_mem]
            e = jnp.exp(sh - jnp.max(sh, axis=-1, keepdims=True))
            probs.append((e / jnp.sum(e, axis=-1, keepdims=True)).astype(BF16))
        y_mem = jnp.dot(jnp.concatenate(probs, axis=-1), vx_ref[...], preferred_element_type=F32)
        y = (jnp.dot(y_ref[sl, :], wo_ref[0:c, :], preferred_element_type=F32)
             + jnp.dot(y_mem.astype(BF16), wo_ref[c:, :], preferred_element_type=F32))
        h1_ref[sl, :] = _layer_norm(alpha * h_ref[sl, :] + y, g_ref[...], b_ref[...])
        return carry

    lax.fori_loop(0, tm // MIX_ROWS, rows, 0)
    h1 = h1_ref[...]

    logits = lax.dot_general(wr_ref[...], h1.astype(BF16), (((1,), (1,)), ((), ())),
                             preferred_element_type=F32) + br_ref[...]
    e_iota = lax.broadcasted_iota(I32, (N_EXPERTS, tm), 0)
    vals, sels = [], []
    for _ in range(TOP_K):
        m = jnp.max(logits, axis=0, keepdims=True)
        sel = jnp.min(jnp.where(logits == m, e_iota, N_EXPERTS), axis=0, keepdims=True)
        vals.append(m)
        sels.append(sel)
        logits = jnp.where(e_iota == sel, -jnp.inf, logits)
    exps = [jnp.exp(v - vals[0]) for v in vals]
    denom = exps[0] + exps[1] + exps[2] + exps[3]
    gate_ref[...] = jnp.concatenate([e / denom for e in exps], axis=0)
    idx_ref[...] = jnp.concatenate(sels, axis=0)

    onehot = jnp.zeros((N_EXPERTS, tm), F32)
    for sel in sels:
        onehot = onehot + (e_iota == sel).astype(F32)
    earlier = (lax.broadcasted_iota(I32, (tm, tm), 0) < lax.broadcasted_iota(I32, (tm, tm), 1)).astype(BF16)
    rank_e = jnp.dot(onehot.astype(BF16), earlier, preferred_element_type=F32)
    rank_ref[...] = jnp.concatenate(
        [jnp.sum(jnp.where(e_iota == sel, rank_e, 0.0), axis=0, keepdims=True) for sel in sels],
        axis=0).astype(I32)
    cnt_ref[...] = jnp.sum(onehot, axis=1, keepdims=True).astype(I32)


def _mix_out(h, y_mix, qm_src, qm_block, kx, vx, wo_bf, ln_g, ln_b, wr_t_bf, b_router, alpha):
    bsz, s, d = h.shape
    c = y_mix.shape[-1]
    n_mem = vx.shape[1] // MEM_HEADS
    tm = _tile(s, TOKEN_TILE)
    n_s = s // tm
    n_tok = bsz * s
    tok_spec = pl.BlockSpec((TOP_K, tm), lambda b, i: (0, b * n_s + i))
    return pl.pallas_call(
        functools.partial(_mix_out_kernel, tm=tm, c=c, n_mem=n_mem, alpha=alpha),
        grid=(bsz, n_s),
        in_specs=[
            pl.BlockSpec((None, tm, d), lambda b, i: (b, i, 0)),
            pl.BlockSpec((None, tm, c), lambda b, i: (b, i, 0)),
            pl.BlockSpec((None, tm, MEM_WIDTH), lambda b, i: (b, i, qm_block)),
            pl.BlockSpec((None, MEM_WIDTH, MEM_HEADS * n_mem), lambda b, i: (b, 0, 0)),
            pl.BlockSpec((None, MEM_HEADS * n_mem, MEM_WIDTH), lambda b, i: (b, 0, 0)),
            pl.BlockSpec((c + MEM_WIDTH, d), lambda b, i: (0, 0)),
            pl.BlockSpec((1, d), lambda b, i: (0, 0)),
            pl.BlockSpec((1, d), lambda b, i: (0, 0)),
            pl.BlockSpec((N_EXPERTS, d), lambda b, i: (0, 0)),
            pl.BlockSpec((N_EXPERTS, 1), lambda b, i: (0, 0)),
        ],
        out_specs=[
            pl.BlockSpec((None, tm, d), lambda b, i: (b, i, 0)),
            tok_spec, tok_spec, tok_spec,
            pl.BlockSpec((None, N_EXPERTS, 1), lambda b, i: (b * n_s + i, 0, 0)),
        ],
        out_shape=[
            jax.ShapeDtypeStruct((bsz, s, d), F32),
            jax.ShapeDtypeStruct((TOP_K, n_tok), I32),
            jax.ShapeDtypeStruct((TOP_K, n_tok), F32),
            jax.ShapeDtypeStruct((TOP_K, n_tok), I32),
            jax.ShapeDtypeStruct((bsz * n_s, N_EXPERTS, 1), I32),
        ],
        compiler_params=pltpu.CompilerParams(
            dimension_semantics=("arbitrary", "arbitrary"), vmem_limit_bytes=VMEM_LIMIT),
        name="mix_out",
    )(h, y_mix, qm_src, kx, vx, wo_bf, ln_g.reshape(1, d), ln_b.reshape(1, d), wr_t_bf,
      b_router.reshape(N_EXPERTS, 1))


def _copy_rows(src_ref, src_row, dst_ref, dst_row, count, sem, max_bit, wait):
    for bit in range(max_bit, RUN_ALIGN.bit_length() - 2, -1):
        size = 1 << bit
        off = lax.shift_left(lax.shift_right_logical(count, bit + 1), bit + 1)

        @pl.when(jnp.bitwise_and(lax.shift_right_logical(count, bit), 1) == 1)
        def _():
            cp = pltpu.make_async_copy(src_ref.at[pl.ds(pl.multiple_of(src_row + off, RUN_ALIGN), size)],
                                       dst_ref.at[pl.ds(pl.multiple_of(dst_row + off, RUN_ALIGN), size)], sem)
            if wait:
                cp.wait()
            else:
                cp.start()


def _tile_runs(tile, tc_ref, ls_ref, base_ref, local_ref, global_ref, sem, max_bit, to_global, wait):
    if wait:
        last = tile * N_EXPERTS + N_EXPERTS - 1
        total = ls_ref[last] + tc_ref[last]
        total_bit = local_ref.shape[0].bit_length() - 1
        if to_global:
            _copy_rows(local_ref, 0, global_ref, 0, total, sem, total_bit, True)
        else:
            _copy_rows(global_ref, 0, local_ref, 0, total, sem, total_bit, True)
        return

    def per_expert(e, carry):
        k = tile * N_EXPERTS + e
        if to_global:
            _copy_rows(local_ref, ls_ref[k], global_ref, base_ref[k], tc_ref[k], sem, max_bit, wait)
        else:
            _copy_rows(global_ref, base_ref[k], local_ref, ls_ref[k], tc_ref[k], sem, max_bit, wait)
        return carry

    lax.fori_loop(0, N_EXPERTS, per_expert, 0)


def _dispatch_kernel(tc_ref, ls_ref, base_ref, fill_lo_ref, fill_hi_ref, na_ref,
                     pos_ref, h_ref, xs_ref, buf, zero_ref, sems, fill_sem, *, tm, tmb):
    t = pl.program_id(0)
    n_t = pl.num_programs(0)
    slot = lax.rem(t, 2)
    max_bit = tm.bit_length() - 1

    def runs(tile, slot_, wait):
        _tile_runs(tile, tc_ref, ls_ref, base_ref, buf.at[slot_], xs_ref, sems.at[slot_], max_bit, True, wait)

    @pl.when(t >= 2)
    def _():
        runs(t - 2, slot, True)

    x = h_ref[...].astype(BF16)
    pos = pos_ref[...]

    def sort_rows(i, carry):
        r0 = pl.multiple_of(i * SORT_ROWS, SORT_ROWS)
        row = r0 + lax.broadcasted_iota(I32, (SORT_ROWS, tm), 0)
        perm = jnp.zeros((SORT_ROWS, tm), F32)
        for k in range(TOP_K):
            perm = jnp.where(row == pos[k:k + 1, :], 1.0, perm)
        rows = jnp.dot(perm.astype(BF16), x, preferred_element_type=F32)
        buf[slot, pl.ds(r0, SORT_ROWS), :] = _pack_halves(rows)
        return carry

    lax.fori_loop(0, _sorted_rows(tm) // SORT_ROWS, sort_rows, 0)
    runs(t, slot, False)

    def zero_fill(wait):
        pad_bit = tmb.bit_length() - 2

        def per_expert(e, carry):
            _copy_rows(zero_ref, 0, xs_ref, fill_lo_ref[e], fill_hi_ref[e] - fill_lo_ref[e],
                       fill_sem, pad_bit, wait)
            return carry

        def per_block(blk, carry):
            cp = pltpu.make_async_copy(zero_ref, xs_ref.at[pl.ds(blk * tmb, tmb)], fill_sem)
            if wait:
                cp.wait()
            else:
                cp.start()
            return carry

        lax.fori_loop(0, N_EXPERTS, per_expert, 0)
        lax.fori_loop(na_ref[0], xs_ref.shape[0] // tmb, per_block, 0)

    @pl.when(t == 0)
    def _():
        zero_ref[...] = jnp.zeros_like(zero_ref)
        zero_fill(False)

    @pl.when(t == n_t - 1)
    def _():
        @pl.when(t >= 1)
        def _():
            runs(t - 1, 1 - slot, True)
        runs(t, slot, True)
        zero_fill(True)


def _dispatch(h1_flat, pos, tables, fill_lo, fill_hi, n_active, n_rows, tm, tmb):
    n_tok, d = h1_flat.shape
    n_t = n_tok // tm
    tc, ls, base = tables
    return pl.pallas_call(
        functools.partial(_dispatch_kernel, tm=tm, tmb=tmb),
        grid_spec=pltpu.PrefetchScalarGridSpec(
            num_scalar_prefetch=6,
            grid=(n_t,),
            in_specs=[
                pl.BlockSpec((TOP_K, tm), lambda i, *_: (0, i)),
                pl.BlockSpec((tm, d), lambda i, *_: (i, 0)),
            ],
            out_specs=pl.BlockSpec(memory_space=pl.ANY),
            scratch_shapes=[
                pltpu.VMEM((2, _sorted_rows(tm), d // 2), I32),
                pltpu.VMEM((tmb, d // 2), I32),
                pltpu.SemaphoreType.DMA((2,)),
                pltpu.SemaphoreType.DMA,
            ],
        ),
        out_shape=jax.ShapeDtypeStruct((n_rows, d // 2), I32),
        compiler_params=pltpu.CompilerParams(
            dimension_semantics=("arbitrary",), vmem_limit_bytes=VMEM_LIMIT),
        name="moe_dispatch",
    )(tc, ls, base, fill_lo, fill_hi, n_active, pos, h1_flat)


def _expert_kernel(ord_ref, seq_ref, nseq_ref, na_ref, valid_ref, xs_ref, wgu_hbm, bgu_ref, wd_hbm, bd_ref, ys_ref,
                   wgu_f32, wd_f32, wgu_bf, wd_bf, sems, *, f):
    i = pl.program_id(0)
    n_active = na_ref[0]
    j = jnp.minimum(i, n_active - 1)
    prev = jnp.maximum(j - 1, 0)
    o = ord_ref[j]
    new_expert = jnp.logical_or(i == 0, o != ord_ref[prev])

    def fetch(o_, slot, wait):
        e = seq_ref[o_]
        copies = []
        for src, dst, parts in ((wgu_hbm, wgu_f32, WEIGHT_DMA_PARTS), (wd_hbm, wd_f32, WEIGHT_DMA_PARTS // 2)):
            rows = src.shape[1] // parts
            for p in range(parts):
                copies.append(pltpu.make_async_copy(src.at[e, pl.ds(p * rows, rows)],
                                                    dst.at[slot, pl.ds(p * rows, rows)], sems.at[slot]))
        for cp in copies:
            if wait:
                cp.wait()
            else:
                cp.start()

    @pl.when(i == 0)
    def _():
        fetch(0, 0, False)

    @pl.when(jnp.logical_and(i < n_active, new_expert))
    def _():
        slot = lax.rem(o, 2)
        fetch(o, slot, True)

        @pl.when(o + 1 < nseq_ref[0])
        def _():
            fetch(o + 1, 1 - slot, False)

        rows = 128
        for r in range(0, wgu_bf.shape[0], rows):
            wgu_bf[r:r + rows, :] = wgu_f32[slot, r:r + rows, :].astype(BF16)
        for r in range(0, wd_bf.shape[0], rows):
            wd_bf[r:r + rows, :] = wd_f32[slot, r:r + rows, :].astype(BF16)

    valid = valid_ref[i]

    def rows_pass(r, n):
        x = _unpack_halves(xs_ref[r:r + n, :])
        gu = jnp.dot(x, wgu_bf[...], preferred_element_type=F32) + bgu_ref[...]
        gate = jnp.minimum(gu[:, :f], SWIGLU_LIMIT)
        up = jnp.clip(gu[:, f:], -SWIGLU_LIMIT, SWIGLU_LIMIT)
        act = (up + 1.0) * gate * _sigmoid(SWIGLU_ALPHA * gate)
        y = jnp.dot(act.astype(BF16), wd_bf[...], preferred_element_type=F32) + bd_ref[...]
        ys_ref[r:r + n, :] = _pack_halves(y.astype(BF16).astype(F32))

    def zero_rows(r, n):
        ys_ref[r:r + n, :] = jnp.zeros((n, ys_ref.shape[1]), I32)

    for r in range(0, xs_ref.shape[0], 2 * EXPERT_SUB):
        @pl.when(valid > r + EXPERT_SUB)
        def _():
            rows_pass(r, 2 * EXPERT_SUB)

        @pl.when(jnp.logical_and(valid > r, valid <= r + EXPERT_SUB))
        def _():
            rows_pass(r, EXPERT_SUB)
            zero_rows(r + EXPERT_SUB, EXPERT_SUB)

        @pl.when(valid <= r)
        def _():
            zero_rows(r, 2 * EXPERT_SUB)


def _experts(xs, block_ord, expert_seq, n_seq, n_active, block_valid, w_gate_up, b_gate_up, w_down, b_down, tmb):
    n_rows, half = xs.shape
    n_e, d, f2 = w_gate_up.shape
    f = f2 // 2
    nb = n_rows // tmb

    def blk(i, refs):
        return jnp.minimum(i, refs[3][0] - 1)

    def expert(i, refs):
        return refs[1][refs[0][blk(i, refs)]]

    return pl.pallas_call(
        functools.partial(_expert_kernel, f=f),
        grid_spec=pltpu.PrefetchScalarGridSpec(
            num_scalar_prefetch=5,
            grid=(nb,),
            in_specs=[
                pl.BlockSpec((tmb, half), lambda i, *refs: (blk(i, refs), 0)),
                pl.BlockSpec(memory_space=pl.ANY),
                pl.BlockSpec((None, 1, f2), lambda i, *refs: (expert(i, refs), 0, 0)),
                pl.BlockSpec(memory_space=pl.ANY),
                pl.BlockSpec((None, 1, d), lambda i, *refs: (expert(i, refs), 0, 0)),
            ],
            out_specs=pl.BlockSpec((tmb, half), lambda i, *refs: (i, 0)),
            scratch_shapes=[
                pltpu.VMEM((2, d, f2), F32), pltpu.VMEM((2, f, d), F32),
                pltpu.VMEM((d, f2), BF16), pltpu.VMEM((f, d), BF16),
                pltpu.SemaphoreType.DMA((2,)),
            ],
        ),
        out_shape=jax.ShapeDtypeStruct((n_rows, half), I32),
        compiler_params=pltpu.CompilerParams(
            dimension_semantics=("arbitrary",), vmem_limit_bytes=EXPERT_VMEM_LIMIT),
        name="moe_experts",
    )(block_ord, expert_seq, n_seq, n_active, block_valid, xs, w_gate_up, b_gate_up.reshape(n_e, 1, f2), w_down,
      b_down.reshape(n_e, 1, d))


def _combine_kernel(tc_ref, ls_ref, base_ref, pos_ref, gate_ref, h_ref, ys_ref, g_ref, b_ref, o_ref,
                    buf, sems, *, tm, alpha):
    t = pl.program_id(0)
    n_t = pl.num_programs(0)
    slot = lax.rem(t, 2)
    max_bit = tm.bit_length() - 1

    def runs(tile, slot_, wait):
        _tile_runs(tile, tc_ref, ls_ref, base_ref, buf.at[slot_], ys_ref, sems.at[slot_], max_bit, False, wait)

    @pl.when(t == 0)
    def _():
        buf[...] = jnp.zeros_like(buf)
        runs(t, slot, False)

    @pl.when(t + 1 < n_t)
    def _():
        runs(t + 1, 1 - slot, False)

    runs(t, slot, True)

    pos = pos_ref[...]
    gates = gate_ref[...]
    col = lax.broadcasted_iota(I32, (tm, _sorted_rows(tm)), 1)
    weights = jnp.zeros((tm, _sorted_rows(tm)), F32)
    for k in range(TOP_K):
        weights = jnp.where(col == pos[:, k:k + 1], gates[:, k:k + 1], weights)
    y = jnp.dot(weights.astype(BF16), _unpack_halves(buf[slot]), preferred_element_type=F32)
    o_ref[...] = _layer_norm(alpha * h_ref[...] + y, g_ref[...], b_ref[...])


def _combine(h1_flat, ys, pos_t, gates_t, tables, ln_g, ln_b, alpha, tm):
    n_tok, d = h1_flat.shape
    n_t = n_tok // tm
    tc, ls, base = tables
    return pl.pallas_call(
        functools.partial(_combine_kernel, tm=tm, alpha=alpha),
        grid_spec=pltpu.PrefetchScalarGridSpec(
            num_scalar_prefetch=3,
            grid=(n_t,),
            in_specs=[
                pl.BlockSpec((tm, TOP_K), lambda i, *_: (i, 0)),
                pl.BlockSpec((tm, TOP_K), lambda i, *_: (i, 0)),
                pl.BlockSpec((tm, d), lambda i, *_: (i, 0)),
                pl.BlockSpec(memory_space=pl.ANY),
                pl.BlockSpec((1, d), lambda i, *_: (0, 0)),
                pl.BlockSpec((1, d), lambda i, *_: (0, 0)),
            ],
            out_specs=pl.BlockSpec((tm, d), lambda i, *_: (i, 0)),
            scratch_shapes=[pltpu.VMEM((2, _sorted_rows(tm), d // 2), I32), pltpu.SemaphoreType.DMA((2,))],
        ),
        out_shape=jax.ShapeDtypeStruct((n_tok, d), F32),
        compiler_params=pltpu.CompilerParams(
            dimension_semantics=("arbitrary",), vmem_limit_bytes=VMEM_LIMIT),
        name="moe_combine",
    )(tc, ls, base, pos_t, gates_t, h1_flat, ys, ln_g.reshape(1, d), ln_b.reshape(1, d))


def _moe(h1, idx, gates, rank, tile_counts, layer, w_gate_up, b_gate_up, w_down, b_down, ln_g, ln_b, alpha):
    bsz, s, d = h1.shape
    n_tok = bsz * s
    tm = _tile(s, TOKEN_TILE)
    n_t = n_tok // tm
    tmb = EXPERT_ROWS
    n_rows = n_tok * TOP_K + n_t * N_EXPERTS * (RUN_ALIGN - 1) + N_EXPERTS * tmb
    n_rows = (n_rows + tmb - 1) // tmb * tmb
    nb = n_rows // tmb

    tile_counts = (tile_counts + RUN_ALIGN - 1) // RUN_ALIGN * RUN_ALIGN
    counts = jnp.sum(tile_counts, axis=0)
    padded = (counts + tmb - 1) // tmb * tmb
    pad_end = jnp.cumsum(padded).astype(I32)
    pad_start = pad_end - padded
    base = pad_start[None, :] + jnp.cumsum(tile_counts, axis=0) - tile_counts
    local = jnp.cumsum(tile_counts, axis=1) - tile_counts
    experts = jnp.arange(N_EXPERTS, dtype=I32)
    local_tok = jnp.repeat(local, tm, axis=0)
    pos = rank + jnp.sum(jnp.where(idx[:, :, None] == experts, local_tok[None], 0), axis=-1).astype(I32)
    block_start = jnp.arange(nb, dtype=I32) * tmb
    block_expert = jnp.minimum(
        jnp.sum((block_start[:, None] >= pad_end[None, :]).astype(I32), axis=1), N_EXPERTS - 1).astype(I32)
    n_active = (pad_end[-1:] // tmb).astype(I32)
    tables = (tile_counts.reshape(-1).astype(I32), local.reshape(-1).astype(I32), base.reshape(-1).astype(I32))

    h1_flat = h1.reshape(n_tok, d)
    xs = _dispatch(h1_flat, pos, tables, pad_start + counts, pad_end, n_active, n_rows, tm, tmb)
    merge = lambda w: w.reshape((w.shape[0] * w.shape[1],) + w.shape[2:])
    present = padded > 0
    expert_ord = jnp.cumsum(present.astype(I32)) - 1
    expert_seq = (jnp.argsort(jnp.logical_not(present), stable=True) + layer * N_EXPERTS).astype(I32)
    n_seq = jnp.sum(present.astype(I32)).reshape(1)
    block_valid = jnp.clip((pad_start + counts)[block_expert] - block_start, 0, tmb).astype(I32)
    ys = _experts(xs, expert_ord[block_expert].astype(I32), expert_seq, n_seq, n_active, block_valid,
                  merge(w_gate_up), merge(b_gate_up), merge(w_down), merge(b_down), tmb)
    out = _combine(h1_flat, ys, pos.T, gates.T, tables, ln_g, ln_b, alpha, tm)
    return out.reshape(bsz, s, d)


@jax.jit
def _forward(x, mem, mem_ln_g, mem_ln_b, w_mem_kv, w_in_conv, conv_dw_w, conv_dw_b, conv_ln_g,
             conv_ln_b, w_in_sb, w_mix_out, ln_mix_g, ln_mix_b, w_router, b_router, w_gate_up,
             b_gate_up, w_down, b_down, ln_moe_g, ln_moe_b):
    depth = w_mix_out.shape[0]
    alpha = float((2 * depth) ** 0.25)
    c = conv_dw_w.shape[-1]
    kx, vx = _mem_kv(mem, mem_ln_g, mem_ln_b, w_mem_kv.astype(BF16))

    h = x
    for i in range(depth):
        j = i // 2
        if i % 2 == 0:
            y_mix, qm = _conv_mix(h, w_in_conv[j].astype(BF16), conv_dw_w[j], conv_dw_b[j],
                                  conv_ln_g[j], conv_ln_b[j])
            qm_src, qm_block = qm, 0
        else:
            u = _sb_proj(h, w_in_sb[j].astype(BF16), c)
            y_mix = _sb_attn(u, c)
            qm_src, qm_block = u, (3 * c) // MEM_WIDTH
        h1, idx, gates, rank, cnt = _mix_out(
            h, y_mix, qm_src, qm_block, kx, vx, w_mix_out[i].astype(BF16), ln_mix_g[i], ln_mix_b[i],
            w_router[i].T.astype(BF16), b_router[i], alpha)
        h = _moe(h1, idx, gates, rank, cnt[:, :, 0], i, w_gate_up, b_gate_up, w_down, b_down,
                 ln_moe_g[i], ln_moe_b[i], alpha)
    return h


def kernel(x, mem, mem_ln_g, mem_ln_b, w_mem_kv, w_in_conv, conv_dw_w, conv_dw_b, conv_ln_g, conv_ln_b,
           w_in_sb, w_mix_out, ln_mix_g, ln_mix_b, w_router, b_router, w_gate_up, b_gate_up, w_down,
           b_down, ln_moe_g, ln_moe_b):
    return _forward(x, mem, mem_ln_g, mem_ln_b, w_mem_kv, w_in_conv, conv_dw_w, conv_dw_b, conv_ln_g,
                    conv_ln_b, w_in_sb, w_mix_out, ln_mix_g, ln_mix_b, w_router, b_router, w_gate_up,
                    b_gate_up, w_down, b_down, ln_moe_g, ln_moe_b)
```

```python
import functools

import jax
import jax.numpy as jnp
from jax import lax
from jax.experimental import pallas as pl
from jax.experimental.pallas import tpu as pltpu

F32 = jnp.float32
BF16 = jnp.bfloat16
I32 = jnp.int32

HEAD_DIM = 64
MEM_HEADS = 4
MEM_WIDTH = MEM_HEADS * HEAD_DIM
CONV_WIDTH = 31
N_EXPERTS = 32
TOP_K = 4
SWIGLU_LIMIT = 7.0
SWIGLU_ALPHA = 1.702
LN_EPS = 1e-5

LANES = 128
CONV_HALO = 32
CONV_STRIDE = 4
CONV_ROWS = 8 * CONV_STRIDE
SB_SKIP_BELOW = -104.0
LOG2_E = 1.4426950408889634
SB_PAIRS_PER_STEP = 6
SB_QUERY_ROWS = 256
SB_KEY_ROWS = 128
SB_FEW_ROWS = 64
TOKEN_TILE = 512
MIX_ROWS = 256
EXPERT_ROWS = 1024
EXPERT_SUB = 256
SORT_ROWS = 256
RUN_ALIGN = 8


def _sorted_rows(tm):
    rows = TOP_K * tm + N_EXPERTS * RUN_ALIGN
    return (rows + SORT_ROWS - 1) // SORT_ROWS * SORT_ROWS
WEIGHT_DMA_PARTS = 4
VMEM_LIMIT = 48 * 1024 * 1024
EXPERT_VMEM_LIMIT = 56 * 1024 * 1024
HIGH_HALF = -65536


def _layer_norm(x, g, b):
    mu = jnp.mean(x, axis=-1, keepdims=True)
    xc = x - mu
    var = jnp.mean(xc * xc, axis=-1, keepdims=True)
    return xc * lax.rsqrt(var + LN_EPS) * g + b


def _sigmoid(x):
    return 1.0 / (1.0 + jnp.exp(-x))


def _tile(n, want):
    t = min(n, want)
    assert n % t == 0, (n, t)
    return t


def _pack_halves(x):
    n = x.shape[1] // 2
    lo = lax.bitcast_convert_type(x[:, :n], I32)
    hi = lax.bitcast_convert_type(x[:, n:], I32)
    return jnp.bitwise_or(jnp.bitwise_and(hi, HIGH_HALF), lax.shift_right_logical(lo, 16))


def _unpack_halves(p):
    lo = lax.bitcast_convert_type(lax.shift_left(p, 16), F32)
    hi = lax.bitcast_convert_type(jnp.bitwise_and(p, HIGH_HALF), F32)
    return jnp.concatenate([lo, hi], axis=1).astype(BF16)


def _mem_kv_kernel(mem_ref, g_ref, b_ref, w_ref, kx_ref, vx_ref):
    n_mem = mem_ref.shape[0]
    xn = _layer_norm(mem_ref[...], g_ref[...], b_ref[...])
    kv = jnp.dot(xn.astype(BF16), w_ref[...], preferred_element_type=F32)
    k_t = (kv[:, :MEM_WIDTH] * (HEAD_DIM ** -0.5)).T
    v = kv[:, MEM_WIDTH:]
    k_head = lax.broadcasted_iota(I32, (MEM_WIDTH, n_mem), 0) // HEAD_DIM
    v_head = lax.broadcasted_iota(I32, (n_mem, MEM_WIDTH), 1) // HEAD_DIM
    for hd in range(MEM_HEADS):
        kx_ref[:, hd * n_mem:(hd + 1) * n_mem] = jnp.where(k_head == hd, k_t, 0.0).astype(BF16)
        vx_ref[hd * n_mem:(hd + 1) * n_mem, :] = jnp.where(v_head == hd, v, 0.0).astype(BF16)


def _mem_kv(mem, g, b, w_bf):
    bsz, n_mem, d = mem.shape
    return pl.pallas_call(
        _mem_kv_kernel,
        grid=(bsz,),
        in_specs=[
            pl.BlockSpec((None, n_mem, d), lambda i: (i, 0, 0)),
            pl.BlockSpec((1, d), lambda i: (0, 0)),
            pl.BlockSpec((1, d), lambda i: (0, 0)),
            pl.BlockSpec((d, 2 * MEM_WIDTH), lambda i: (0, 0)),
        ],
        out_specs=[
            pl.BlockSpec((None, MEM_WIDTH, MEM_HEADS * n_mem), lambda i: (i, 0, 0)),
            pl.BlockSpec((None, MEM_HEADS * n_mem, MEM_WIDTH), lambda i: (i, 0, 0)),
        ],
        out_shape=[
            jax.ShapeDtypeStruct((bsz, MEM_WIDTH, MEM_HEADS * n_mem), BF16),
            jax.ShapeDtypeStruct((bsz, MEM_HEADS * n_mem, MEM_WIDTH), BF16),
        ],
        name="mem_kv",
    )(mem, g.reshape(1, d), b.reshape(1, d), w_bf)


def _conv_mix_kernel(h_ref, w_ref, dww_ref, dwb_ref, g_ref, b_ref, y_ref, qm_ref, gbuf, cbuf, *, tm, c):
    n_groups = c // LANES

    @pl.when(pl.program_id(1) == 0)
    def _():
        gbuf[:, 0:CONV_HALO, :] = jnp.zeros((n_groups, CONV_HALO, LANES), F32)

    @pl.when(pl.program_id(1) > 0)
    def _():
        gbuf[:, 0:CONV_HALO, :] = gbuf[:, tm:tm + CONV_HALO, :]

    u = jnp.dot(h_ref[...].astype(BF16), w_ref[...], preferred_element_type=F32)
    glu = u[:, :c] * _sigmoid(u[:, c:2 * c])
    for grp in range(n_groups):
        gbuf[grp, CONV_HALO:, :] = glu[:, grp * LANES:(grp + 1) * LANES]
    qm_ref[...] = u[:, 2 * c:].astype(BF16)

    first_tap = CONV_HALO - (CONV_WIDTH - 1)

    def rows(i, carry):
        r0 = pl.multiple_of(i * CONV_ROWS, CONV_ROWS)
        for grp in range(n_groups):
            src = gbuf.at[grp, pl.ds(r0, CONV_ROWS + CONV_HALO), :]
            dst = cbuf.at[grp, pl.ds(r0, CONV_ROWS), :]
            lanes = slice(grp * LANES, (grp + 1) * LANES)
            taps = [jnp.broadcast_to(dww_ref[j:j + 1, lanes], (8, LANES)) for j in range(CONV_WIDTH)]
            bias = jnp.broadcast_to(dwb_ref[:, lanes], (8, LANES))
            for r in range(CONV_STRIDE):
                acc = bias
                for j in range(CONV_WIDTH):
                    acc = acc + src[pl.ds(first_tap + j + r, 8, stride=CONV_STRIDE), :] * taps[j]
                dst[pl.ds(r, 8, stride=CONV_STRIDE), :] = acc
        conv = jnp.concatenate([cbuf[grp, pl.ds(r0, CONV_ROWS), :] for grp in range(n_groups)], axis=-1)
        yn = _layer_norm(conv, g_ref[...], b_ref[...])
        y_ref[pl.ds(r0, CONV_ROWS), :] = (yn * _sigmoid(yn)).astype(BF16)
        return carry

    lax.fori_loop(0, tm // CONV_ROWS, rows, 0)


def _conv_mix(h, w_bf, dw_w, dw_b, ln_g, ln_b):
    bsz, s, d = h.shape
    c = dw_w.shape[1]
    tm = _tile(s, TOKEN_TILE)
    n_out = w_bf.shape[1]
    row = lambda a: a.reshape(1, -1)
    return pl.pallas_call(
        functools.partial(_conv_mix_kernel, tm=tm, c=c),
        grid=(bsz, s // tm),
        in_specs=[
            pl.BlockSpec((None, tm, d), lambda b, i: (b, i, 0)),
            pl.BlockSpec((d, n_out), lambda b, i: (0, 0)),
            pl.BlockSpec((CONV_WIDTH, c), lambda b, i: (0, 0)),
            pl.BlockSpec((1, c), lambda b, i: (0, 0)),
            pl.BlockSpec((1, c), lambda b, i: (0, 0)),
            pl.BlockSpec((1, c), lambda b, i: (0, 0)),
        ],
        out_specs=[
            pl.BlockSpec((None, tm, c), lambda b, i: (b, i, 0)),
            pl.BlockSpec((None, tm, MEM_WIDTH), lambda b, i: (b, i, 0)),
        ],
        out_shape=[
            jax.ShapeDtypeStruct((bsz, s, c), BF16),
            jax.ShapeDtypeStruct((bsz, s, MEM_WIDTH), BF16),
        ],
        scratch_shapes=[pltpu.VMEM((c // LANES, tm + CONV_HALO, LANES), F32),
                        pltpu.VMEM((c // LANES, tm, LANES), F32)],
        compiler_params=pltpu.CompilerParams(
            dimension_semantics=("arbitrary", "arbitrary"), vmem_limit_bytes=VMEM_LIMIT),
        name="conv_mix",
    )(h, w_bf, dw_w, row(dw_b), row(ln_g), row(ln_b))


def _sb_proj_kernel(h_ref, w_ref, u_ref, *, c):
    u = jnp.dot(h_ref[...].astype(BF16), w_ref[...], preferred_element_type=F32)
    u_ref[:, :c] = (u[:, :c] * (HEAD_DIM ** -0.5)).astype(BF16)
    u_ref[:, c:] = u[:, c:].astype(BF16)


def _sb_proj(h, w_bf, c):
    bsz, s, d = h.shape
    tm = _tile(s, TOKEN_TILE)
    n_out = w_bf.shape[1]
    return pl.pallas_call(
        functools.partial(_sb_proj_kernel, c=c),
        grid=(bsz, s // tm),
        in_specs=[
            pl.BlockSpec((None, tm, d), lambda b, i: (b, i, 0)),
            pl.BlockSpec((d, n_out), lambda b, i: (0, 0)),
        ],
        out_specs=pl.BlockSpec((None, tm, n_out), lambda b, i: (b, i, 0)),
        out_shape=jax.ShapeDtypeStruct((bsz, s, n_out), BF16),
        compiler_params=pltpu.CompilerParams(
            dimension_semantics=("arbitrary", "arbitrary"), vmem_limit_bytes=VMEM_LIMIT),
        name="sb_proj",
    )(h, w_bf)


def _sb_attn_kernel(q_ref, k_ref, v_ref, o_ref, carry_ref, acc_ref, *, tq, tk, n_pairs):
    qi = pl.program_id(2)
    n_heads = 2 * n_pairs
    lane = lax.broadcasted_iota(I32, (tq, LANES), 1)
    k_lane = lax.broadcasted_iota(I32, (tk, LANES), 1)
    q_heads = []
    for pr in range(n_pairs):
        q2 = q_ref[:, pr * LANES:(pr + 1) * LANES]
        q_heads += [jnp.where(lane < HEAD_DIM, q2, jnp.zeros_like(q2)),
                    jnp.where(lane >= HEAD_DIM, q2, jnp.zeros_like(q2))]
    later = (lax.broadcasted_iota(I32, (tk, tk), 0) > lax.broadcasted_iota(I32, (tk, tk), 1)).astype(BF16)

    def block(kb, masked, r0, r1):
        n = r1 - r0
        start = pl.multiple_of(kb * tk, tk)
        if masked:
            causal = ((kb * tk + lax.broadcasted_iota(I32, (n, tk), 1))
                      < (qi * tq + r0 + lax.broadcasted_iota(I32, (n, tk), 0)))
        log_nots, log_betas, parts = [], [], []
        for pr in range(n_pairs):
            k2 = k_ref[pl.ds(start, tk), pr * LANES:(pr + 1) * LANES]
            z2 = lax.dot_general(jnp.concatenate([q[r0:r1] for q in q_heads[2 * pr:2 * pr + 2]], axis=0), k2,
                                 (((1,), (1,)), ((), ())), preferred_element_type=F32)
            for half in range(2):
                z = z2[half * n:(half + 1) * n] * LOG2_E
                sp = jnp.maximum(z, 0.0) + jnp.log2(1.0 + jnp.exp2(-jnp.abs(z)))
                log_not = -sp
                if masked:
                    log_not = jnp.where(causal, log_not, 0.0)
                hi = log_not.astype(BF16)
                parts += [hi, (log_not - hi.astype(F32)).astype(BF16)]
                log_nots.append(log_not)
                log_betas.append(z - sp)
        sums = jnp.dot(jnp.concatenate(parts, axis=0), later, preferred_element_type=F32)
        for pr in range(n_pairs):
            v2 = v_ref[pl.ds(start, tk), pr * LANES:(pr + 1) * LANES]
            ws = []
            for half in range(2):
                hd = 2 * pr + half
                carry = carry_ref[hd, r0:r1, :]
                tail = sums[2 * hd * n:(2 * hd + 1) * n] + sums[(2 * hd + 1) * n:(2 * hd + 2) * n] + carry
                w = jnp.exp2(log_betas[hd] + tail)
                if masked:
                    w = jnp.where(causal, w, 0.0)
                ws.append(w.astype(BF16))
                carry_ref[hd, r0:r1, :] = carry + jnp.sum(log_nots[hd], axis=1, keepdims=True)
            v_rows = jnp.concatenate([jnp.where(k_lane < HEAD_DIM, v2, jnp.zeros_like(v2)),
                                      jnp.where(k_lane >= HEAD_DIM, v2, jnp.zeros_like(v2))], axis=0)
            acc_ref[pr, r0:r1, :] += jnp.dot(jnp.concatenate(ws, axis=1), v_rows, preferred_element_type=F32)

    carry_ref[...] = jnp.zeros_like(carry_ref)
    acc_ref[...] = jnp.zeros_like(acc_ref)
    n_diag = tq // tk
    for j in range(n_diag - 1, -1, -1):
        block(qi * n_diag + j, True, j * tk, tq)

    row_count = 1 + lax.broadcasted_iota(I32, (tq, tk), 0)

    def live_rows():
        top = carry_ref[0]
        for hd in range(1, n_heads):
            top = jnp.maximum(top, carry_ref[hd])
        return jnp.max(jnp.where(top > SB_SKIP_BELOW * LOG2_E, row_count, 0))

    def cond(state):
        kb, n_live = state
        return jnp.logical_and(kb >= 0, n_live > 0)

    def body(state):
        kb, n_live = state

        @pl.when(n_live <= SB_FEW_ROWS)
        def _():
            block(kb, False, 0, SB_FEW_ROWS)

        @pl.when(n_live > SB_FEW_ROWS)
        def _():
            block(kb, False, 0, tq)

        return kb - 1, live_rows()

    lax.while_loop(cond, body, (qi * n_diag - 1, live_rows()))
    o_ref[...] = jnp.concatenate([acc_ref[pr] for pr in range(n_pairs)], axis=1).astype(BF16)


def _sb_attn(u, c):
    bsz, s, _ = u.shape
    tq = _tile(s, SB_QUERY_ROWS)
    tk = _tile(tq, SB_KEY_ROWS)
    width = SB_PAIRS_PER_STEP * LANES
    n_groups = c // width
    return pl.pallas_call(
        functools.partial(_sb_attn_kernel, tq=tq, tk=tk, n_pairs=SB_PAIRS_PER_STEP),
        grid=(bsz, n_groups, s // tq),
        in_specs=[
            pl.BlockSpec((None, tq, width), lambda b, p, i: (b, i, p)),
            pl.BlockSpec((None, s, width), lambda b, p, i: (b, 0, n_groups + p), pipeline_mode=pl.Buffered(1)),
            pl.BlockSpec((None, s, width), lambda b, p, i: (b, 0, 2 * n_groups + p), pipeline_mode=pl.Buffered(1)),
        ],
        out_specs=pl.BlockSpec((None, tq, width), lambda b, p, i: (b, i, p)),
        out_shape=jax.ShapeDtypeStruct((bsz, s, c), BF16),
        scratch_shapes=[pltpu.VMEM((2 * SB_PAIRS_PER_STEP, tq, tk), F32),
                        pltpu.VMEM((SB_PAIRS_PER_STEP, tq, LANES), F32)],
        compiler_params=pltpu.CompilerParams(
            dimension_semantics=("arbitrary", "arbitrary", "arbitrary"), vmem_limit_bytes=VMEM_LIMIT),
        name="sb_attn",
    )(u, u, u)


def _mix_out_kernel(h_ref, y_ref, qm_ref, kx_ref, vx_ref, wo_ref, g_ref, b_ref, wr_ref, br_ref,
                    h1_ref, idx_ref, gate_ref, rank_ref, cnt_ref, *, tm, c, n_mem, alpha):
    def rows(i, carry):
        sl = pl.ds(pl.multiple_of(i * MIX_ROWS, MIX_ROWS), MIX_ROWS)
        s = jnp.dot(qm_ref[sl, :], kx_ref[...], preferred_element_type=F32)
        probs = []
        for hd in range(MEM_HEADS):
            sh = s[:, hd * n_mem:(hd + 1) * n_mem]
            e = jnp.exp(sh - jnp.max(sh, axis=-1, keepdims=True))
            probs.append((e / jnp.sum(e, axis=-1, keepdims=True)).astype(BF16))
        y_mem = jnp.dot(jnp.concatenate(probs, axis=-1), vx_ref[...], preferred_element_type=F32)
        y = (jnp.dot(y_ref[sl, :], wo_ref[0:c, :], preferred_element_type=F32)
             + jnp.dot(y_mem.astype(BF16), wo_ref[c:, :], preferred_element_type=F32))
        h1_ref[sl, :] = _layer_norm(alpha * h_ref[sl, :] + y, g_ref[...], b_ref[...])
        return carry

    lax.fori_loop(0, tm // MIX_ROWS, rows, 0)
    h1 = h1_ref[...]

    logits = lax.dot_general(wr_ref[...], h1.astype(BF16), (((1,), (1,)), ((), ())),
                             preferred_element_type=F32) + br_ref[...]
    e_iota = lax.broadcasted_iota(I32, (N_EXPERTS, tm), 0)
    vals, sels = [], []
    for _ in range(TOP_K):
        m = jnp.max(logits, axis=0, keepdims=True)
        sel = jnp.min(jnp.where(logits == m, e_iota, N_EXPERTS), axis=0, keepdims=True)
        vals.append(m)
        sels.append(sel)
        logits = jnp.where(e_iota == sel, -jnp.inf, logits)
    exps = [jnp.exp(v - vals[0]) for v in vals]
    denom = exps[0] + exps[1] + exps[2] + exps[3]
    gate_ref[...] = jnp.concatenate([e / denom for e in exps], axis=0)
    idx_ref[...] = jnp.concatenate(sels, axis=0)

    onehot = jnp.zeros((N_EXPERTS, tm), F32)
    for sel in sels:
        onehot = onehot + (e_iota == sel).astype(F32)
    earlier = (lax.broadcasted_iota(I32, (tm, tm), 0) < lax.broadcasted_iota(I32, (tm, tm), 1)).astype(BF16)
    rank_e = jnp.dot(onehot.astype(BF16), earlier, preferred_element_type=F32)
    rank_ref[...] = jnp.concatenate(
        [jnp.sum(jnp.where(e_iota == sel, rank_e, 0.0), axis=0, keepdims=True) for sel in sels],
        axis=0).astype(I32)
    cnt_ref[...] = jnp.sum(onehot, axis=1, keepdims=True).astype(I32)


def _mix_out(h, y_mix, qm_src, qm_block, kx, vx, wo_bf, ln_g, ln_b, wr_t_bf, b_router, alpha):
    bsz, s, d = h.shape
    c = y_mix.shape[-1]
    n_mem = vx.shape[1] // MEM_HEADS
    tm = _tile(s, TOKEN_TILE)
    n_s = s // tm
    n_tok = bsz * s
    tok_spec = pl.BlockSpec((TOP_K, tm), lambda b, i: (0, b * n_s + i))
    return pl.pallas_call(
        functools.partial(_mix_out_kernel, tm=tm, c=c, n_mem=n_mem, alpha=alpha),
        grid=(bsz, n_s),
        in_specs=[
            pl.BlockSpec((None, tm, d), lambda b, i: (b, i, 0)),
            pl.BlockSpec((None, tm, c), lambda b, i: (b, i, 0)),
            pl.BlockSpec((None, tm, MEM_WIDTH), lambda b, i: (b, i, qm_block)),
            pl.BlockSpec((None, MEM_WIDTH, MEM_HEADS * n_mem), lambda b, i: (b, 0, 0)),
            pl.BlockSpec((None, MEM_HEADS * n_mem, MEM_WIDTH), lambda b, i: (b, 0, 0)),
            pl.BlockSpec((c + MEM_WIDTH, d), lambda b, i: (0, 0)),
            pl.BlockSpec((1, d), lambda b, i: (0, 0)),
            pl.BlockSpec((1, d), lambda b, i: (0, 0)),
            pl.BlockSpec((N_EXPERTS, d), lambda b, i: (0, 0)),
            pl.BlockSpec((N_EXPERTS, 1), lambda b, i: (0, 0)),
        ],
        out_specs=[
            pl.BlockSpec((None, tm, d), lambda b, i: (b, i, 0)),
            tok_spec, tok_spec, tok_spec,
            pl.BlockSpec((None, N_EXPERTS, 1), lambda b, i: (b * n_s + i, 0, 0)),
        ],
        out_shape=[
            jax.ShapeDtypeStruct((bsz, s, d), F32),
            jax.ShapeDtypeStruct((TOP_K, n_tok), I32),
            jax.ShapeDtypeStruct((TOP_K, n_tok), F32),
            jax.ShapeDtypeStruct((TOP_K, n_tok), I32),
            jax.ShapeDtypeStruct((bsz * n_s, N_EXPERTS, 1), I32),
        ],
        compiler_params=pltpu.CompilerParams(
            dimension_semantics=("arbitrary", "arbitrary"), vmem_limit_bytes=VMEM_LIMIT),
        name="mix_out",
    )(h, y_mix, qm_src, kx, vx, wo_bf, ln_g.reshape(1, d), ln_b.reshape(1, d), wr_t_bf,
      b_router.reshape(N_EXPERTS, 1))


def _copy_rows(src_ref, src_row, dst_ref, dst_row, count, sem, max_bit, wait):
    for bit in range(max_bit, RUN_ALIGN.bit_length() - 2, -1):
        size = 1 << bit
        off = lax.shift_left(lax.shift_right_logical(count, bit + 1), bit + 1)

        @pl.when(jnp.bitwise_and(lax.shift_right_logical(count, bit), 1) == 1)
        def _():
            cp = pltpu.make_async_copy(src_ref.at[pl.ds(pl.multiple_of(src_row + off, RUN_ALIGN), size)],
                                       dst_ref.at[pl.ds(pl.multiple_of(dst_row + off, RUN_ALIGN), size)], sem)
            if wait:
                cp.wait()
            else:
                cp.start()


def _tile_runs(tile, tc_ref, ls_ref, base_ref, local_ref, global_ref, sem, max_bit, to_global, wait):
    if wait:
        last = tile * N_EXPERTS + N_EXPERTS - 1
        total = ls_ref[last] + tc_ref[last]
        total_bit = local_ref.shape[0].bit_length() - 1
        if to_global:
            _copy_rows(local_ref, 0, global_ref, 0, total, sem, total_bit, True)
        else:
            _copy_rows(global_ref, 0, local_ref, 0, total, sem, total_bit, True)
        return

    def per_expert(e, carry):
        k = tile * N_EXPERTS + e
        if to_global:
            _copy_rows(local_ref, ls_ref[k], global_ref, base_ref[k], tc_ref[k], sem, max_bit, wait)
        else:
            _copy_rows(global_ref, base_ref[k], local_ref, ls_ref[k], tc_ref[k], sem, max_bit, wait)
        return carry

    lax.fori_loop(0, N_EXPERTS, per_expert, 0)


def _dispatch_kernel(tc_ref, ls_ref, base_ref, fill_lo_ref, fill_hi_ref, na_ref,
                     pos_ref, h_ref, xs_ref, buf, zero_ref, sems, fill_sem, *, tm, tmb):
    t = pl.program_id(0)
    n_t = pl.num_programs(0)
    slot = lax.rem(t, 2)
    max_bit = tm.bit_length() - 1

    def runs(tile, slot_, wait):
        _tile_runs(tile, tc_ref, ls_ref, base_ref, buf.at[slot_], xs_ref, sems.at[slot_], max_bit, True, wait)

    @pl.when(t >= 2)
    def _():
        runs(t - 2, slot, True)

    x = h_ref[...].astype(BF16)
    pos = pos_ref[...]

    def sort_rows(i, carry):
        r0 = pl.multiple_of(i * SORT_ROWS, SORT_ROWS)
        row = r0 + lax.broadcasted_iota(I32, (SORT_ROWS, tm), 0)
        perm = jnp.zeros((SORT_ROWS, tm), F32)
        for k in range(TOP_K):
            perm = jnp.where(row == pos[k:k + 1, :], 1.0, perm)
        rows = jnp.dot(perm.astype(BF16), x, preferred_element_type=F32)
        buf[slot, pl.ds(r0, SORT_ROWS), :] = _pack_halves(rows)
        return carry

    lax.fori_loop(0, _sorted_rows(tm) // SORT_ROWS, sort_rows, 0)
    runs(t, slot, False)

    def zero_fill(wait):
        pad_bit = tmb.bit_length() - 2

        def per_expert(e, carry):
            _copy_rows(zero_ref, 0, xs_ref, fill_lo_ref[e], fill_hi_ref[e] - fill_lo_ref[e],
                       fill_sem, pad_bit, wait)
            return carry

        def per_block(blk, carry):
            cp = pltpu.make_async_copy(zero_ref, xs_ref.at[pl.ds(blk * tmb, tmb)], fill_sem)
            if wait:
                cp.wait()
            else:
                cp.start()
            return carry

        lax.fori_loop(0, N_EXPERTS, per_expert, 0)
        lax.fori_loop(na_ref[0], xs_ref.shape[0] // tmb, per_block, 0)

    @pl.when(t == 0)
    def _():
        zero_ref[...] = jnp.zeros_like(zero_ref)
        zero_fill(False)

    @pl.when(t == n_t - 1)
    def _():
        @pl.when(t >= 1)
        def _():
            runs(t - 1, 1 - slot, True)
        runs(t, slot, True)
        zero_fill(True)


def _dispatch(h1_flat, pos, tables, fill_lo, fill_hi, n_active, n_rows, tm, tmb):
    n_tok, d = h1_flat.shape
    n_t = n_tok // tm
    tc, ls, base = tables
    return pl.pallas_call(
        functools.partial(_dispatch_kernel, tm=tm, tmb=tmb),
        grid_spec=pltpu.PrefetchScalarGridSpec(
            num_scalar_prefetch=6,
            grid=(n_t,),
            in_specs=[
                pl.BlockSpec((TOP_K, tm), lambda i, *_: (0, i)),
                pl.BlockSpec((tm, d), lambda i, *_: (i, 0)),
            ],
            out_specs=pl.BlockSpec(memory_space=pl.ANY),
            scratch_shapes=[
                pltpu.VMEM((2, _sorted_rows(tm), d // 2), I32),
                pltpu.VMEM((tmb, d // 2), I32),
                pltpu.SemaphoreType.DMA((2,)),
                pltpu.SemaphoreType.DMA,
            ],
        ),
        out_shape=jax.ShapeDtypeStruct((n_rows, d // 2), I32),
        compiler_params=pltpu.CompilerParams(
            dimension_semantics=("arbitrary",), vmem_limit_bytes=VMEM_LIMIT),
        name="moe_dispatch",
    )(tc, ls, base, fill_lo, fill_hi, n_active, pos, h1_flat)


def _expert_kernel(ord_ref, seq_ref, nseq_ref, na_ref, valid_ref, xs_ref, wgu_hbm, bgu_ref, wd_hbm, bd_ref, ys_ref,
                   wgu_f32, wd_f32, wgu_bf, wd_bf, sems, *, f):
    i = pl.program_id(0)
    n_active = na_ref[0]
    j = jnp.minimum(i, n_active - 1)
    prev = jnp.maximum(j - 1, 0)
    o = ord_ref[j]
    new_expert = jnp.logical_or(i == 0, o != ord_ref[prev])

    def fetch(o_, slot, wait):
        e = seq_ref[o_]
        copies = []
        for src, dst, parts in ((wgu_hbm, wgu_f32, WEIGHT_DMA_PARTS), (wd_hbm, wd_f32, WEIGHT_DMA_PARTS // 2)):
            rows = src.shape[1] // parts
            for p in range(parts):
                copies.append(pltpu.make_async_copy(src.at[e, pl.ds(p * rows, rows)],
                                                    dst.at[slot, pl.ds(p * rows, rows)], sems.at[slot]))
        for cp in copies:
            if wait:
                cp.wait()
            else:
                cp.start()

    @pl.when(i == 0)
    def _():
        fetch(0, 0, False)

    @pl.when(jnp.logical_and(i < n_active, new_expert))
    def _():
        slot = lax.rem(o, 2)
        fetch(o, slot, True)

        @pl.when(o + 1 < nseq_ref[0])
        def _():
            fetch(o + 1, 1 - slot, False)

        rows = 128
        for r in range(0, wgu_bf.shape[0], rows):
            wgu_bf[r:r + rows, :] = wgu_f32[slot, r:r + rows, :].astype(BF16)
        for r in range(0, wd_bf.shape[0], rows):
            wd_bf[r:r + rows, :] = wd_f32[slot, r:r + rows, :].astype(BF16)

    valid = valid_ref[i]

    def rows_pass(r, n):
        x = _unpack_halves(xs_ref[r:r + n, :])
        gu = jnp.dot(x, wgu_bf[...], preferred_element_type=F32) + bgu_ref[...]
        gate = jnp.minimum(gu[:, :f], SWIGLU_LIMIT)
        up = jnp.clip(gu[:, f:], -SWIGLU_LIMIT, SWIGLU_LIMIT)
        act = (up + 1.0) * gate * _sigmoid(SWIGLU_ALPHA * gate)
        y = jnp.dot(act.astype(BF16), wd_bf[...], preferred_element_type=F32) + bd_ref[...]
        ys_ref[r:r + n, :] = _pack_halves(y.astype(BF16).astype(F32))

    def zero_rows(r, n):
        ys_ref[r:r + n, :] = jnp.zeros((n, ys_ref.shape[1]), I32)

    for r in range(0, xs_ref.shape[0], 2 * EXPERT_SUB):
        @pl.when(valid > r + EXPERT_SUB)
        def _():
            rows_pass(r, 2 * EXPERT_SUB)

        @pl.when(jnp.logical_and(valid > r, valid <= r + EXPERT_SUB))
        def _():
            rows_pass(r, EXPERT_SUB)
            zero_rows(r + EXPERT_SUB, EXPERT_SUB)

        @pl.when(valid <= r)
        def _():
            zero_rows(r, 2 * EXPERT_SUB)


def _experts(xs, block_ord, expert_seq, n_seq, n_active, block_valid, w_gate_up, b_gate_up, w_down, b_down, tmb):
    n_rows, half = xs.shape
    n_e, d, f2 = w_gate_up.shape
    f = f2 // 2
    nb = n_rows // tmb

    def blk(i, refs):
        return jnp.minimum(i, refs[3][0] - 1)

    def expert(i, refs):
        return refs[1][refs[0][blk(i, refs)]]

    return pl.pallas_call(
        functools.partial(_expert_kernel, f=f),
        grid_spec=pltpu.PrefetchScalarGridSpec(
            num_scalar_prefetch=5,
            grid=(nb,),
            in_specs=[
                pl.BlockSpec((tmb, half), lambda i, *refs: (blk(i, refs), 0)),
                pl.BlockSpec(memory_space=pl.ANY),
                pl.BlockSpec((None, 1, f2), lambda i, *refs: (expert(i, refs), 0, 0)),
                pl.BlockSpec(memory_space=pl.ANY),
                pl.BlockSpec((None, 1, d), lambda i, *refs: (expert(i, refs), 0, 0)),
            ],
            out_specs=pl.BlockSpec((tmb, half), lambda i, *refs: (i, 0)),
            scratch_shapes=[
                pltpu.VMEM((2, d, f2), F32), pltpu.VMEM((2, f, d), F32),
                pltpu.VMEM((d, f2), BF16), pltpu.VMEM((f, d), BF16),
                pltpu.SemaphoreType.DMA((2,)),
            ],
        ),
        out_shape=jax.ShapeDtypeStruct((n_rows, half), I32),
        compiler_params=pltpu.CompilerParams(
            dimension_semantics=("arbitrary",), vmem_limit_bytes=EXPERT_VMEM_LIMIT),
        name="moe_experts",
    )(block_ord, expert_seq, n_seq, n_active, block_valid, xs, w_gate_up, b_gate_up.reshape(n_e, 1, f2), w_down,
      b_down.reshape(n_e, 1, d))


def _combine_kernel(tc_ref, ls_ref, base_ref, pos_ref, gate_ref, h_ref, ys_ref, g_ref, b_ref, o_ref,
                    buf, sems, *, tm, alpha):
    t = pl.program_id(0)
    n_t = pl.num_programs(0)
    slot = lax.rem(t, 2)
    max_bit = tm.bit_length() - 1

    def runs(tile, slot_, wait):
        _tile_runs(tile, tc_ref, ls_ref, base_ref, buf.at[slot_], ys_ref, sems.at[slot_], max_bit, False, wait)

    @pl.when(t == 0)
    def _():
        buf[...] = jnp.zeros_like(buf)
        runs(t, slot, False)

    @pl.when(t + 1 < n_t)
    def _():
        runs(t + 1, 1 - slot, False)

    runs(t, slot, True)

    pos = pos_ref[...]
    gates = gate_ref[...]
    col = lax.broadcasted_iota(I32, (tm, _sorted_rows(tm)), 1)
    weights = jnp.zeros((tm, _sorted_rows(tm)), F32)
    for k in range(TOP_K):
        weights = jnp.where(col == pos[:, k:k + 1], gates[:, k:k + 1], weights)
    y = jnp.dot(weights.astype(BF16), _unpack_halves(buf[slot]), preferred_element_type=F32)
    o_ref[...] = _layer_norm(alpha * h_ref[...] + y, g_ref[...], b_ref[...])


def _combine(h1_flat, ys, pos_t, gates_t, tables, ln_g, ln_b, alpha, tm):
    n_tok, d = h1_flat.shape
    n_t = n_tok // tm
    tc, ls, base = tables
    return pl.pallas_call(
        functools.partial(_combine_kernel, tm=tm, alpha=alpha),
        grid_spec=pltpu.PrefetchScalarGridSpec(
            num_scalar_prefetch=3,
            grid=(n_t,),
            in_specs=[
                pl.BlockSpec((tm, TOP_K), lambda i, *_: (i, 0)),
                pl.BlockSpec((tm, TOP_K), lambda i, *_: (i, 0)),
                pl.BlockSpec((tm, d), lambda i, *_: (i, 0)),
                pl.BlockSpec(memory_space=pl.ANY),
                pl.BlockSpec((1, d), lambda i, *_: (0, 0)),
                pl.BlockSpec((1, d), lambda i, *_: (0, 0)),
            ],
            out_specs=pl.BlockSpec((tm, d), lambda i, *_: (i, 0)),
            scratch_shapes=[pltpu.VMEM((2, _sorted_rows(tm), d // 2), I32), pltpu.SemaphoreType.DMA((2,))],
        ),
        out_shape=jax.ShapeDtypeStruct((n_tok, d), F32),
        compiler_params=pltpu.CompilerParams(
            dimension_semantics=("arbitrary",), vmem_limit_bytes=VMEM_LIMIT),
        name="moe_combine",
    )(tc, ls, base, pos_t, gates_t, h1_flat, ys, ln_g.reshape(1, d), ln_b.reshape(1, d))


def _moe(h1, idx, gates, rank, tile_counts, layer, w_gate_up, b_gate_up, w_down, b_down, ln_g, ln_b, alpha):
    bsz, s, d = h1.shape
    n_tok = bsz * s
    tm = _tile(s, TOKEN_TILE)
    n_t = n_tok // tm
    tmb = EXPERT_ROWS
    n_rows = n_tok * TOP_K + n_t * N_EXPERTS * (RUN_ALIGN - 1) + N_EXPERTS * tmb
    n_rows = (n_rows + tmb - 1) // tmb * tmb
    nb = n_rows // tmb

    tile_counts = (tile_counts + RUN_ALIGN - 1) // RUN_ALIGN * RUN_ALIGN
    counts = jnp.sum(tile_counts, axis=0)
    padded = (counts + tmb - 1) // tmb * tmb
    pad_end = jnp.cumsum(padded).astype(I32)
    pad_start = pad_end - padded
    base = pad_start[None, :] + jnp.cumsum(tile_counts, axis=0) - tile_counts
    local = jnp.cumsum(tile_counts, axis=1) - tile_counts
    experts = jnp.arange(N_EXPERTS, dtype=I32)
    local_tok = jnp.repeat(local, tm, axis=0)
    pos = rank + jnp.sum(jnp.where(idx[:, :, None] == experts, local_tok[None], 0), axis=-1).astype(I32)
    block_start = jnp.arange(nb, dtype=I32) * tmb
    block_expert = jnp.minimum(
        jnp.sum((block_start[:, None] >= pad_end[None, :]).astype(I32), axis=1), N_EXPERTS - 1).astype(I32)
    n_active = (pad_end[-1:] // tmb).astype(I32)
    tables = (tile_counts.reshape(-1).astype(I32), local.reshape(-1).astype(I32), base.reshape(-1).astype(I32))

    h1_flat = h1.reshape(n_tok, d)
    xs = _dispatch(h1_flat, pos, tables, pad_start + counts, pad_end, n_active, n_rows, tm, tmb)
    merge = lambda w: w.reshape((w.shape[0] * w.shape[1],) + w.shape[2:])
    present = padded > 0
    expert_ord = jnp.cumsum(present.astype(I32)) - 1
    expert_seq = (jnp.argsort(jnp.logical_not(present), stable=True) + layer * N_EXPERTS).astype(I32)
    n_seq = jnp.sum(present.astype(I32)).reshape(1)
    block_valid = jnp.clip((pad_start + counts)[block_expert] - block_start, 0, tmb).astype(I32)
    ys = _experts(xs, expert_ord[block_expert].astype(I32), expert_seq, n_seq, n_active, block_valid,
                  merge(w_gate_up), merge(b_gate_up), merge(w_down), merge(b_down), tmb)
    out = _combine(h1_flat, ys, pos.T, gates.T, tables, ln_g, ln_b, alpha, tm)
    return out.reshape(bsz, s, d)


@jax.jit
def _forward(x, mem, mem_ln_g, mem_ln_b, w_mem_kv, w_in_conv, conv_dw_w, conv_dw_b, conv_ln_g,
             conv_ln_b, w_in_sb, w_mix_out, ln_mix_g, ln_mix_b, w_router, b_router, w_gate_up,
             b_gate_up, w_down, b_down, ln_moe_g, ln_moe_b):
    depth = w_mix_out.shape[0]
    alpha = float((2 * depth) ** 0.25)
    c = conv_dw_w.shape[-1]
    kx, vx = _mem_kv(mem, mem_ln_g, mem_ln_b, w_mem_kv.astype(BF16))

    h = x
    for i in range(depth):
        j = i // 2
        if i % 2 == 0:
            y_mix, qm = _conv_mix(h, w_in_conv[j].astype(BF16), conv_dw_w[j], conv_dw_b[j],
                                  conv_ln_g[j], conv_ln_b[j])
            qm_src, qm_block = qm, 0
        else:
            u = _sb_proj(h, w_in_sb[j].astype(BF16), c)
            y_mix = _sb_attn(u, c)
            qm_src, qm_block = u, (3 * c) // MEM_WIDTH
        h1, idx, gates, rank, cnt = _mix_out(
            h, y_mix, qm_src, qm_block, kx, vx, w_mix_out[i].astype(BF16), ln_mix_g[i], ln_mix_b[i],
            w_router[i].T.astype(BF16), b_router[i], alpha)
        h = _moe(h1, idx, gates, rank, cnt[:, :, 0], i, w_gate_up, b_gate_up, w_down, b_down,
                 ln_moe_g[i], ln_moe_b[i], alpha)
    return h


def kernel(x, mem, mem_ln_g, mem_ln_b, w_mem_kv, w_in_conv, conv_dw_w, conv_dw_b, conv_ln_g, conv_ln_b,
           w_in_sb, w_mix_out, ln_mix_g, ln_mix_b, w_router, b_router, w_gate_up, b_gate_up, w_down,
           b_down, ln_moe_g, ln_moe_b):
    return _forward(x, mem, mem_ln_g, mem_ln_b, w_mem_kv, w_in_conv, conv_dw_w, conv_dw_b, conv_ln_g,
                    conv_ln_b, w_in_sb, w_mix_out, ln_mix_g, ln_mix_b, w_router, b_router, w_gate_up,
                    b_gate_up, w_down, b_down, ln_moe_g, ln_moe_b)
```

```python
import functools

import jax
import jax.numpy as jnp
from jax import lax
from jax.experimental import pallas as pl
from jax.experimental.pallas import tpu as pltpu

F32 = jnp.float32
BF16 = jnp.bfloat16
I32 = jnp.int32

HEAD_DIM = 64
MEM_HEADS = 4
MEM_WIDTH = MEM_HEADS * HEAD_DIM
CONV_WIDTH = 31
N_EXPERTS = 32
TOP_K = 4
SWIGLU_LIMIT = 7.0
SWIGLU_ALPHA = 1.702
LN_EPS = 1e-5

LANES = 128
CONV_HALO = 32
CONV_STRIDE = 4
CONV_ROWS = 8 * CONV_STRIDE
SB_SKIP_BELOW = -104.0
LOG2_E = 1.4426950408889634
SB_PAIRS_PER_STEP = 6
SB_QUERY_ROWS = 256
SB_KEY_ROWS = 128
SB_FEW_ROWS = 64
TOKEN_TILE = 512
MIX_ROWS = 256
EXPERT_ROWS = 1024
EXPERT_COLS = 256
EXPERT_SUB = 256
SORT_ROWS = 256
RUN_ALIGN = 8


def _sorted_rows(tm):
    rows = TOP_K * tm + N_EXPERTS * RUN_ALIGN
    return (rows + SORT_ROWS - 1) // SORT_ROWS * SORT_ROWS
WEIGHT_DMA_PARTS = 4
VMEM_LIMIT = 48 * 1024 * 1024
EXPERT_VMEM_LIMIT = 56 * 1024 * 1024
HIGH_HALF = -65536


def _layer_norm(x, g, b):
    mu = jnp.mean(x, axis=-1, keepdims=True)
    xc = x - mu
    var = jnp.mean(xc * xc, axis=-1, keepdims=True)
    return xc * lax.rsqrt(var + LN_EPS) * g + b


def _sigmoid(x):
    return 1.0 / (1.0 + jnp.exp(-x))


def _tile(n, want):
    t = min(n, want)
    assert n % t == 0, (n, t)
    return t


def _pack_halves(x):
    n = x.shape[1] // 2
    lo = lax.bitcast_convert_type(x[:, :n], I32)
    hi = lax.bitcast_convert_type(x[:, n:], I32)
    return jnp.bitwise_or(jnp.bitwise_and(hi, HIGH_HALF), lax.shift_right_logical(lo, 16))


def _unpack_halves(p):
    lo = lax.bitcast_convert_type(lax.shift_left(p, 16), F32)
    hi = lax.bitcast_convert_type(jnp.bitwise_and(p, HIGH_HALF), F32)
    return jnp.concatenate([lo, hi], axis=1).astype(BF16)


def _mem_kv_kernel(mem_ref, g_ref, b_ref, w_ref, kx_ref, vx_ref):
    n_mem = mem_ref.shape[0]
    xn = _layer_norm(mem_ref[...], g_ref[...], b_ref[...])
    kv = jnp.dot(xn.astype(BF16), w_ref[...], preferred_element_type=F32)
    k_t = (kv[:, :MEM_WIDTH] * (HEAD_DIM ** -0.5)).T
    v = kv[:, MEM_WIDTH:]
    k_head = lax.broadcasted_iota(I32, (MEM_WIDTH, n_mem), 0) // HEAD_DIM
    v_head = lax.broadcasted_iota(I32, (n_mem, MEM_WIDTH), 1) // HEAD_DIM
    for hd in range(MEM_HEADS):
        kx_ref[:, hd * n_mem:(hd + 1) * n_mem] = jnp.where(k_head == hd, k_t, 0.0).astype(BF16)
        vx_ref[hd * n_mem:(hd + 1) * n_mem, :] = jnp.where(v_head == hd, v, 0.0).astype(BF16)


def _mem_kv(mem, g, b, w_bf):
    bsz, n_mem, d = mem.shape
    return pl.pallas_call(
        _mem_kv_kernel,
        grid=(bsz,),
        in_specs=[
            pl.BlockSpec((None, n_mem, d), lambda i: (i, 0, 0)),
            pl.BlockSpec((1, d), lambda i: (0, 0)),
            pl.BlockSpec((1, d), lambda i: (0, 0)),
            pl.BlockSpec((d, 2 * MEM_WIDTH), lambda i: (0, 0)),
        ],
        out_specs=[
            pl.BlockSpec((None, MEM_WIDTH, MEM_HEADS * n_mem), lambda i: (i, 0, 0)),
            pl.BlockSpec((None, MEM_HEADS * n_mem, MEM_WIDTH), lambda i: (i, 0, 0)),
        ],
        out_shape=[
            jax.ShapeDtypeStruct((bsz, MEM_WIDTH, MEM_HEADS * n_mem), BF16),
            jax.ShapeDtypeStruct((bsz, MEM_HEADS * n_mem, MEM_WIDTH), BF16),
        ],
        name="mem_kv",
    )(mem, g.reshape(1, d), b.reshape(1, d), w_bf)


def _conv_mix_kernel(h_ref, w_ref, dww_ref, dwb_ref, g_ref, b_ref, y_ref, qm_ref, gbuf, cbuf, *, tm, c):
    n_groups = c // LANES

    @pl.when(pl.program_id(1) == 0)
    def _():
        gbuf[:, 0:CONV_HALO, :] = jnp.zeros((n_groups, CONV_HALO, LANES), F32)

    @pl.when(pl.program_id(1) > 0)
    def _():
        gbuf[:, 0:CONV_HALO, :] = gbuf[:, tm:tm + CONV_HALO, :]

    u = jnp.dot(h_ref[...].astype(BF16), w_ref[...], preferred_element_type=F32)
    glu = u[:, :c] * _sigmoid(u[:, c:2 * c])
    for grp in range(n_groups):
        gbuf[grp, CONV_HALO:, :] = glu[:, grp * LANES:(grp + 1) * LANES]
    qm_ref[...] = u[:, 2 * c:].astype(BF16)

    first_tap = CONV_HALO - (CONV_WIDTH - 1)

    def rows(i, carry):
        r0 = pl.multiple_of(i * CONV_ROWS, CONV_ROWS)
        for grp in range(n_groups):
            src = gbuf.at[grp, pl.ds(r0, CONV_ROWS + CONV_HALO), :]
            dst = cbuf.at[grp, pl.ds(r0, CONV_ROWS), :]
            lanes = slice(grp * LANES, (grp + 1) * LANES)
            taps = [jnp.broadcast_to(dww_ref[j:j + 1, lanes], (8, LANES)) for j in range(CONV_WIDTH)]
            bias = jnp.broadcast_to(dwb_ref[:, lanes], (8, LANES))
            for r in range(CONV_STRIDE):
                acc = bias
                for j in range(CONV_WIDTH):
                    acc = acc + src[pl.ds(first_tap + j + r, 8, stride=CONV_STRIDE), :] * taps[j]
                dst[pl.ds(r, 8, stride=CONV_STRIDE), :] = acc
        conv = jnp.concatenate([cbuf[grp, pl.ds(r0, CONV_ROWS), :] for grp in range(n_groups)], axis=-1)
        yn = _layer_norm(conv, g_ref[...], b_ref[...])
        y_ref[pl.ds(r0, CONV_ROWS), :] = (yn * _sigmoid(yn)).astype(BF16)
        return carry

    lax.fori_loop(0, tm // CONV_ROWS, rows, 0)


def _conv_mix(h, w_bf, dw_w, dw_b, ln_g, ln_b):
    bsz, s, d = h.shape
    c = dw_w.shape[1]
    tm = _tile(s, TOKEN_TILE)
    n_out = w_bf.shape[1]
    row = lambda a: a.reshape(1, -1)
    return pl.pallas_call(
        functools.partial(_conv_mix_kernel, tm=tm, c=c),
        grid=(bsz, s // tm),
        in_specs=[
            pl.BlockSpec((None, tm, d), lambda b, i: (b, i, 0)),
            pl.BlockSpec((d, n_out), lambda b, i: (0, 0)),
            pl.BlockSpec((CONV_WIDTH, c), lambda b, i: (0, 0)),
            pl.BlockSpec((1, c), lambda b, i: (0, 0)),
            pl.BlockSpec((1, c), lambda b, i: (0, 0)),
            pl.BlockSpec((1, c), lambda b, i: (0, 0)),
        ],
        out_specs=[
            pl.BlockSpec((None, tm, c), lambda b, i: (b, i, 0)),
            pl.BlockSpec((None, tm, MEM_WIDTH), lambda b, i: (b, i, 0)),
        ],
        out_shape=[
            jax.ShapeDtypeStruct((bsz, s, c), BF16),
            jax.ShapeDtypeStruct((bsz, s, MEM_WIDTH), BF16),
        ],
        scratch_shapes=[pltpu.VMEM((c // LANES, tm + CONV_HALO, LANES), F32),
                        pltpu.VMEM((c // LANES, tm, LANES), F32)],
        compiler_params=pltpu.CompilerParams(
            dimension_semantics=("arbitrary", "arbitrary"), vmem_limit_bytes=VMEM_LIMIT),
        name="conv_mix",
    )(h, w_bf, dw_w, row(dw_b), row(ln_g), row(ln_b))


def _sb_proj_kernel(h_ref, w_ref, u_ref, *, c):
    u = jnp.dot(h_ref[...].astype(BF16), w_ref[...], preferred_element_type=F32)
    u_ref[:, :c] = (u[:, :c] * (HEAD_DIM ** -0.5)).astype(BF16)
    u_ref[:, c:] = u[:, c:].astype(BF16)


def _sb_proj(h, w_bf, c):
    bsz, s, d = h.shape
    tm = _tile(s, TOKEN_TILE)
    n_out = w_bf.shape[1]
    return pl.pallas_call(
        functools.partial(_sb_proj_kernel, c=c),
        grid=(bsz, s // tm),
        in_specs=[
            pl.BlockSpec((None, tm, d), lambda b, i: (b, i, 0)),
            pl.BlockSpec((d, n_out), lambda b, i: (0, 0)),
        ],
        out_specs=pl.BlockSpec((None, tm, n_out), lambda b, i: (b, i, 0)),
        out_shape=jax.ShapeDtypeStruct((bsz, s, n_out), BF16),
        compiler_params=pltpu.CompilerParams(
            dimension_semantics=("arbitrary", "arbitrary"), vmem_limit_bytes=VMEM_LIMIT),
        name="sb_proj",
    )(h, w_bf)


def _sb_attn_kernel(q_ref, k_ref, v_ref, o_ref, carry_ref, acc_ref, *, tq, tk, n_pairs):
    qi = pl.program_id(2)
    n_heads = 2 * n_pairs
    lane = lax.broadcasted_iota(I32, (tq, LANES), 1)
    k_lane = lax.broadcasted_iota(I32, (tk, LANES), 1)
    q_heads = []
    for pr in range(n_pairs):
        q2 = q_ref[:, pr * LANES:(pr + 1) * LANES]
        q_heads += [jnp.where(lane < HEAD_DIM, q2, jnp.zeros_like(q2)),
                    jnp.where(lane >= HEAD_DIM, q2, jnp.zeros_like(q2))]
    later = (lax.broadcasted_iota(I32, (tk, tk), 0) > lax.broadcasted_iota(I32, (tk, tk), 1)).astype(BF16)

    def block(kb, masked, r0, r1):
        n = r1 - r0
        start = pl.multiple_of(kb * tk, tk)
        if masked:
            causal = ((kb * tk + lax.broadcasted_iota(I32, (n, tk), 1))
                      < (qi * tq + r0 + lax.broadcasted_iota(I32, (n, tk), 0)))
        log_nots, log_betas, parts = [], [], []
        for pr in range(n_pairs):
            k2 = k_ref[pl.ds(start, tk), pr * LANES:(pr + 1) * LANES]
            z2 = lax.dot_general(jnp.concatenate([q[r0:r1] for q in q_heads[2 * pr:2 * pr + 2]], axis=0), k2,
                                 (((1,), (1,)), ((), ())), preferred_element_type=F32)
            for half in range(2):
                z = z2[half * n:(half + 1) * n] * LOG2_E
                sp = jnp.maximum(z, 0.0) + jnp.log2(1.0 + jnp.exp2(-jnp.abs(z)))
                log_not = -sp
                if masked:
                    log_not = jnp.where(causal, log_not, 0.0)
                hi = log_not.astype(BF16)
                parts += [hi, (log_not - hi.astype(F32)).astype(BF16)]
                log_nots.append(log_not)
                log_betas.append(z - sp)
        sums = jnp.dot(jnp.concatenate(parts, axis=0), later, preferred_element_type=F32)
        for pr in range(n_pairs):
            v2 = v_ref[pl.ds(start, tk), pr * LANES:(pr + 1) * LANES]
            ws = []
            for half in range(2):
                hd = 2 * pr + half
                carry = carry_ref[hd, r0:r1, :]
                tail = sums[2 * hd * n:(2 * hd + 1) * n] + sums[(2 * hd + 1) * n:(2 * hd + 2) * n] + carry
                w = jnp.exp2(log_betas[hd] + tail)
                if masked:
                    w = jnp.where(causal, w, 0.0)
                ws.append(w.astype(BF16))
                carry_ref[hd, r0:r1, :] = carry + jnp.sum(log_nots[hd], axis=1, keepdims=True)
            v_rows = jnp.concatenate([jnp.where(k_lane < HEAD_DIM, v2, jnp.zeros_like(v2)),
                                      jnp.where(k_lane >= HEAD_DIM, v2, jnp.zeros_like(v2))], axis=0)
            acc_ref[pr, r0:r1, :] += jnp.dot(jnp.concatenate(ws, axis=1), v_rows, preferred_element_type=F32)

    carry_ref[...] = jnp.zeros_like(carry_ref)
    acc_ref[...] = jnp.zeros_like(acc_ref)
    n_diag = tq // tk
    for j in range(n_diag - 1, -1, -1):
        block(qi * n_diag + j, True, j * tk, tq)

    row_count = 1 + lax.broadcasted_iota(I32, (tq, tk), 0)

    def live_rows():
        top = carry_ref[0]
        for hd in range(1, n_heads):
            top = jnp.maximum(top, carry_ref[hd])
        return jnp.max(jnp.where(top > SB_SKIP_BELOW * LOG2_E, row_count, 0))

    def cond(state):
        kb, n_live = state
        return jnp.logical_and(kb >= 0, n_live > 0)

    def body(state):
        kb, n_live = state

        @pl.when(n_live <= SB_FEW_ROWS)
        def _():
            block(kb, False, 0, SB_FEW_ROWS)

        @pl.when(n_live > SB_FEW_ROWS)
        def _():
            block(kb, False, 0, tq)

        return kb - 1, live_rows()

    lax.while_loop(cond, body, (qi * n_diag - 1, live_rows()))
    o_ref[...] = jnp.concatenate([acc_ref[pr] for pr in range(n_pairs)], axis=1).astype(BF16)


def _sb_attn(u, c):
    bsz, s, _ = u.shape
    tq = _tile(s, SB_QUERY_ROWS)
    tk = _tile(tq, SB_KEY_ROWS)
    width = SB_PAIRS_PER_STEP * LANES
    n_groups = c // width
    return pl.pallas_call(
        functools.partial(_sb_attn_kernel, tq=tq, tk=tk, n_pairs=SB_PAIRS_PER_STEP),
        grid=(bsz, n_groups, s // tq),
        in_specs=[
            pl.BlockSpec((None, tq, width), lambda b, p, i: (b, i, p)),
            pl.BlockSpec((None, s, width), lambda b, p, i: (b, 0, n_groups + p), pipeline_mode=pl.Buffered(1)),
            pl.BlockSpec((None, s, width), lambda b, p, i: (b, 0, 2 * n_groups + p), pipeline_mode=pl.Buffered(1)),
        ],
        out_specs=pl.BlockSpec((None, tq, width), lambda b, p, i: (b, i, p)),
        out_shape=jax.ShapeDtypeStruct((bsz, s, c), BF16),
        scratch_shapes=[pltpu.VMEM((2 * SB_PAIRS_PER_STEP, tq, tk), F32),
                        pltpu.VMEM((SB_PAIRS_PER_STEP, tq, LANES), F32)],
        compiler_params=pltpu.CompilerParams(
            dimension_semantics=("arbitrary", "arbitrary", "arbitrary"), vmem_limit_bytes=VMEM_LIMIT),
        name="sb_attn",
    )(u, u, u)


def _mix_out_kernel(h_ref, y_ref, qm_ref, kx_ref, vx_ref, wo_ref, g_ref, b_ref, wr_ref, br_ref,
                    h1_ref, idx_ref, gate_ref, rank_ref, cnt_ref, *, tm, c, n_mem, alpha):
    def rows(i, carry):
        sl = pl.ds(pl.multiple_of(i * MIX_ROWS, MIX_ROWS), MIX_ROWS)
        s = jnp.dot(qm_ref[sl, :], kx_ref[...], preferred_element_type=F32)
        probs = []
        for hd in range(MEM_HEADS):
            sh = s[:, hd * n_mem:(hd + 1) * n_mem]
            e = jnp.exp(sh - jnp.max(sh, axis=-1, keepdims=True))
            probs.append((e / jnp.sum(e, axis=-1, keepdims=True)).astype(BF16))
        y_mem = jnp.dot(jnp.concatenate(probs, axis=-1), vx_ref[...], preferred_element_type=F32)
        y = (jnp.dot(y_ref[sl, :], wo_ref[0:c, :], preferred_element_type=F32)
             + jnp.dot(y_mem.astype(BF16), wo_ref[c:, :], preferred_element_type=F32))
        h1_ref[sl, :] = _layer_norm(alpha * h_ref[sl, :] + y, g_ref[...], b_ref[...])
        return carry

    lax.fori_loop(0, tm // MIX_ROWS, rows, 0)
    h1 = h1_ref[...]

    logits = lax.dot_general(wr_ref[...], h1.astype(BF16), (((1,), (1,)), ((), ())),
                             preferred_element_type=F32) + br_ref[...]
    e_iota = lax.broadcasted_iota(I32, (N_EXPERTS, tm), 0)
    vals, sels = [], []
    for _ in range(TOP_K):
        m = jnp.max(logits, axis=0, keepdims=True)
        sel = jnp.min(jnp.where(logits == m, e_iota, N_EXPERTS), axis=0, keepdims=True)
        vals.append(m)
        sels.append(sel)
        logits = jnp.where(e_iota == sel, -jnp.inf, logits)
    exps = [jnp.exp(v - vals[0]) for v in vals]
    denom = exps[0] + exps[1] + exps[2] + exps[3]
    gate_ref[...] = jnp.concatenate([e / denom for e in exps], axis=0)
    idx_ref[...] = jnp.concatenate(sels, axis=0)

    onehot = jnp.zeros((N_EXPERTS, tm), F32)
    for sel in sels:
        onehot = onehot + (e_iota == sel).astype(F32)
    earlier = (lax.broadcasted_iota(I32, (tm, tm), 0) < lax.broadcasted_iota(I32, (tm, tm), 1)).astype(BF16)
    rank_e = jnp.dot(onehot.astype(BF16), earlier, preferred_element_type=F32)
    rank_ref[...] = jnp.concatenate(
        [jnp.sum(jnp.where(e_iota == sel, rank_e, 0.0), axis=0, keepdims=True) for sel in sels],
        axis=0).astype(I32)
    cnt_ref[...] = jnp.sum(onehot, axis=1, keepdims=True).astype(I32)


def _mix_out(h, y_mix, qm_src, qm_block, kx, vx, wo_bf, ln_g, ln_b, wr_t_bf, b_router, alpha):
    bsz, s, d = h.shape
    c = y_mix.shape[-1]
    n_mem = vx.shape[1] // MEM_HEADS
    tm = _tile(s, TOKEN_TILE)
    n_s = s // tm
    n_tok = bsz * s
    tok_spec = pl.BlockSpec((TOP_K, tm), lambda b, i: (0, b * n_s + i))
    return pl.pallas_call(
        functools.partial(_mix_out_kernel, tm=tm, c=c, n_mem=n_mem, alpha=alpha),
        grid=(bsz, n_s),
        in_specs=[
            pl.BlockSpec((None, tm, d), lambda b, i: (b, i, 0)),
            pl.BlockSpec((None, tm, c), lambda b, i: (b, i, 0)),
            pl.BlockSpec((None, tm, MEM_WIDTH), lambda b, i: (b, i, qm_block)),
            pl.BlockSpec((None, MEM_WIDTH, MEM_HEADS * n_mem), lambda b, i: (b, 0, 0)),
            pl.BlockSpec((None, MEM_HEADS * n_mem, MEM_WIDTH), lambda b, i: (b, 0, 0)),
            pl.BlockSpec((c + MEM_WIDTH, d), lambda b, i: (0, 0)),
            pl.BlockSpec((1, d), lambda b, i: (0, 0)),
            pl.BlockSpec((1, d), lambda b, i: (0, 0)),
            pl.BlockSpec((N_EXPERTS, d), lambda b, i: (0, 0)),
            pl.BlockSpec((N_EXPERTS, 1), lambda b, i: (0, 0)),
        ],
        out_specs=[
            pl.BlockSpec((None, tm, d), lambda b, i: (b, i, 0)),
            tok_spec, tok_spec, tok_spec,
            pl.BlockSpec((None, N_EXPERTS, 1), lambda b, i: (b * n_s + i, 0, 0)),
        ],
        out_shape=[
            jax.ShapeDtypeStruct((bsz, s, d), F32),
            jax.ShapeDtypeStruct((TOP_K, n_tok), I32),
            jax.ShapeDtypeStruct((TOP_K, n_tok), F32),
            jax.ShapeDtypeStruct((TOP_K, n_tok), I32),
            jax.ShapeDtypeStruct((bsz * n_s, N_EXPERTS, 1), I32),
        ],
        compiler_params=pltpu.CompilerParams(
            dimension_semantics=("arbitrary", "arbitrary"), vmem_limit_bytes=VMEM_LIMIT),
        name="mix_out",
    )(h, y_mix, qm_src, kx, vx, wo_bf, ln_g.reshape(1, d), ln_b.reshape(1, d), wr_t_bf,
      b_router.reshape(N_EXPERTS, 1))


def _copy_rows(src_ref, src_row, dst_ref, dst_row, count, sem, max_bit, wait):
    for bit in range(max_bit, RUN_ALIGN.bit_length() - 2, -1):
        size = 1 << bit
        off = lax.shift_left(lax.shift_right_logical(count, bit + 1), bit + 1)

        @pl.when(jnp.bitwise_and(lax.shift_right_logical(count, bit), 1) == 1)
        def _():
            cp = pltpu.make_async_copy(src_ref.at[pl.ds(pl.multiple_of(src_row + off, RUN_ALIGN), size)],
                                       dst_ref.at[pl.ds(pl.multiple_of(dst_row + off, RUN_ALIGN), size)], sem)
            if wait:
                cp.wait()
            else:
                cp.start()


def _tile_runs(tile, tc_ref, ls_ref, base_ref, local_ref, global_ref, sem, max_bit, to_global, wait):
    if wait:
        last = tile * N_EXPERTS + N_EXPERTS - 1
        total = ls_ref[last] + tc_ref[last]
        total_bit = local_ref.shape[0].bit_length() - 1
        if to_global:
            _copy_rows(local_ref, 0, global_ref, 0, total, sem, total_bit, True)
        else:
            _copy_rows(global_ref, 0, local_ref, 0, total, sem, total_bit, True)
        return

    def per_expert(e, carry):
        k = tile * N_EXPERTS + e
        if to_global:
            _copy_rows(local_ref, ls_ref[k], global_ref, base_ref[k], tc_ref[k], sem, max_bit, wait)
        else:
            _copy_rows(global_ref, base_ref[k], local_ref, ls_ref[k], tc_ref[k], sem, max_bit, wait)
        return carry

    lax.fori_loop(0, N_EXPERTS, per_expert, 0)


def _dispatch_kernel(tc_ref, ls_ref, base_ref, fill_lo_ref, fill_hi_ref, na_ref,
                     pos_ref, h_ref, xs_ref, buf, zero_ref, sems, fill_sem, *, tm, tmb):
    t = pl.program_id(0)
    n_t = pl.num_programs(0)
    slot = lax.rem(t, 2)
    max_bit = tm.bit_length() - 1

    def runs(tile, slot_, wait):
        _tile_runs(tile, tc_ref, ls_ref, base_ref, buf.at[slot_], xs_ref, sems.at[slot_], max_bit, True, wait)

    @pl.when(t >= 2)
    def _():
        runs(t - 2, slot, True)

    x = h_ref[...].astype(BF16)
    pos = pos_ref[...]

    def sort_rows(i, carry):
        r0 = pl.multiple_of(i * SORT_ROWS, SORT_ROWS)
        row = r0 + lax.broadcasted_iota(I32, (SORT_ROWS, tm), 0)
        perm = jnp.zeros((SORT_ROWS, tm), F32)
        for k in range(TOP_K):
            perm = jnp.where(row == pos[k:k + 1, :], 1.0, perm)
        rows = jnp.dot(perm.astype(BF16), x, preferred_element_type=F32)
        buf[slot, pl.ds(r0, SORT_ROWS), :] = _pack_halves(rows)
        return carry

    lax.fori_loop(0, _sorted_rows(tm) // SORT_ROWS, sort_rows, 0)
    runs(t, slot, False)

    def zero_fill(wait):
        pad_bit = tmb.bit_length() - 2

        def per_expert(e, carry):
            _copy_rows(zero_ref, 0, xs_ref, fill_lo_ref[e], fill_hi_ref[e] - fill_lo_ref[e],
                       fill_sem, pad_bit, wait)
            return carry

        def per_block(blk, carry):
            cp = pltpu.make_async_copy(zero_ref, xs_ref.at[pl.ds(blk * tmb, tmb)], fill_sem)
            if wait:
                cp.wait()
            else:
                cp.start()
            return carry

        lax.fori_loop(0, N_EXPERTS, per_expert, 0)
        lax.fori_loop(na_ref[0], xs_ref.shape[0] // tmb, per_block, 0)

    @pl.when(t == 0)
    def _():
        zero_ref[...] = jnp.zeros_like(zero_ref)
        zero_fill(False)

    @pl.when(t == n_t - 1)
    def _():
        @pl.when(t >= 1)
        def _():
            runs(t - 1, 1 - slot, True)
        runs(t, slot, True)
        zero_fill(True)


def _dispatch(h1_flat, pos, tables, fill_lo, fill_hi, n_active, n_rows, tm, tmb):
    n_tok, d = h1_flat.shape
    n_t = n_tok // tm
    tc, ls, base = tables
    return pl.pallas_call(
        functools.partial(_dispatch_kernel, tm=tm, tmb=tmb),
        grid_spec=pltpu.PrefetchScalarGridSpec(
            num_scalar_prefetch=6,
            grid=(n_t,),
            in_specs=[
                pl.BlockSpec((TOP_K, tm), lambda i, *_: (0, i)),
                pl.BlockSpec((tm, d), lambda i, *_: (i, 0)),
            ],
            out_specs=pl.BlockSpec(memory_space=pl.ANY),
            scratch_shapes=[
                pltpu.VMEM((2, _sorted_rows(tm), d // 2), I32),
                pltpu.VMEM((tmb, d // 2), I32),
                pltpu.SemaphoreType.DMA((2,)),
                pltpu.SemaphoreType.DMA,
            ],
        ),
        out_shape=jax.ShapeDtypeStruct((n_rows, d // 2), I32),
        compiler_params=pltpu.CompilerParams(
            dimension_semantics=("arbitrary",), vmem_limit_bytes=VMEM_LIMIT),
        name="moe_dispatch",
    )(tc, ls, base, fill_lo, fill_hi, n_active, pos, h1_flat)


def _expert_kernel(ord_ref, seq_ref, nseq_ref, na_ref, valid_ref, xs_ref, wgu_hbm, bgu_ref, wd_hbm, bd_ref, ys_ref,
                   wgu_f32, wd_f32, wgu_bf, wd_bf, act_ref, sems, *, f):
    i = pl.program_id(0)
    n_active = na_ref[0]
    j = jnp.minimum(i, n_active - 1)
    prev = jnp.maximum(j - 1, 0)
    o = ord_ref[j]
    new_expert = jnp.logical_or(i == 0, o != ord_ref[prev])

    def fetch(o_, slot, wait):
        e = seq_ref[o_]
        copies = []
        for src, dst, parts in ((wgu_hbm, wgu_f32, WEIGHT_DMA_PARTS), (wd_hbm, wd_f32, WEIGHT_DMA_PARTS // 2)):
            rows = src.shape[1] // parts
            for p in range(parts):
                copies.append(pltpu.make_async_copy(src.at[e, pl.ds(p * rows, rows)],
                                                    dst.at[slot, pl.ds(p * rows, rows)], sems.at[slot]))
        for cp in copies:
            if wait:
                cp.wait()
            else:
                cp.start()

    @pl.when(i == 0)
    def _():
        fetch(0, 0, False)

    @pl.when(jnp.logical_and(i < n_active, new_expert))
    def _():
        slot = lax.rem(o, 2)
        fetch(o, slot, True)

        @pl.when(o + 1 < nseq_ref[0])
        def _():
            fetch(o + 1, 1 - slot, False)

        rows = 128
        for r in range(0, wgu_bf.shape[0], rows):
            wgu_bf[r:r + rows, :] = wgu_f32[slot, r:r + rows, :].astype(BF16)
        for r in range(0, wd_bf.shape[0], rows):
            wd_bf[r:r + rows, :] = wd_f32[slot, r:r + rows, :].astype(BF16)

    valid = valid_ref[i]

    def rows_pass(r, n):
        x = _unpack_halves(xs_ref[r:r + n, :])
        for c0 in range(0, f, EXPERT_COLS):
            g = jnp.dot(x, wgu_bf[:, c0:c0 + EXPERT_COLS], preferred_element_type=F32) + bgu_ref[:, c0:c0 + EXPERT_COLS]
            u = (jnp.dot(x, wgu_bf[:, f + c0:f + c0 + EXPERT_COLS], preferred_element_type=F32)
                 + bgu_ref[:, f + c0:f + c0 + EXPERT_COLS])
            gate = jnp.minimum(g, SWIGLU_LIMIT)
            up = jnp.clip(u, -SWIGLU_LIMIT, SWIGLU_LIMIT)
            act_ref[0:n, c0:c0 + EXPERT_COLS] = ((up + 1.0) * gate * _sigmoid(SWIGLU_ALPHA * gate)).astype(BF16)
        y = jnp.dot(act_ref[0:n, :], wd_bf[...], preferred_element_type=F32) + bd_ref[...]
        ys_ref[r:r + n, :] = _pack_halves(y.astype(BF16).astype(F32))

    def zero_rows(r, n):
        ys_ref[r:r + n, :] = jnp.zeros((n, ys_ref.shape[1]), I32)

    for r in range(0, xs_ref.shape[0], 2 * EXPERT_SUB):
        @pl.when(valid > r + EXPERT_SUB)
        def _():
            rows_pass(r, 2 * EXPERT_SUB)

        @pl.when(jnp.logical_and(valid > r, valid <= r + EXPERT_SUB))
        def _():
            rows_pass(r, EXPERT_SUB)
            zero_rows(r + EXPERT_SUB, EXPERT_SUB)

        @pl.when(valid <= r)
        def _():
            zero_rows(r, 2 * EXPERT_SUB)


def _experts(xs, block_ord, expert_seq, n_seq, n_active, block_valid, w_gate_up, b_gate_up, w_down, b_down, tmb):
    n_rows, half = xs.shape
    n_e, d, f2 = w_gate_up.shape
    f = f2 // 2
    nb = n_rows // tmb

    def blk(i, refs):
        return jnp.minimum(i, refs[3][0] - 1)

    def expert(i, refs):
        return refs[1][refs[0][blk(i, refs)]]

    return pl.pallas_call(
        functools.partial(_expert_kernel, f=f),
        grid_spec=pltpu.PrefetchScalarGridSpec(
            num_scalar_prefetch=5,
            grid=(nb,),
            in_specs=[
                pl.BlockSpec((tmb, half), lambda i, *refs: (blk(i, refs), 0)),
                pl.BlockSpec(memory_space=pl.ANY),
                pl.BlockSpec((None, 1, f2), lambda i, *refs: (expert(i, refs), 0, 0)),
                pl.BlockSpec(memory_space=pl.ANY),
                pl.BlockSpec((None, 1, d), lambda i, *refs: (expert(i, refs), 0, 0)),
            ],
            out_specs=pl.BlockSpec((tmb, half), lambda i, *refs: (i, 0)),
            scratch_shapes=[
                pltpu.VMEM((2, d, f2), F32), pltpu.VMEM((2, f, d), F32),
                pltpu.VMEM((d, f2), BF16), pltpu.VMEM((f, d), BF16),
                pltpu.VMEM((2 * EXPERT_SUB, f), BF16),
                pltpu.SemaphoreType.DMA((2,)),
            ],
        ),
        out_shape=jax.ShapeDtypeStruct((n_rows, half), I32),
        compiler_params=pltpu.CompilerParams(
            dimension_semantics=("arbitrary",), vmem_limit_bytes=EXPERT_VMEM_LIMIT),
        name="moe_experts",
    )(block_ord, expert_seq, n_seq, n_active, block_valid, xs, w_gate_up, b_gate_up.reshape(n_e, 1, f2), w_down,
      b_down.reshape(n_e, 1, d))


def _combine_kernel(tc_ref, ls_ref, base_ref, pos_ref, gate_ref, h_ref, ys_ref, g_ref, b_ref, o_ref,
                    buf, sems, *, tm, alpha):
    t = pl.program_id(0)
    n_t = pl.num_programs(0)
    slot = lax.rem(t, 2)
    max_bit = tm.bit_length() - 1

    def runs(tile, slot_, wait):
        _tile_runs(tile, tc_ref, ls_ref, base_ref, buf.at[slot_], ys_ref, sems.at[slot_], max_bit, False, wait)

    @pl.when(t == 0)
    def _():
        buf[...] = jnp.zeros_like(buf)
        runs(t, slot, False)

    @pl.when(t + 1 < n_t)
    def _():
        runs(t + 1, 1 - slot, False)

    runs(t, slot, True)

    pos = pos_ref[...]
    gates = gate_ref[...]
    col = lax.broadcasted_iota(I32, (tm, _sorted_rows(tm)), 1)
    weights = jnp.zeros((tm, _sorted_rows(tm)), F32)
    for k in range(TOP_K):
        weights = jnp.where(col == pos[:, k:k + 1], gates[:, k:k + 1], weights)
    y = jnp.dot(weights.astype(BF16), _unpack_halves(buf[slot]), preferred_element_type=F32)
    o_ref[...] = _layer_norm(alpha * h_ref[...] + y, g_ref[...], b_ref[...])


def _combine(h1_flat, ys, pos_t, gates_t, tables, ln_g, ln_b, alpha, tm):
    n_tok, d = h1_flat.shape
    n_t = n_tok // tm
    tc, ls, base = tables
    return pl.pallas_call(
        functools.partial(_combine_kernel, tm=tm, alpha=alpha),
        grid_spec=pltpu.PrefetchScalarGridSpec(
            num_scalar_prefetch=3,
            grid=(n_t,),
            in_specs=[
                pl.BlockSpec((tm, TOP_K), lambda i, *_: (i, 0)),
                pl.BlockSpec((tm, TOP_K), lambda i, *_: (i, 0)),
                pl.BlockSpec((tm, d), lambda i, *_: (i, 0)),
                pl.BlockSpec(memory_space=pl.ANY),
                pl.BlockSpec((1, d), lambda i, *_: (0, 0)),
                pl.BlockSpec((1, d), lambda i, *_: (0, 0)),
            ],
            out_specs=pl.BlockSpec((tm, d), lambda i, *_: (i, 0)),
            scratch_shapes=[pltpu.VMEM((2, _sorted_rows(tm), d // 2), I32), pltpu.SemaphoreType.DMA((2,))],
        ),
        out_shape=jax.ShapeDtypeStruct((n_tok, d), F32),
        compiler_params=pltpu.CompilerParams(
            dimension_semantics=("arbitrary",), vmem_limit_bytes=VMEM_LIMIT),
        name="moe_combine",
    )(tc, ls, base, pos_t, gates_t, h1_flat, ys, ln_g.reshape(1, d), ln_b.reshape(1, d))


def _moe(h1, idx, gates, rank, tile_counts, layer, w_gate_up, b_gate_up, w_down, b_down, ln_g, ln_b, alpha):
    bsz, s, d = h1.shape
    n_tok = bsz * s
    tm = _tile(s, TOKEN_TILE)
    n_t = n_tok // tm
    tmb = EXPERT_ROWS
    n_rows = n_tok * TOP_K + n_t * N_EXPERTS * (RUN_ALIGN - 1) + N_EXPERTS * tmb
    n_rows = (n_rows + tmb - 1) // tmb * tmb
    nb = n_rows // tmb

    tile_counts = (tile_counts + RUN_ALIGN - 1) // RUN_ALIGN * RUN_ALIGN
    counts = jnp.sum(tile_counts, axis=0)
    padded = (counts + tmb - 1) // tmb * tmb
    pad_end = jnp.cumsum(padded).astype(I32)
    pad_start = pad_end - padded
    base = pad_start[None, :] + jnp.cumsum(tile_counts, axis=0) - tile_counts
    local = jnp.cumsum(tile_counts, axis=1) - tile_counts
    experts = jnp.arange(N_EXPERTS, dtype=I32)
    local_tok = jnp.repeat(local, tm, axis=0)
    pos = rank + jnp.sum(jnp.where(idx[:, :, None] == experts, local_tok[None], 0), axis=-1).astype(I32)
    block_start = jnp.arange(nb, dtype=I32) * tmb
    block_expert = jnp.minimum(
        jnp.sum((block_start[:, None] >= pad_end[None, :]).astype(I32), axis=1), N_EXPERTS - 1).astype(I32)
    n_active = (pad_end[-1:] // tmb).astype(I32)
    tables = (tile_counts.reshape(-1).astype(I32), local.reshape(-1).astype(I32), base.reshape(-1).astype(I32))

    h1_flat = h1.reshape(n_tok, d)
    xs = _dispatch(h1_flat, pos, tables, pad_start + counts, pad_end, n_active, n_rows, tm, tmb)
    merge = lambda w: w.reshape((w.shape[0] * w.shape[1],) + w.shape[2:])
    present = padded > 0
    expert_ord = jnp.cumsum(present.astype(I32)) - 1
    expert_seq = (jnp.argsort(jnp.logical_not(present), stable=True) + layer * N_EXPERTS).astype(I32)
    n_seq = jnp.sum(present.astype(I32)).reshape(1)
    block_valid = jnp.clip((pad_start + counts)[block_expert] - block_start, 0, tmb).astype(I32)
    ys = _experts(xs, expert_ord[block_expert].astype(I32), expert_seq, n_seq, n_active, block_valid,
                  merge(w_gate_up), merge(b_gate_up), merge(w_down), merge(b_down), tmb)
    out = _combine(h1_flat, ys, pos.T, gates.T, tables, ln_g, ln_b, alpha, tm)
    return out.reshape(bsz, s, d)


@jax.jit
def _forward(x, mem, mem_ln_g, mem_ln_b, w_mem_kv, w_in_conv, conv_dw_w, conv_dw_b, conv_ln_g,
             conv_ln_b, w_in_sb, w_mix_out, ln_mix_g, ln_mix_b, w_router, b_router, w_gate_up,
             b_gate_up, w_down, b_down, ln_moe_g, ln_moe_b):
    depth = w_mix_out.shape[0]
    alpha = float((2 * depth) ** 0.25)
    c = conv_dw_w.shape[-1]
    kx, vx = _mem_kv(mem, mem_ln_g, mem_ln_b, w_mem_kv.astype(BF16))

    h = x
    for i in range(depth):
        j = i // 2
        if i % 2 == 0:
            y_mix, qm = _conv_mix(h, w_in_conv[j].astype(BF16), conv_dw_w[j], conv_dw_b[j],
                                  conv_ln_g[j], conv_ln_b[j])
            qm_src, qm_block = qm, 0
        else:
            u = _sb_proj(h, w_in_sb[j].astype(BF16), c)
            y_mix = _sb_attn(u, c)
            qm_src, qm_block = u, (3 * c) // MEM_WIDTH
        h1, idx, gates, rank, cnt = _mix_out(
            h, y_mix, qm_src, qm_block, kx, vx, w_mix_out[i].astype(BF16), ln_mix_g[i], ln_mix_b[i],
            w_router[i].T.astype(BF16), b_router[i], alpha)
        h = _moe(h1, idx, gates, rank, cnt[:, :, 0], i, w_gate_up, b_gate_up, w_down, b_down,
                 ln_moe_g[i], ln_moe_b[i], alpha)
    return h


def kernel(x, mem, mem_ln_g, mem_ln_b, w_mem_kv, w_in_conv, conv_dw_w, conv_dw_b, conv_ln_g, conv_ln_b,
           w_in_sb, w_mix_out, ln_mix_g, ln_mix_b, w_router, b_router, w_gate_up, b_gate_up, w_down,
           b_down, ln_moe_g, ln_moe_b):
    return _forward(x, mem, mem_ln_g, mem_ln_b, w_mem_kv, w_in_conv, conv_dw_w, conv_dw_b, conv_ln_g,
                    conv_ln_b, w_in_sb, w_mix_out, ln_mix_g, ln_mix_b, w_router, b_router, w_gate_up,
                    b_gate_up, w_down, b_down, ln_moe_g, ln_moe_b)
```

```python
import functools

import jax
import jax.numpy as jnp
from jax import lax
from jax.experimental import pallas as pl
from jax.experimental.pallas import tpu as pltpu

F32 = jnp.float32
BF16 = jnp.bfloat16
I32 = jnp.int32

HEAD_DIM = 64
MEM_HEADS = 4
MEM_WIDTH = MEM_HEADS * HEAD_DIM
CONV_WIDTH = 31
N_EXPERTS = 32
TOP_K = 4
SWIGLU_LIMIT = 7.0
SWIGLU_ALPHA = 1.702
LN_EPS = 1e-5

LANES = 128
CONV_HALO = 32
CONV_STRIDE = 4
CONV_ROWS = 8 * CONV_STRIDE
SB_SKIP_BELOW = -104.0
LOG2_E = 1.4426950408889634
SB_PAIRS_PER_STEP = 6
SB_QUERY_ROWS = 256
SB_KEY_ROWS = 128
SB_ROW_PREFIXES = (64, 192)
TOKEN_TILE = 512
MIX_ROWS = 256
EXPERT_ROWS = 1024
EXPERT_SUB = 256
SORT_ROWS = 256
RUN_ALIGN = 8
COMMON_RUN_BIT = 6


def _sorted_rows(tm):
    rows = TOP_K * tm + N_EXPERTS * RUN_ALIGN
    return (rows + SORT_ROWS - 1) // SORT_ROWS * SORT_ROWS
WEIGHT_DMA_PARTS = 4
VMEM_LIMIT = 48 * 1024 * 1024
EXPERT_VMEM_LIMIT = 56 * 1024 * 1024
HIGH_HALF = -65536


def _layer_norm(x, g, b):
    mu = jnp.mean(x, axis=-1, keepdims=True)
    xc = x - mu
    var = jnp.mean(xc * xc, axis=-1, keepdims=True)
    return xc * lax.rsqrt(var + LN_EPS) * g + b


def _sigmoid(x):
    return 1.0 / (1.0 + jnp.exp(-x))


def _tile(n, want):
    t = min(n, want)
    assert n % t == 0, (n, t)
    return t


def _pack_halves(x):
    n = x.shape[1] // 2
    lo = lax.bitcast_convert_type(x[:, :n], I32)
    hi = lax.bitcast_convert_type(x[:, n:], I32)
    return jnp.bitwise_or(jnp.bitwise_and(hi, HIGH_HALF), lax.shift_right_logical(lo, 16))


def _unpack_halves(p):
    lo = lax.bitcast_convert_type(lax.shift_left(p, 16), F32)
    hi = lax.bitcast_convert_type(jnp.bitwise_and(p, HIGH_HALF), F32)
    return jnp.concatenate([lo, hi], axis=1).astype(BF16)


def _mem_kv_kernel(mem_ref, g_ref, b_ref, w_ref, kx_ref, vx_ref):
    n_mem = mem_ref.shape[0]
    xn = _layer_norm(mem_ref[...], g_ref[...], b_ref[...])
    kv = jnp.dot(xn.astype(BF16), w_ref[...], preferred_element_type=F32)
    k_t = (kv[:, :MEM_WIDTH] * (HEAD_DIM ** -0.5)).T
    v = kv[:, MEM_WIDTH:]
    k_head = lax.broadcasted_iota(I32, (MEM_WIDTH, n_mem), 0) // HEAD_DIM
    v_head = lax.broadcasted_iota(I32, (n_mem, MEM_WIDTH), 1) // HEAD_DIM
    for hd in range(MEM_HEADS):
        kx_ref[:, hd * n_mem:(hd + 1) * n_mem] = jnp.where(k_head == hd, k_t, 0.0).astype(BF16)
        vx_ref[hd * n_mem:(hd + 1) * n_mem, :] = jnp.where(v_head == hd, v, 0.0).astype(BF16)


def _mem_kv(mem, g, b, w_bf):
    bsz, n_mem, d = mem.shape
    return pl.pallas_call(
        _mem_kv_kernel,
        grid=(bsz,),
        in_specs=[
            pl.BlockSpec((None, n_mem, d), lambda i: (i, 0, 0)),
            pl.BlockSpec((1, d), lambda i: (0, 0)),
            pl.BlockSpec((1, d), lambda i: (0, 0)),
            pl.BlockSpec((d, 2 * MEM_WIDTH), lambda i: (0, 0)),
        ],
        out_specs=[
            pl.BlockSpec((None, MEM_WIDTH, MEM_HEADS * n_mem), lambda i: (i, 0, 0)),
            pl.BlockSpec((None, MEM_HEADS * n_mem, MEM_WIDTH), lambda i: (i, 0, 0)),
        ],
        out_shape=[
            jax.ShapeDtypeStruct((bsz, MEM_WIDTH, MEM_HEADS * n_mem), BF16),
            jax.ShapeDtypeStruct((bsz, MEM_HEADS * n_mem, MEM_WIDTH), BF16),
        ],
        name="mem_kv",
    )(mem, g.reshape(1, d), b.reshape(1, d), w_bf)


def _conv_mix_kernel(h_ref, w_ref, dww_ref, dwb_ref, g_ref, b_ref, y_ref, qm_ref, gbuf, cbuf, *, tm, c):
    n_groups = c // LANES

    @pl.when(pl.program_id(1) == 0)
    def _():
        gbuf[:, 0:CONV_HALO, :] = jnp.zeros((n_groups, CONV_HALO, LANES), F32)

    @pl.when(pl.program_id(1) > 0)
    def _():
        gbuf[:, 0:CONV_HALO, :] = gbuf[:, tm:tm + CONV_HALO, :]

    u = jnp.dot(h_ref[...].astype(BF16), w_ref[...], preferred_element_type=F32)
    glu = u[:, :c] * _sigmoid(u[:, c:2 * c])
    for grp in range(n_groups):
        gbuf[grp, CONV_HALO:, :] = glu[:, grp * LANES:(grp + 1) * LANES]
    qm_ref[...] = u[:, 2 * c:].astype(BF16)

    first_tap = CONV_HALO - (CONV_WIDTH - 1)

    def rows(i, carry):
        r0 = pl.multiple_of(i * CONV_ROWS, CONV_ROWS)
        for grp in range(n_groups):
            src = gbuf.at[grp, pl.ds(r0, CONV_ROWS + CONV_HALO), :]
            dst = cbuf.at[grp, pl.ds(r0, CONV_ROWS), :]
            lanes = slice(grp * LANES, (grp + 1) * LANES)
            taps = [jnp.broadcast_to(dww_ref[j:j + 1, lanes], (8, LANES)) for j in range(CONV_WIDTH)]
            bias = jnp.broadcast_to(dwb_ref[:, lanes], (8, LANES))
            for r in range(CONV_STRIDE):
                acc = bias
                for j in range(CONV_WIDTH):
                    acc = acc + src[pl.ds(first_tap + j + r, 8, stride=CONV_STRIDE), :] * taps[j]
                dst[pl.ds(r, 8, stride=CONV_STRIDE), :] = acc
        conv = jnp.concatenate([cbuf[grp, pl.ds(r0, CONV_ROWS), :] for grp in range(n_groups)], axis=-1)
        yn = _layer_norm(conv, g_ref[...], b_ref[...])
        y_ref[pl.ds(r0, CONV_ROWS), :] = (yn * _sigmoid(yn)).astype(BF16)
        return carry

    lax.fori_loop(0, tm // CONV_ROWS, rows, 0)


def _conv_mix(h, w_bf, dw_w, dw_b, ln_g, ln_b):
    bsz, s, d = h.shape
    c = dw_w.shape[1]
    tm = _tile(s, TOKEN_TILE)
    n_out = w_bf.shape[1]
    row = lambda a: a.reshape(1, -1)
    return pl.pallas_call(
        functools.partial(_conv_mix_kernel, tm=tm, c=c),
        grid=(bsz, s // tm),
        in_specs=[
            pl.BlockSpec((None, tm, d), lambda b, i: (b, i, 0)),
            pl.BlockSpec((d, n_out), lambda b, i: (0, 0)),
            pl.BlockSpec((CONV_WIDTH, c), lambda b, i: (0, 0)),
            pl.BlockSpec((1, c), lambda b, i: (0, 0)),
            pl.BlockSpec((1, c), lambda b, i: (0, 0)),
            pl.BlockSpec((1, c), lambda b, i: (0, 0)),
        ],
        out_specs=[
            pl.BlockSpec((None, tm, c), lambda b, i: (b, i, 0)),
            pl.BlockSpec((None, tm, MEM_WIDTH), lambda b, i: (b, i, 0)),
        ],
        out_shape=[
            jax.ShapeDtypeStruct((bsz, s, c), BF16),
            jax.ShapeDtypeStruct((bsz, s, MEM_WIDTH), BF16),
        ],
        scratch_shapes=[pltpu.VMEM((c // LANES, tm + CONV_HALO, LANES), F32),
                        pltpu.VMEM((c // LANES, tm, LANES), F32)],
        compiler_params=pltpu.CompilerParams(
            dimension_semantics=("arbitrary", "arbitrary"), vmem_limit_bytes=VMEM_LIMIT),
        name="conv_mix",
    )(h, w_bf, dw_w, row(dw_b), row(ln_g), row(ln_b))


def _sb_proj_kernel(h_ref, w_ref, u_ref, *, c):
    u = jnp.dot(h_ref[...].astype(BF16), w_ref[...], preferred_element_type=F32)
    u_ref[:, :c] = (u[:, :c] * (HEAD_DIM ** -0.5)).astype(BF16)
    u_ref[:, c:] = u[:, c:].astype(BF16)


def _sb_proj(h, w_bf, c):
    bsz, s, d = h.shape
    tm = _tile(s, TOKEN_TILE)
    n_out = w_bf.shape[1]
    return pl.pallas_call(
        functools.partial(_sb_proj_kernel, c=c),
        grid=(bsz, s // tm),
        in_specs=[
            pl.BlockSpec((None, tm, d), lambda b, i: (b, i, 0)),
            pl.BlockSpec((d, n_out), lambda b, i: (0, 0)),
        ],
        out_specs=pl.BlockSpec((None, tm, n_out), lambda b, i: (b, i, 0)),
        out_shape=jax.ShapeDtypeStruct((bsz, s, n_out), BF16),
        compiler_params=pltpu.CompilerParams(
            dimension_semantics=("arbitrary", "arbitrary"), vmem_limit_bytes=VMEM_LIMIT),
        name="sb_proj",
    )(h, w_bf)


def _sb_attn_kernel(q_ref, k_ref, v_ref, o_ref, carry_ref, acc_ref, *, tq, tk, n_pairs):
    qi = pl.program_id(2)
    n_heads = 2 * n_pairs
    lane = lax.broadcasted_iota(I32, (tq, LANES), 1)
    k_lane = lax.broadcasted_iota(I32, (tk, LANES), 1)
    q_heads = []
    for pr in range(n_pairs):
        q2 = q_ref[:, pr * LANES:(pr + 1) * LANES]
        q_heads += [jnp.where(lane < HEAD_DIM, q2, jnp.zeros_like(q2)),
                    jnp.where(lane >= HEAD_DIM, q2, jnp.zeros_like(q2))]
    later = (lax.broadcasted_iota(I32, (tk, tk), 0) > lax.broadcasted_iota(I32, (tk, tk), 1)).astype(BF16)

    def block(kb, masked, r0, r1):
        n = r1 - r0
        start = pl.multiple_of(kb * tk, tk)
        if masked:
            causal = ((kb * tk + lax.broadcasted_iota(I32, (n, tk), 1))
                      < (qi * tq + r0 + lax.broadcasted_iota(I32, (n, tk), 0)))
        log_nots, log_betas, parts = [], [], []
        for pr in range(n_pairs):
            k2 = k_ref[pl.ds(start, tk), pr * LANES:(pr + 1) * LANES]
            z2 = lax.dot_general(jnp.concatenate([q[r0:r1] for q in q_heads[2 * pr:2 * pr + 2]], axis=0), k2,
                                 (((1,), (1,)), ((), ())), preferred_element_type=F32)
            for half in range(2):
                z = z2[half * n:(half + 1) * n] * LOG2_E
                sp = jnp.maximum(z, 0.0) + jnp.log2(1.0 + jnp.exp2(-jnp.abs(z)))
                log_not = -sp
                if masked:
                    log_not = jnp.where(causal, log_not, 0.0)
                hi = log_not.astype(BF16)
                parts += [hi, (log_not - hi.astype(F32)).astype(BF16)]
                log_nots.append(log_not)
                log_betas.append(z - sp)
        sums = jnp.dot(jnp.concatenate(parts, axis=0), later, preferred_element_type=F32)
        for pr in range(n_pairs):
            v2 = v_ref[pl.ds(start, tk), pr * LANES:(pr + 1) * LANES]
            ws = []
            for half in range(2):
                hd = 2 * pr + half
                carry = carry_ref[hd, r0:r1, :]
                tail = sums[2 * hd * n:(2 * hd + 1) * n] + sums[(2 * hd + 1) * n:(2 * hd + 2) * n] + carry
                w = jnp.exp2(log_betas[hd] + tail)
                if masked:
                    w = jnp.where(causal, w, 0.0)
                ws.append(w.astype(BF16))
                carry_ref[hd, r0:r1, :] = carry + jnp.sum(log_nots[hd], axis=1, keepdims=True)
            v_rows = jnp.concatenate([jnp.where(k_lane < HEAD_DIM, v2, jnp.zeros_like(v2)),
                                      jnp.where(k_lane >= HEAD_DIM, v2, jnp.zeros_like(v2))], axis=0)
            acc_ref[pr, r0:r1, :] += jnp.dot(jnp.concatenate(ws, axis=1), v_rows, preferred_element_type=F32)

    carry_ref[...] = jnp.zeros_like(carry_ref)
    acc_ref[...] = jnp.zeros_like(acc_ref)
    n_diag = tq // tk
    for j in range(n_diag - 1, -1, -1):
        block(qi * n_diag + j, True, j * tk, tq)

    row_count = 1 + lax.broadcasted_iota(I32, (tq, tk), 0)

    def live_rows():
        top = carry_ref[0]
        for hd in range(1, n_heads):
            top = jnp.maximum(top, carry_ref[hd])
        return jnp.max(jnp.where(top > SB_SKIP_BELOW * LOG2_E, row_count, 0))

    def cond(state):
        kb, n_live = state
        return jnp.logical_and(kb >= 0, n_live > 0)

    def body(state):
        kb, n_live = state

        bounds = [0] + [r for r in SB_ROW_PREFIXES if r < tq] + [tq]
        for lo, hi in zip(bounds[:-1], bounds[1:]):
            @pl.when(jnp.logical_and(n_live > lo, n_live <= hi))
            def _():
                block(kb, False, 0, hi)

        return kb - 1, live_rows()

    lax.while_loop(cond, body, (qi * n_diag - 1, live_rows()))
    o_ref[...] = jnp.concatenate([acc_ref[pr] for pr in range(n_pairs)], axis=1).astype(BF16)


def _sb_attn(u, c):
    bsz, s, _ = u.shape
    tq = _tile(s, SB_QUERY_ROWS)
    tk = _tile(tq, SB_KEY_ROWS)
    width = SB_PAIRS_PER_STEP * LANES
    n_groups = c // width
    return pl.pallas_call(
        functools.partial(_sb_attn_kernel, tq=tq, tk=tk, n_pairs=SB_PAIRS_PER_STEP),
        grid=(bsz, n_groups, s // tq),
        in_specs=[
            pl.BlockSpec((None, tq, width), lambda b, p, i: (b, i, p)),
            pl.BlockSpec((None, s, width), lambda b, p, i: (b, 0, n_groups + p), pipeline_mode=pl.Buffered(1)),
            pl.BlockSpec((None, s, width), lambda b, p, i: (b, 0, 2 * n_groups + p), pipeline_mode=pl.Buffered(1)),
        ],
        out_specs=pl.BlockSpec((None, tq, width), lambda b, p, i: (b, i, p)),
        out_shape=jax.ShapeDtypeStruct((bsz, s, c), BF16),
        scratch_shapes=[pltpu.VMEM((2 * SB_PAIRS_PER_STEP, tq, tk), F32),
                        pltpu.VMEM((SB_PAIRS_PER_STEP, tq, LANES), F32)],
        compiler_params=pltpu.CompilerParams(
            dimension_semantics=("arbitrary", "arbitrary", "arbitrary"), vmem_limit_bytes=VMEM_LIMIT),
        name="sb_attn",
    )(u, u, u)


def _mix_out_kernel(h_ref, y_ref, qm_ref, kx_ref, vx_ref, wo_ref, g_ref, b_ref, wr_ref, br_ref,
                    h1_ref, idx_ref, gate_ref, rank_ref, cnt_ref, *, tm, c, n_mem, alpha):
    def rows(i, carry):
        sl = pl.ds(pl.multiple_of(i * MIX_ROWS, MIX_ROWS), MIX_ROWS)
        s = jnp.dot(qm_ref[sl, :], kx_ref[...], preferred_element_type=F32)
        probs = []
        for hd in range(MEM_HEADS):
            sh = s[:, hd * n_mem:(hd + 1) * n_mem]
            e = jnp.exp(sh - jnp.max(sh, axis=-1, keepdims=True))
            probs.append((e / jnp.sum(e, axis=-1, keepdims=True)).astype(BF16))
        y_mem = jnp.dot(jnp.concatenate(probs, axis=-1), vx_ref[...], preferred_element_type=F32)
        y = (jnp.dot(y_ref[sl, :], wo_ref[0:c, :], preferred_element_type=F32)
             + jnp.dot(y_mem.astype(BF16), wo_ref[c:, :], preferred_element_type=F32))
        h1_ref[sl, :] = _layer_norm(alpha * h_ref[sl, :] + y, g_ref[...], b_ref[...])
        return carry

    lax.fori_loop(0, tm // MIX_ROWS, rows, 0)
    h1 = h1_ref[...]

    logits = lax.dot_general(wr_ref[...], h1.astype(BF16), (((1,), (1,)), ((), ())),
                             preferred_element_type=F32) + br_ref[...]
    e_iota = lax.broadcasted_iota(I32, (N_EXPERTS, tm), 0)
    vals, sels = [], []
    for _ in range(TOP_K):
        m = jnp.max(logits, axis=0, keepdims=True)
        sel = jnp.min(jnp.where(logits == m, e_iota, N_EXPERTS), axis=0, keepdims=True)
        vals.append(m)
        sels.append(sel)
        logits = jnp.where(e_iota == sel, -jnp.inf, logits)
    exps = [jnp.exp(v - vals[0]) for v in vals]
    denom = exps[0] + exps[1] + exps[2] + exps[3]
    gate_ref[...] = jnp.concatenate([e / denom for e in exps], axis=0)
    idx_ref[...] = jnp.concatenate(sels, axis=0)

    onehot = jnp.zeros((N_EXPERTS, tm), F32)
    for sel in sels:
        onehot = onehot + (e_iota == sel).astype(F32)
    earlier = (lax.broadcasted_iota(I32, (tm, tm), 0) < lax.broadcasted_iota(I32, (tm, tm), 1)).astype(BF16)
    rank_e = jnp.dot(onehot.astype(BF16), earlier, preferred_element_type=F32)
    rank_ref[...] = jnp.concatenate(
        [jnp.sum(jnp.where(e_iota == sel, rank_e, 0.0), axis=0, keepdims=True) for sel in sels],
        axis=0).astype(I32)
    cnt_ref[...] = jnp.sum(onehot, axis=1, keepdims=True).astype(I32)


def _mix_out(h, y_mix, qm_src, qm_block, kx, vx, wo_bf, ln_g, ln_b, wr_t_bf, b_router, alpha):
    bsz, s, d = h.shape
    c = y_mix.shape[-1]
    n_mem = vx.shape[1] // MEM_HEADS
    tm = _tile(s, TOKEN_TILE)
    n_s = s // tm
    n_tok = bsz * s
    tok_spec = pl.BlockSpec((TOP_K, tm), lambda b, i: (0, b * n_s + i))
    return pl.pallas_call(
        functools.partial(_mix_out_kernel, tm=tm, c=c, n_mem=n_mem, alpha=alpha),
        grid=(bsz, n_s),
        in_specs=[
            pl.BlockSpec((None, tm, d), lambda b, i: (b, i, 0)),
            pl.BlockSpec((None, tm, c), lambda b, i: (b, i, 0)),
            pl.BlockSpec((None, tm, MEM_WIDTH), lambda b, i: (b, i, qm_block)),
            pl.BlockSpec((None, MEM_WIDTH, MEM_HEADS * n_mem), lambda b, i: (b, 0, 0)),
            pl.BlockSpec((None, MEM_HEADS * n_mem, MEM_WIDTH), lambda b, i: (b, 0, 0)),
            pl.BlockSpec((c + MEM_WIDTH, d), lambda b, i: (0, 0)),
            pl.BlockSpec((1, d), lambda b, i: (0, 0)),
            pl.BlockSpec((1, d), lambda b, i: (0, 0)),
            pl.BlockSpec((N_EXPERTS, d), lambda b, i: (0, 0)),
            pl.BlockSpec((N_EXPERTS, 1), lambda b, i: (0, 0)),
        ],
        out_specs=[
            pl.BlockSpec((None, tm, d), lambda b, i: (b, i, 0)),
            tok_spec, tok_spec, tok_spec,
            pl.BlockSpec((None, N_EXPERTS, 1), lambda b, i: (b * n_s + i, 0, 0)),
        ],
        out_shape=[
            jax.ShapeDtypeStruct((bsz, s, d), F32),
            jax.ShapeDtypeStruct((TOP_K, n_tok), I32),
            jax.ShapeDtypeStruct((TOP_K, n_tok), F32),
            jax.ShapeDtypeStruct((TOP_K, n_tok), I32),
            jax.ShapeDtypeStruct((bsz * n_s, N_EXPERTS, 1), I32),
        ],
        compiler_params=pltpu.CompilerParams(
            dimension_semantics=("arbitrary", "arbitrary"), vmem_limit_bytes=VMEM_LIMIT),
        name="mix_out",
    )(h, y_mix, qm_src, kx, vx, wo_bf, ln_g.reshape(1, d), ln_b.reshape(1, d), wr_t_bf,
      b_router.reshape(N_EXPERTS, 1))


def _copy_rows(src_ref, src_row, dst_ref, dst_row, count, sem, max_bit, wait):
    def bits(hi, lo):
        for bit in range(hi, lo - 1, -1):
            size = 1 << bit
            off = lax.shift_left(lax.shift_right_logical(count, bit + 1), bit + 1)

            @pl.when(jnp.bitwise_and(lax.shift_right_logical(count, bit), 1) == 1)
            def _():
                cp = pltpu.make_async_copy(src_ref.at[pl.ds(pl.multiple_of(src_row + off, RUN_ALIGN), size)],
                                           dst_ref.at[pl.ds(pl.multiple_of(dst_row + off, RUN_ALIGN), size)], sem)
                if wait:
                    cp.wait()
                else:
                    cp.start()

    low_bit = RUN_ALIGN.bit_length() - 1
    split = min(max_bit, COMMON_RUN_BIT)
    if max_bit > split:
        pl.when(count >= (1 << (split + 1)))(lambda: bits(max_bit, split + 1))
    bits(split, low_bit)


def _tile_runs(tile, tc_ref, ls_ref, base_ref, local_ref, global_ref, sem, max_bit, to_global, wait):
    if wait:
        last = tile * N_EXPERTS + N_EXPERTS - 1
        total = ls_ref[last] + tc_ref[last]
        total_bit = local_ref.shape[0].bit_length() - 1
        if to_global:
            _copy_rows(local_ref, 0, global_ref, 0, total, sem, total_bit, True)
        else:
            _copy_rows(global_ref, 0, local_ref, 0, total, sem, total_bit, True)
        return

    def per_expert(e, carry):
        k = tile * N_EXPERTS + e
        if to_global:
            _copy_rows(local_ref, ls_ref[k], global_ref, base_ref[k], tc_ref[k], sem, max_bit, wait)
        else:
            _copy_rows(global_ref, base_ref[k], local_ref, ls_ref[k], tc_ref[k], sem, max_bit, wait)
        return carry

    lax.fori_loop(0, N_EXPERTS, per_expert, 0)


def _dispatch_kernel(tc_ref, ls_ref, base_ref, fill_lo_ref, fill_hi_ref, na_ref,
                     pos_ref, h_ref, xs_ref, buf, zero_ref, sems, fill_sem, *, tm, tmb):
    t = pl.program_id(0)
    n_t = pl.num_programs(0)
    slot = lax.rem(t, 2)
    max_bit = tm.bit_length() - 1

    def runs(tile, slot_, wait):
        _tile_runs(tile, tc_ref, ls_ref, base_ref, buf.at[slot_], xs_ref, sems.at[slot_], max_bit, True, wait)

    @pl.when(t >= 2)
    def _():
        runs(t - 2, slot, True)

    x = h_ref[...].astype(BF16)
    pos = pos_ref[...]

    def sort_rows(i, carry):
        r0 = pl.multiple_of(i * SORT_ROWS, SORT_ROWS)
        row = r0 + lax.broadcasted_iota(I32, (SORT_ROWS, tm), 0)
        perm = jnp.zeros((SORT_ROWS, tm), F32)
        for k in range(TOP_K):
            perm = jnp.where(row == pos[k:k + 1, :], 1.0, perm)
        rows = jnp.dot(perm.astype(BF16), x, preferred_element_type=F32)
        buf[slot, pl.ds(r0, SORT_ROWS), :] = _pack_halves(rows)
        return carry

    lax.fori_loop(0, _sorted_rows(tm) // SORT_ROWS, sort_rows, 0)
    runs(t, slot, False)

    def zero_fill(wait):
        pad_bit = tmb.bit_length() - 2

        def per_expert(e, carry):
            _copy_rows(zero_ref, 0, xs_ref, fill_lo_ref[e], fill_hi_ref[e] - fill_lo_ref[e],
                       fill_sem, pad_bit, wait)
            return carry

        def per_block(blk, carry):
            cp = pltpu.make_async_copy(zero_ref, xs_ref.at[pl.ds(blk * tmb, tmb)], fill_sem)
            if wait:
                cp.wait()
            else:
                cp.start()
            return carry

        lax.fori_loop(0, N_EXPERTS, per_expert, 0)
        lax.fori_loop(na_ref[0], xs_ref.shape[0] // tmb, per_block, 0)

    @pl.when(t == 0)
    def _():
        zero_ref[...] = jnp.zeros_like(zero_ref)
        zero_fill(False)

    @pl.when(t == n_t - 1)
    def _():
        @pl.when(t >= 1)
        def _():
            runs(t - 1, 1 - slot, True)
        runs(t, slot, True)
        zero_fill(True)


def _dispatch(h1_flat, pos, tables, fill_lo, fill_hi, n_active, n_rows, tm, tmb):
    n_tok, d = h1_flat.shape
    n_t = n_tok // tm
    tc, ls, base = tables
    return pl.pallas_call(
        functools.partial(_dispatch_kernel, tm=tm, tmb=tmb),
        grid_spec=pltpu.PrefetchScalarGridSpec(
            num_scalar_prefetch=6,
            grid=(n_t,),
            in_specs=[
                pl.BlockSpec((TOP_K, tm), lambda i, *_: (0, i)),
                pl.BlockSpec((tm, d), lambda i, *_: (i, 0)),
            ],
            out_specs=pl.BlockSpec(memory_space=pl.ANY),
            scratch_shapes=[
                pltpu.VMEM((2, _sorted_rows(tm), d // 2), I32),
                pltpu.VMEM((tmb, d // 2), I32),
                pltpu.SemaphoreType.DMA((2,)),
                pltpu.SemaphoreType.DMA,
            ],
        ),
        out_shape=jax.ShapeDtypeStruct((n_rows, d // 2), I32),
        compiler_params=pltpu.CompilerParams(
            dimension_semantics=("arbitrary",), vmem_limit_bytes=VMEM_LIMIT),
        name="moe_dispatch",
    )(tc, ls, base, fill_lo, fill_hi, n_active, pos, h1_flat)


def _expert_kernel(ord_ref, seq_ref, nseq_ref, na_ref, valid_ref, xs_ref, wgu_hbm, bgu_ref, wd_hbm, bd_ref, ys_ref,
                   wgu_f32, wd_f32, wgu_bf, wd_bf, sems, *, f):
    i = pl.program_id(0)
    n_active = na_ref[0]
    j = jnp.minimum(i, n_active - 1)
    prev = jnp.maximum(j - 1, 0)
    o = ord_ref[j]
    new_expert = jnp.logical_or(i == 0, o != ord_ref[prev])

    def fetch(o_, slot, wait):
        e = seq_ref[o_]
        copies = []
        for src, dst, parts in ((wgu_hbm, wgu_f32, WEIGHT_DMA_PARTS), (wd_hbm, wd_f32, WEIGHT_DMA_PARTS // 2)):
            rows = src.shape[1] // parts
            for p in range(parts):
                copies.append(pltpu.make_async_copy(src.at[e, pl.ds(p * rows, rows)],
                                                    dst.at[slot, pl.ds(p * rows, rows)], sems.at[slot]))
        for cp in copies:
            if wait:
                cp.wait()
            else:
                cp.start()

    @pl.when(i == 0)
    def _():
        fetch(0, 0, False)

    @pl.when(jnp.logical_and(i < n_active, new_expert))
    def _():
        slot = lax.rem(o, 2)
        fetch(o, slot, True)

        @pl.when(o + 1 < nseq_ref[0])
        def _():
            fetch(o + 1, 1 - slot, False)

        rows = 128
        for r in range(0, wgu_bf.shape[0], rows):
            wgu_bf[r:r + rows, :] = wgu_f32[slot, r:r + rows, :].astype(BF16)
        for r in range(0, wd_bf.shape[0], rows):
            wd_bf[r:r + rows, :] = wd_f32[slot, r:r + rows, :].astype(BF16)

    valid = valid_ref[i]

    def rows_pass(r, n):
        x = _unpack_halves(xs_ref[r:r + n, :])
        gu = jnp.dot(x, wgu_bf[...], preferred_element_type=F32) + bgu_ref[...]
        gate = jnp.minimum(gu[:, :f], SWIGLU_LIMIT)
        up = jnp.clip(gu[:, f:], -SWIGLU_LIMIT, SWIGLU_LIMIT)
        act = (up + 1.0) * gate * _sigmoid(SWIGLU_ALPHA * gate)
        y = jnp.dot(act.astype(BF16), wd_bf[...], preferred_element_type=F32) + bd_ref[...]
        ys_ref[r:r + n, :] = _pack_halves(y.astype(BF16).astype(F32))

    def zero_rows(r, n):
        ys_ref[r:r + n, :] = jnp.zeros((n, ys_ref.shape[1]), I32)

    for r in range(0, xs_ref.shape[0], 2 * EXPERT_SUB):
        @pl.when(valid > r + EXPERT_SUB)
        def _():
            rows_pass(r, 2 * EXPERT_SUB)

        @pl.when(jnp.logical_and(valid > r, valid <= r + EXPERT_SUB))
        def _():
            rows_pass(r, EXPERT_SUB)
            zero_rows(r + EXPERT_SUB, EXPERT_SUB)

        @pl.when(valid <= r)
        def _():
            zero_rows(r, 2 * EXPERT_SUB)


def _experts(xs, block_ord, expert_seq, n_seq, n_active, block_valid, w_gate_up, b_gate_up, w_down, b_down, tmb):
    n_rows, half = xs.shape
    n_e, d, f2 = w_gate_up.shape
    f = f2 // 2
    nb = n_rows // tmb

    def blk(i, refs):
        return jnp.minimum(i, refs[3][0] - 1)

    def expert(i, refs):
        return refs[1][refs[0][blk(i, refs)]]

    return pl.pallas_call(
        functools.partial(_expert_kernel, f=f),
        grid_spec=pltpu.PrefetchScalarGridSpec(
            num_scalar_prefetch=5,
            grid=(nb,),
            in_specs=[
                pl.BlockSpec((tmb, half), lambda i, *refs: (blk(i, refs), 0)),
                pl.BlockSpec(memory_space=pl.ANY),
                pl.BlockSpec((None, 1, f2), lambda i, *refs: (expert(i, refs), 0, 0)),
                pl.BlockSpec(memory_space=pl.ANY),
                pl.BlockSpec((None, 1, d), lambda i, *refs: (expert(i, refs), 0, 0)),
            ],
            out_specs=pl.BlockSpec((tmb, half), lambda i, *refs: (i, 0)),
            scratch_shapes=[
                pltpu.VMEM((2, d, f2), F32), pltpu.VMEM((2, f, d), F32),
                pltpu.VMEM((d, f2), BF16), pltpu.VMEM((f, d), BF16),
                pltpu.SemaphoreType.DMA((2,)),
            ],
        ),
        out_shape=jax.ShapeDtypeStruct((n_rows, half), I32),
        compiler_params=pltpu.CompilerParams(
            dimension_semantics=("arbitrary",), vmem_limit_bytes=EXPERT_VMEM_LIMIT),
        name="moe_experts",
    )(block_ord, expert_seq, n_seq, n_active, block_valid, xs, w_gate_up, b_gate_up.reshape(n_e, 1, f2), w_down,
      b_down.reshape(n_e, 1, d))


def _combine_kernel(tc_ref, ls_ref, base_ref, pos_ref, gate_ref, h_ref, ys_ref, g_ref, b_ref, o_ref,
                    buf, sems, *, tm, alpha):
    t = pl.program_id(0)
    n_t = pl.num_programs(0)
    slot = lax.rem(t, 2)
    max_bit = tm.bit_length() - 1

    def runs(tile, slot_, wait):
        _tile_runs(tile, tc_ref, ls_ref, base_ref, buf.at[slot_], ys_ref, sems.at[slot_], max_bit, False, wait)

    @pl.when(t == 0)
    def _():
        buf[...] = jnp.zeros_like(buf)
        runs(t, slot, False)

    @pl.when(t + 1 < n_t)
    def _():
        runs(t + 1, 1 - slot, False)

    runs(t, slot, True)

    pos = pos_ref[...]
    gates = gate_ref[...]
    col = lax.broadcasted_iota(I32, (tm, _sorted_rows(tm)), 1)
    weights = jnp.zeros((tm, _sorted_rows(tm)), F32)
    for k in range(TOP_K):
        weights = jnp.where(col == pos[:, k:k + 1], gates[:, k:k + 1], weights)
    y = jnp.dot(weights.astype(BF16), _unpack_halves(buf[slot]), preferred_element_type=F32)
    o_ref[...] = _layer_norm(alpha * h_ref[...] + y, g_ref[...], b_ref[...])


def _combine(h1_flat, ys, pos_t, gates_t, tables, ln_g, ln_b, alpha, tm):
    n_tok, d = h1_flat.shape
    n_t = n_tok // tm
    tc, ls, base = tables
    return pl.pallas_call(
        functools.partial(_combine_kernel, tm=tm, alpha=alpha),
        grid_spec=pltpu.PrefetchScalarGridSpec(
            num_scalar_prefetch=3,
            grid=(n_t,),
            in_specs=[
                pl.BlockSpec((tm, TOP_K), lambda i, *_: (i, 0)),
                pl.BlockSpec((tm, TOP_K), lambda i, *_: (i, 0)),
                pl.BlockSpec((tm, d), lambda i, *_: (i, 0)),
                pl.BlockSpec(memory_space=pl.ANY),
                pl.BlockSpec((1, d), lambda i, *_: (0, 0)),
                pl.BlockSpec((1, d), lambda i, *_: (0, 0)),
            ],
            out_specs=pl.BlockSpec((tm, d), lambda i, *_: (i, 0)),
            scratch_shapes=[pltpu.VMEM((2, _sorted_rows(tm), d // 2), I32), pltpu.SemaphoreType.DMA((2,))],
        ),
        out_shape=jax.ShapeDtypeStruct((n_tok, d), F32),
        compiler_params=pltpu.CompilerParams(
            dimension_semantics=("arbitrary",), vmem_limit_bytes=VMEM_LIMIT),
        name="moe_combine",
    )(tc, ls, base, pos_t, gates_t, h1_flat, ys, ln_g.reshape(1, d), ln_b.reshape(1, d))


def _moe(h1, idx, gates, rank, tile_counts, layer, w_gate_up, b_gate_up, w_down, b_down, ln_g, ln_b, alpha):
    bsz, s, d = h1.shape
    n_tok = bsz * s
    tm = _tile(s, TOKEN_TILE)
    n_t = n_tok // tm
    tmb = EXPERT_ROWS
    n_rows = n_tok * TOP_K + n_t * N_EXPERTS * (RUN_ALIGN - 1) + N_EXPERTS * tmb
    n_rows = (n_rows + tmb - 1) // tmb * tmb
    nb = n_rows // tmb

    tile_counts = (tile_counts + RUN_ALIGN - 1) // RUN_ALIGN * RUN_ALIGN
    counts = jnp.sum(tile_counts, axis=0)
    padded = (counts + tmb - 1) // tmb * tmb
    pad_end = jnp.cumsum(padded).astype(I32)
    pad_start = pad_end - padded
    base = pad_start[None, :] + jnp.cumsum(tile_counts, axis=0) - tile_counts
    local = jnp.cumsum(tile_counts, axis=1) - tile_counts
    experts = jnp.arange(N_EXPERTS, dtype=I32)
    local_tok = jnp.repeat(local, tm, axis=0)
    pos = rank + jnp.sum(jnp.where(idx[:, :, None] == experts, local_tok[None], 0), axis=-1).astype(I32)
    block_start = jnp.arange(nb, dtype=I32) * tmb
    block_expert = jnp.minimum(
        jnp.sum((block_start[:, None] >= pad_end[None, :]).astype(I32), axis=1), N_EXPERTS - 1).astype(I32)
    n_active = (pad_end[-1:] // tmb).astype(I32)
    tables = (tile_counts.reshape(-1).astype(I32), local.reshape(-1).astype(I32), base.reshape(-1).astype(I32))

    h1_flat = h1.reshape(n_tok, d)
    xs = _dispatch(h1_flat, pos, tables, pad_start + counts, pad_end, n_active, n_rows, tm, tmb)
    merge = lambda w: w.reshape((w.shape[0] * w.shape[1],) + w.shape[2:])
    present = padded > 0
    expert_ord = jnp.cumsum(present.astype(I32)) - 1
    expert_seq = (jnp.argsort(jnp.logical_not(present), stable=True) + layer * N_EXPERTS).astype(I32)
    n_seq = jnp.sum(present.astype(I32)).reshape(1)
    block_valid = jnp.clip((pad_start + counts)[block_expert] - block_start, 0, tmb).astype(I32)
    ys = _experts(xs, expert_ord[block_expert].astype(I32), expert_seq, n_seq, n_active, block_valid,
                  merge(w_gate_up), merge(b_gate_up), merge(w_down), merge(b_down), tmb)
    out = _combine(h1_flat, ys, pos.T, gates.T, tables, ln_g, ln_b, alpha, tm)
    return out.reshape(bsz, s, d)


@jax.jit
def _forward(x, mem, mem_ln_g, mem_ln_b, w_mem_kv, w_in_conv, conv_dw_w, conv_dw_b, conv_ln_g,
             conv_ln_b, w_in_sb, w_mix_out, ln_mix_g, ln_mix_b, w_router, b_router, w_gate_up,
             b_gate_up, w_down, b_down, ln_moe_g, ln_moe_b):
    depth = w_mix_out.shape[0]
    alpha = float((2 * depth) ** 0.25)
    c = conv_dw_w.shape[-1]
    kx, vx = _mem_kv(mem, mem_ln_g, mem_ln_b, w_mem_kv.astype(BF16))

    h = x
    for i in range(depth):
        j = i // 2
        if i % 2 == 0:
            y_mix, qm = _conv_mix(h, w_in_conv[j].astype(BF16), conv_dw_w[j], conv_dw_b[j],
                                  conv_ln_g[j], conv_ln_b[j])
            qm_src, qm_block = qm, 0
        else:
            u = _sb_proj(h, w_in_sb[j].astype(BF16), c)
            y_mix = _sb_attn(u, c)
            qm_src, qm_block = u, (3 * c) // MEM_WIDTH
        h1, idx, gates, rank, cnt = _mix_out(
            h, y_mix, qm_src, qm_block, kx, vx, w_mix_out[i].astype(BF16), ln_mix_g[i], ln_mix_b[i],
            w_router[i].T.astype(BF16), b_router[i], alpha)
        h = _moe(h1, idx, gates, rank, cnt[:, :, 0], i, w_gate_up, b_gate_up, w_down, b_down,
                 ln_moe_g[i], ln_moe_b[i], alpha)
    return h


def kernel(x, mem, mem_ln_g, mem_ln_b, w_mem_kv, w_in_conv, conv_dw_w, conv_dw_b, conv_ln_g, conv_ln_b,
           w_in_sb, w_mix_out, ln_mix_g, ln_mix_b, w_router, b_router, w_gate_up, b_gate_up, w_down,
           b_down, ln_moe_g, ln_moe_b):
    return _forward(x, mem, mem_ln_g, mem_ln_b, w_mem_kv, w_in_conv, conv_dw_w, conv_dw_b, conv_ln_g,
                    conv_ln_b, w_in_sb, w_mix_out, ln_mix_g, ln_mix_b, w_router, b_router, w_gate_up,
                    b_gate_up, w_down, b_down, ln_moe_g, ln_moe_b)
```

```python
import functools

import jax
import jax.numpy as jnp
from jax import lax
from jax.experimental import pallas as pl
from jax.experimental.pallas import tpu as pltpu

F32 = jnp.float32
BF16 = jnp.bfloat16
I32 = jnp.int32

HEAD_DIM = 64
MEM_HEADS = 4
MEM_WIDTH = MEM_HEADS * HEAD_DIM
CONV_WIDTH = 31
N_EXPERTS = 32
TOP_K = 4
SWIGLU_LIMIT = 7.0
SWIGLU_ALPHA = 1.702
LN_EPS = 1e-5

LANES = 128
CONV_HALO = 32
CONV_STRIDE = 4
CONV_ROWS = 8 * CONV_STRIDE
SB_SKIP_BELOW = -104.0
LOG2_E = 1.4426950408889634
SB_PAIRS_PER_STEP = 6
SB_QUERY_ROWS = 256
SB_KEY_ROWS = 128
SB_ROW_PREFIXES = (64, 192)
TOKEN_TILE = 512
MIX_ROWS = 256
EXPERT_ROWS = 1024
EXPERT_SUB = 256
SORT_ROWS = 256
RUN_ALIGN = 8
COMMON_RUN_BIT = 6


def _sorted_rows(tm):
    rows = TOP_K * tm + N_EXPERTS * RUN_ALIGN
    return (rows + SORT_ROWS - 1) // SORT_ROWS * SORT_ROWS
WEIGHT_DMA_PARTS = 4
VMEM_LIMIT = 48 * 1024 * 1024
EXPERT_VMEM_LIMIT = 60 * 1024 * 1024
HIGH_HALF = -65536


def _layer_norm(x, g, b):
    mu = jnp.mean(x, axis=-1, keepdims=True)
    xc = x - mu
    var = jnp.mean(xc * xc, axis=-1, keepdims=True)
    return xc * lax.rsqrt(var + LN_EPS) * g + b


def _sigmoid(x):
    return 1.0 / (1.0 + jnp.exp(-x))


def _tile(n, want):
    t = min(n, want)
    assert n % t == 0, (n, t)
    return t


def _pack_halves(x):
    n = x.shape[1] // 2
    lo = lax.bitcast_convert_type(x[:, :n], I32)
    hi = lax.bitcast_convert_type(x[:, n:], I32)
    return jnp.bitwise_or(jnp.bitwise_and(hi, HIGH_HALF), lax.shift_right_logical(lo, 16))


def _unpack_halves(p):
    lo = lax.bitcast_convert_type(lax.shift_left(p, 16), F32)
    hi = lax.bitcast_convert_type(jnp.bitwise_and(p, HIGH_HALF), F32)
    return jnp.concatenate([lo, hi], axis=1).astype(BF16)


def _mem_kv_kernel(mem_ref, g_ref, b_ref, w_ref, kx_ref, vx_ref):
    n_mem = mem_ref.shape[0]
    xn = _layer_norm(mem_ref[...], g_ref[...], b_ref[...])
    kv = jnp.dot(xn.astype(BF16), w_ref[...], preferred_element_type=F32)
    k_t = (kv[:, :MEM_WIDTH] * (HEAD_DIM ** -0.5)).T
    v = kv[:, MEM_WIDTH:]
    k_head = lax.broadcasted_iota(I32, (MEM_WIDTH, n_mem), 0) // HEAD_DIM
    v_head = lax.broadcasted_iota(I32, (n_mem, MEM_WIDTH), 1) // HEAD_DIM
    for hd in range(MEM_HEADS):
        kx_ref[:, hd * n_mem:(hd + 1) * n_mem] = jnp.where(k_head == hd, k_t, 0.0).astype(BF16)
        vx_ref[hd * n_mem:(hd + 1) * n_mem, :] = jnp.where(v_head == hd, v, 0.0).astype(BF16)


def _mem_kv(mem, g, b, w_bf):
    bsz, n_mem, d = mem.shape
    return pl.pallas_call(
        _mem_kv_kernel,
        grid=(bsz,),
        in_specs=[
            pl.BlockSpec((None, n_mem, d), lambda i: (i, 0, 0)),
            pl.BlockSpec((1, d), lambda i: (0, 0)),
            pl.BlockSpec((1, d), lambda i: (0, 0)),
            pl.BlockSpec((d, 2 * MEM_WIDTH), lambda i: (0, 0)),
        ],
        out_specs=[
            pl.BlockSpec((None, MEM_WIDTH, MEM_HEADS * n_mem), lambda i: (i, 0, 0)),
            pl.BlockSpec((None, MEM_HEADS * n_mem, MEM_WIDTH), lambda i: (i, 0, 0)),
        ],
        out_shape=[
            jax.ShapeDtypeStruct((bsz, MEM_WIDTH, MEM_HEADS * n_mem), BF16),
            jax.ShapeDtypeStruct((bsz, MEM_HEADS * n_mem, MEM_WIDTH), BF16),
        ],
        name="mem_kv",
    )(mem, g.reshape(1, d), b.reshape(1, d), w_bf)


def _conv_mix_kernel(h_ref, w_ref, dww_ref, dwb_ref, g_ref, b_ref, y_ref, qm_ref, gbuf, cbuf, *, tm, c):
    n_groups = c // LANES

    @pl.when(pl.program_id(1) == 0)
    def _():
        gbuf[:, 0:CONV_HALO, :] = jnp.zeros((n_groups, CONV_HALO, LANES), F32)

    @pl.when(pl.program_id(1) > 0)
    def _():
        gbuf[:, 0:CONV_HALO, :] = gbuf[:, tm:tm + CONV_HALO, :]

    u = jnp.dot(h_ref[...].astype(BF16), w_ref[...], preferred_element_type=F32)
    glu = u[:, :c] * _sigmoid(u[:, c:2 * c])
    for grp in range(n_groups):
        gbuf[grp, CONV_HALO:, :] = glu[:, grp * LANES:(grp + 1) * LANES]
    qm_ref[...] = u[:, 2 * c:].astype(BF16)

    first_tap = CONV_HALO - (CONV_WIDTH - 1)

    def rows(i, carry):
        r0 = pl.multiple_of(i * CONV_ROWS, CONV_ROWS)
        for grp in range(n_groups):
            src = gbuf.at[grp, pl.ds(r0, CONV_ROWS + CONV_HALO), :]
            dst = cbuf.at[grp, pl.ds(r0, CONV_ROWS), :]
            lanes = slice(grp * LANES, (grp + 1) * LANES)
            taps = [jnp.broadcast_to(dww_ref[j:j + 1, lanes], (8, LANES)) for j in range(CONV_WIDTH)]
            bias = jnp.broadcast_to(dwb_ref[:, lanes], (8, LANES))
            for r in range(CONV_STRIDE):
                acc = bias
                for j in range(CONV_WIDTH):
                    acc = acc + src[pl.ds(first_tap + j + r, 8, stride=CONV_STRIDE), :] * taps[j]
                dst[pl.ds(r, 8, stride=CONV_STRIDE), :] = acc
        conv = jnp.concatenate([cbuf[grp, pl.ds(r0, CONV_ROWS), :] for grp in range(n_groups)], axis=-1)
        yn = _layer_norm(conv, g_ref[...], b_ref[...])
        y_ref[pl.ds(r0, CONV_ROWS), :] = (yn * _sigmoid(yn)).astype(BF16)
        return carry

    lax.fori_loop(0, tm // CONV_ROWS, rows, 0)


def _conv_mix(h, w_bf, dw_w, dw_b, ln_g, ln_b):
    bsz, s, d = h.shape
    c = dw_w.shape[1]
    tm = _tile(s, TOKEN_TILE)
    n_out = w_bf.shape[1]
    row = lambda a: a.reshape(1, -1)
    return pl.pallas_call(
        functools.partial(_conv_mix_kernel, tm=tm, c=c),
        grid=(bsz, s // tm),
        in_specs=[
            pl.BlockSpec((None, tm, d), lambda b, i: (b, i, 0)),
            pl.BlockSpec((d, n_out), lambda b, i: (0, 0)),
            pl.BlockSpec((CONV_WIDTH, c), lambda b, i: (0, 0)),
            pl.BlockSpec((1, c), lambda b, i: (0, 0)),
            pl.BlockSpec((1, c), lambda b, i: (0, 0)),
            pl.BlockSpec((1, c), lambda b, i: (0, 0)),
        ],
        out_specs=[
            pl.BlockSpec((None, tm, c), lambda b, i: (b, i, 0)),
            pl.BlockSpec((None, tm, MEM_WIDTH), lambda b, i: (b, i, 0)),
        ],
        out_shape=[
            jax.ShapeDtypeStruct((bsz, s, c), BF16),
            jax.ShapeDtypeStruct((bsz, s, MEM_WIDTH), BF16),
        ],
        scratch_shapes=[pltpu.VMEM((c // LANES, tm + CONV_HALO, LANES), F32),
                        pltpu.VMEM((c // LANES, tm, LANES), F32)],
        compiler_params=pltpu.CompilerParams(
            dimension_semantics=("arbitrary", "arbitrary"), vmem_limit_bytes=VMEM_LIMIT),
        name="conv_mix",
    )(h, w_bf, dw_w, row(dw_b), row(ln_g), row(ln_b))


def _sb_proj_kernel(h_ref, w_ref, u_ref, *, c):
    u = jnp.dot(h_ref[...].astype(BF16), w_ref[...], preferred_element_type=F32)
    u_ref[:, :c] = (u[:, :c] * (HEAD_DIM ** -0.5)).astype(BF16)
    u_ref[:, c:] = u[:, c:].astype(BF16)


def _sb_proj(h, w_bf, c):
    bsz, s, d = h.shape
    tm = _tile(s, TOKEN_TILE)
    n_out = w_bf.shape[1]
    return pl.pallas_call(
        functools.partial(_sb_proj_kernel, c=c),
        grid=(bsz, s // tm),
        in_specs=[
            pl.BlockSpec((None, tm, d), lambda b, i: (b, i, 0)),
            pl.BlockSpec((d, n_out), lambda b, i: (0, 0)),
        ],
        out_specs=pl.BlockSpec((None, tm, n_out), lambda b, i: (b, i, 0)),
        out_shape=jax.ShapeDtypeStruct((bsz, s, n_out), BF16),
        compiler_params=pltpu.CompilerParams(
            dimension_semantics=("arbitrary", "arbitrary"), vmem_limit_bytes=VMEM_LIMIT),
        name="sb_proj",
    )(h, w_bf)


def _sb_attn_kernel(q_ref, k_ref, v_ref, o_ref, carry_ref, acc_ref, *, tq, tk, n_pairs):
    qi = pl.program_id(2)
    n_heads = 2 * n_pairs
    lane = lax.broadcasted_iota(I32, (tq, LANES), 1)
    k_lane = lax.broadcasted_iota(I32, (tk, LANES), 1)
    q_heads = []
    for pr in range(n_pairs):
        q2 = q_ref[:, pr * LANES:(pr + 1) * LANES]
        q_heads += [jnp.where(lane < HEAD_DIM, q2, jnp.zeros_like(q2)),
                    jnp.where(lane >= HEAD_DIM, q2, jnp.zeros_like(q2))]
    later = (lax.broadcasted_iota(I32, (tk, tk), 0) > lax.broadcasted_iota(I32, (tk, tk), 1)).astype(BF16)

    def block(kb, masked, r0, r1):
        n = r1 - r0
        start = pl.multiple_of(kb * tk, tk)
        if masked:
            causal = ((kb * tk + lax.broadcasted_iota(I32, (n, tk), 1))
                      < (qi * tq + r0 + lax.broadcasted_iota(I32, (n, tk), 0)))
        log_nots, log_betas, parts = [], [], []
        for pr in range(n_pairs):
            k2 = k_ref[pl.ds(start, tk), pr * LANES:(pr + 1) * LANES]
            z2 = lax.dot_general(jnp.concatenate([q[r0:r1] for q in q_heads[2 * pr:2 * pr + 2]], axis=0), k2,
                                 (((1,), (1,)), ((), ())), preferred_element_type=F32)
            for half in range(2):
                z = z2[half * n:(half + 1) * n] * LOG2_E
                sp = jnp.maximum(z, 0.0) + jnp.log2(1.0 + jnp.exp2(-jnp.abs(z)))
                log_not = -sp
                if masked:
                    log_not = jnp.where(causal, log_not, 0.0)
                hi = log_not.astype(BF16)
                parts += [hi, (log_not - hi.astype(F32)).astype(BF16)]
                log_nots.append(log_not)
                log_betas.append(z - sp)
        sums = jnp.dot(jnp.concatenate(parts, axis=0), later, preferred_element_type=F32)
        for pr in range(n_pairs):
            v2 = v_ref[pl.ds(start, tk), pr * LANES:(pr + 1) * LANES]
            ws = []
            for half in range(2):
                hd = 2 * pr + half
                carry = carry_ref[hd, r0:r1, :]
                tail = sums[2 * hd * n:(2 * hd + 1) * n] + sums[(2 * hd + 1) * n:(2 * hd + 2) * n] + carry
                w = jnp.exp2(log_betas[hd] + tail)
                if masked:
                    w = jnp.where(causal, w, 0.0)
                ws.append(w.astype(BF16))
                carry_ref[hd, r0:r1, :] = carry + jnp.sum(log_nots[hd], axis=1, keepdims=True)
            v_rows = jnp.concatenate([jnp.where(k_lane < HEAD_DIM, v2, jnp.zeros_like(v2)),
                                      jnp.where(k_lane >= HEAD_DIM, v2, jnp.zeros_like(v2))], axis=0)
            acc_ref[pr, r0:r1, :] += jnp.dot(jnp.concatenate(ws, axis=1), v_rows, preferred_element_type=F32)

    carry_ref[...] = jnp.zeros_like(carry_ref)
    acc_ref[...] = jnp.zeros_like(acc_ref)
    n_diag = tq // tk
    for j in range(n_diag - 1, -1, -1):
        block(qi * n_diag + j, True, j * tk, tq)

    row_count = 1 + lax.broadcasted_iota(I32, (tq, tk), 0)

    def live_rows():
        top = carry_ref[0]
        for hd in range(1, n_heads):
            top = jnp.maximum(top, carry_ref[hd])
        return jnp.max(jnp.where(top > SB_SKIP_BELOW * LOG2_E, row_count, 0))

    def cond(state):
        kb, n_live = state
        return jnp.logical_and(kb >= 0, n_live > 0)

    def body(state):
        kb, n_live = state

        bounds = [0] + [r for r in SB_ROW_PREFIXES if r < tq] + [tq]
        for lo, hi in zip(bounds[:-1], bounds[1:]):
            @pl.when(jnp.logical_and(n_live > lo, n_live <= hi))
            def _():
                block(kb, False, 0, hi)

        return kb - 1, live_rows()

    lax.while_loop(cond, body, (qi * n_diag - 1, live_rows()))
    o_ref[...] = jnp.concatenate([acc_ref[pr] for pr in range(n_pairs)], axis=1).astype(BF16)


def _sb_attn(u, c):
    bsz, s, _ = u.shape
    tq = _tile(s, SB_QUERY_ROWS)
    tk = _tile(tq, SB_KEY_ROWS)
    width = SB_PAIRS_PER_STEP * LANES
    n_groups = c // width
    return pl.pallas_call(
        functools.partial(_sb_attn_kernel, tq=tq, tk=tk, n_pairs=SB_PAIRS_PER_STEP),
        grid=(bsz, n_groups, s // tq),
        in_specs=[
            pl.BlockSpec((None, tq, width), lambda b, p, i: (b, i, p)),
            pl.BlockSpec((None, s, width), lambda b, p, i: (b, 0, n_groups + p), pipeline_mode=pl.Buffered(1)),
            pl.BlockSpec((None, s, width), lambda b, p, i: (b, 0, 2 * n_groups + p), pipeline_mode=pl.Buffered(1)),
        ],
        out_specs=pl.BlockSpec((None, tq, width), lambda b, p, i: (b, i, p)),
        out_shape=jax.ShapeDtypeStruct((bsz, s, c), BF16),
        scratch_shapes=[pltpu.VMEM((2 * SB_PAIRS_PER_STEP, tq, tk), F32),
                        pltpu.VMEM((SB_PAIRS_PER_STEP, tq, LANES), F32)],
        compiler_params=pltpu.CompilerParams(
            dimension_semantics=("arbitrary", "arbitrary", "arbitrary"), vmem_limit_bytes=VMEM_LIMIT),
        name="sb_attn",
    )(u, u, u)


def _mix_out_kernel(h_ref, y_ref, qm_ref, kx_ref, vx_ref, wo_ref, g_ref, b_ref, wr_ref, br_ref,
                    h1_ref, idx_ref, gate_ref, rank_ref, cnt_ref, *, tm, c, n_mem, alpha):
    def rows(i, carry):
        sl = pl.ds(pl.multiple_of(i * MIX_ROWS, MIX_ROWS), MIX_ROWS)
        s = jnp.dot(qm_ref[sl, :], kx_ref[...], preferred_element_type=F32)
        probs = []
        for hd in range(MEM_HEADS):
            sh = s[:, hd * n_mem:(hd + 1) * n_mem]
            e = jnp.exp(sh - jnp.max(sh, axis=-1, keepdims=True))
            probs.append((e / jnp.sum(e, axis=-1, keepdims=True)).astype(BF16))
        y_mem = jnp.dot(jnp.concatenate(probs, axis=-1), vx_ref[...], preferred_element_type=F32)
        y = (jnp.dot(y_ref[sl, :], wo_ref[0:c, :], preferred_element_type=F32)
             + jnp.dot(y_mem.astype(BF16), wo_ref[c:, :], preferred_element_type=F32))
        h1_ref[sl, :] = _layer_norm(alpha * h_ref[sl, :] + y, g_ref[...], b_ref[...])
        return carry

    lax.fori_loop(0, tm // MIX_ROWS, rows, 0)
    h1 = h1_ref[...]

    logits = lax.dot_general(wr_ref[...], h1.astype(BF16), (((1,), (1,)), ((), ())),
                             preferred_element_type=F32) + br_ref[...]
    e_iota = lax.broadcasted_iota(I32, (N_EXPERTS, tm), 0)
    vals, sels = [], []
    for _ in range(TOP_K):
        m = jnp.max(logits, axis=0, keepdims=True)
        sel = jnp.min(jnp.where(logits == m, e_iota, N_EXPERTS), axis=0, keepdims=True)
        vals.append(m)
        sels.append(sel)
        logits = jnp.where(e_iota == sel, -jnp.inf, logits)
    exps = [jnp.exp(v - vals[0]) for v in vals]
    denom = exps[0] + exps[1] + exps[2] + exps[3]
    gate_ref[...] = jnp.concatenate([e / denom for e in exps], axis=0)
    idx_ref[...] = jnp.concatenate(sels, axis=0)

    onehot = jnp.zeros((N_EXPERTS, tm), F32)
    for sel in sels:
        onehot = onehot + (e_iota == sel).astype(F32)
    earlier = (lax.broadcasted_iota(I32, (tm, tm), 0) < lax.broadcasted_iota(I32, (tm, tm), 1)).astype(BF16)
    rank_e = jnp.dot(onehot.astype(BF16), earlier, preferred_element_type=F32)
    rank_ref[...] = jnp.concatenate(
        [jnp.sum(jnp.where(e_iota == sel, rank_e, 0.0), axis=0, keepdims=True) for sel in sels],
        axis=0).astype(I32)
    cnt_ref[...] = jnp.sum(onehot, axis=1, keepdims=True).astype(I32)


def _mix_out(h, y_mix, qm_src, qm_block, kx, vx, wo_bf, ln_g, ln_b, wr_t_bf, b_router, alpha):
    bsz, s, d = h.shape
    c = y_mix.shape[-1]
    n_mem = vx.shape[1] // MEM_HEADS
    tm = _tile(s, TOKEN_TILE)
    n_s = s // tm
    n_tok = bsz * s
    tok_spec = pl.BlockSpec((TOP_K, tm), lambda b, i: (0, b * n_s + i))
    return pl.pallas_call(
        functools.partial(_mix_out_kernel, tm=tm, c=c, n_mem=n_mem, alpha=alpha),
        grid=(bsz, n_s),
        in_specs=[
            pl.BlockSpec((None, tm, d), lambda b, i: (b, i, 0)),
            pl.BlockSpec((None, tm, c), lambda b, i: (b, i, 0)),
            pl.BlockSpec((None, tm, MEM_WIDTH), lambda b, i: (b, i, qm_block)),
            pl.BlockSpec((None, MEM_WIDTH, MEM_HEADS * n_mem), lambda b, i: (b, 0, 0)),
            pl.BlockSpec((None, MEM_HEADS * n_mem, MEM_WIDTH), lambda b, i: (b, 0, 0)),
            pl.BlockSpec((c + MEM_WIDTH, d), lambda b, i: (0, 0)),
            pl.BlockSpec((1, d), lambda b, i: (0, 0)),
            pl.BlockSpec((1, d), lambda b, i: (0, 0)),
            pl.BlockSpec((N_EXPERTS, d), lambda b, i: (0, 0)),
            pl.BlockSpec((N_EXPERTS, 1), lambda b, i: (0, 0)),
        ],
        out_specs=[
            pl.BlockSpec((None, tm, d), lambda b, i: (b, i, 0)),
            tok_spec, tok_spec, tok_spec,
            pl.BlockSpec((None, N_EXPERTS, 1), lambda b, i: (b * n_s + i, 0, 0)),
        ],
        out_shape=[
            jax.ShapeDtypeStruct((bsz, s, d), F32),
            jax.ShapeDtypeStruct((TOP_K, n_tok), I32),
            jax.ShapeDtypeStruct((TOP_K, n_tok), F32),
            jax.ShapeDtypeStruct((TOP_K, n_tok), I32),
            jax.ShapeDtypeStruct((bsz * n_s, N_EXPERTS, 1), I32),
        ],
        compiler_params=pltpu.CompilerParams(
            dimension_semantics=("arbitrary", "arbitrary"), vmem_limit_bytes=VMEM_LIMIT),
        name="mix_out",
    )(h, y_mix, qm_src, kx, vx, wo_bf, ln_g.reshape(1, d), ln_b.reshape(1, d), wr_t_bf,
      b_router.reshape(N_EXPERTS, 1))


def _copy_rows(src_ref, src_row, dst_ref, dst_row, count, sem, max_bit, wait, long_runs_rare=False):
    def copy(off, size):
        cp = pltpu.make_async_copy(src_ref.at[pl.ds(pl.multiple_of(src_row + off, RUN_ALIGN), size)],
                                   dst_ref.at[pl.ds(pl.multiple_of(dst_row + off, RUN_ALIGN), size)], sem)
        if wait:
            cp.wait()
        else:
            cp.start()

    low_bit = RUN_ALIGN.bit_length() - 1
    if long_runs_rare and max_bit > COMMON_RUN_BIT:
        max_bit = COMMON_RUN_BIT
        chunk = 1 << (COMMON_RUN_BIT + 1)

        def chunks(i, carry):
            copy(i * chunk, chunk)
            return carry

        lax.fori_loop(0, lax.shift_right_logical(count, COMMON_RUN_BIT + 1), chunks, 0)
    for bit in range(max_bit, low_bit - 1, -1):
        off = lax.shift_left(lax.shift_right_logical(count, bit + 1), bit + 1)
        pl.when(jnp.bitwise_and(lax.shift_right_logical(count, bit), 1) == 1)(
            functools.partial(copy, off, 1 << bit))


def _tile_runs(tile, tc_ref, ls_ref, base_ref, local_ref, global_ref, sem, max_bit, to_global, wait):
    if wait:
        last = tile * N_EXPERTS + N_EXPERTS - 1
        total = ls_ref[last] + tc_ref[last]
        total_bit = local_ref.shape[0].bit_length() - 1
        if to_global:
            _copy_rows(local_ref, 0, global_ref, 0, total, sem, total_bit, True)
        else:
            _copy_rows(global_ref, 0, local_ref, 0, total, sem, total_bit, True)
        return

    def per_expert(e, carry):
        k = tile * N_EXPERTS + e
        if to_global:
            _copy_rows(local_ref, ls_ref[k], global_ref, base_ref[k], tc_ref[k], sem, max_bit, wait, True)
        else:
            _copy_rows(global_ref, base_ref[k], local_ref, ls_ref[k], tc_ref[k], sem, max_bit, wait, True)
        return carry

    lax.fori_loop(0, N_EXPERTS, per_expert, 0)


def _dispatch_kernel(tc_ref, ls_ref, base_ref, fill_lo_ref, fill_hi_ref, na_ref,
                     pos_ref, h_ref, xs_ref, buf, zero_ref, sems, fill_sem, *, tm, tmb):
    t = pl.program_id(0)
    n_t = pl.num_programs(0)
    slot = lax.rem(t, 2)
    max_bit = tm.bit_length() - 1

    def runs(tile, slot_, wait):
        _tile_runs(tile, tc_ref, ls_ref, base_ref, buf.at[slot_], xs_ref, sems.at[slot_], max_bit, True, wait)

    @pl.when(t >= 2)
    def _():
        runs(t - 2, slot, True)

    x = h_ref[...].astype(BF16)
    pos = pos_ref[...]

    def sort_rows(i, carry):
        r0 = pl.multiple_of(i * SORT_ROWS, SORT_ROWS)
        row = r0 + lax.broadcasted_iota(I32, (SORT_ROWS, tm), 0)
        perm = jnp.zeros((SORT_ROWS, tm), F32)
        for k in range(TOP_K):
            perm = jnp.where(row == pos[k:k + 1, :], 1.0, perm)
        rows = jnp.dot(perm.astype(BF16), x, preferred_element_type=F32)
        buf[slot, pl.ds(r0, SORT_ROWS), :] = _pack_halves(rows)
        return carry

    lax.fori_loop(0, _sorted_rows(tm) // SORT_ROWS, sort_rows, 0)
    runs(t, slot, False)

    def zero_fill(wait):
        pad_bit = tmb.bit_length() - 2

        def per_expert(e, carry):
            _copy_rows(zero_ref, 0, xs_ref, fill_lo_ref[e], fill_hi_ref[e] - fill_lo_ref[e],
                       fill_sem, pad_bit, wait)
            return carry

        def per_block(blk, carry):
            cp = pltpu.make_async_copy(zero_ref, xs_ref.at[pl.ds(blk * tmb, tmb)], fill_sem)
            if wait:
                cp.wait()
            else:
                cp.start()
            return carry

        lax.fori_loop(0, N_EXPERTS, per_expert, 0)
        lax.fori_loop(na_ref[0], xs_ref.shape[0] // tmb, per_block, 0)

    @pl.when(t == 0)
    def _():
        zero_ref[...] = jnp.zeros_like(zero_ref)
        zero_fill(False)

    @pl.when(t == n_t - 1)
    def _():
        @pl.when(t >= 1)
        def _():
            runs(t - 1, 1 - slot, True)
        runs(t, slot, True)
        zero_fill(True)


def _dispatch(h1_flat, pos, tables, fill_lo, fill_hi, n_active, n_rows, tm, tmb):
    n_tok, d = h1_flat.shape
    n_t = n_tok // tm
    tc, ls, base = tables
    return pl.pallas_call(
        functools.partial(_dispatch_kernel, tm=tm, tmb=tmb),
        grid_spec=pltpu.PrefetchScalarGridSpec(
            num_scalar_prefetch=6,
            grid=(n_t,),
            in_specs=[
                pl.BlockSpec((TOP_K, tm), lambda i, *_: (0, i)),
                pl.BlockSpec((tm, d), lambda i, *_: (i, 0)),
            ],
            out_specs=pl.BlockSpec(memory_space=pl.ANY),
            scratch_shapes=[
                pltpu.VMEM((2, _sorted_rows(tm), d // 2), I32),
                pltpu.VMEM((tmb, d // 2), I32),
                pltpu.SemaphoreType.DMA((2,)),
                pltpu.SemaphoreType.DMA,
            ],
        ),
        out_shape=jax.ShapeDtypeStruct((n_rows, d // 2), I32),
        compiler_params=pltpu.CompilerParams(
            dimension_semantics=("arbitrary",), vmem_limit_bytes=VMEM_LIMIT),
        name="moe_dispatch",
    )(tc, ls, base, fill_lo, fill_hi, n_active, pos, h1_flat)


def _expert_kernel(ord_ref, seq_ref, nseq_ref, na_ref, valid_ref, xs_ref, wgu_hbm, bgu_ref, wd_hbm, bd_ref, ys_ref,
                   wgu_f32, wd_f32, wgu_bf, wd_bf, sems, *, f):
    i = pl.program_id(0)
    n_active = na_ref[0]
    j = jnp.minimum(i, n_active - 1)
    prev = jnp.maximum(j - 1, 0)
    o = ord_ref[j]
    new_expert = jnp.logical_or(i == 0, o != ord_ref[prev])

    def fetch(o_, slot, wait):
        e = seq_ref[o_]
        copies = []
        for src, dst, parts in ((wgu_hbm, wgu_f32, WEIGHT_DMA_PARTS), (wd_hbm, wd_f32, WEIGHT_DMA_PARTS // 2)):
            rows = src.shape[1] // parts
            for p in range(parts):
                copies.append(pltpu.make_async_copy(src.at[e, pl.ds(p * rows, rows)],
                                                    dst.at[slot, pl.ds(p * rows, rows)], sems.at[slot]))
        for cp in copies:
            if wait:
                cp.wait()
            else:
                cp.start()

    @pl.when(i == 0)
    def _():
        fetch(0, 0, False)

    @pl.when(jnp.logical_and(i < n_active, new_expert))
    def _():
        slot = lax.rem(o, 2)
        fetch(o, slot, True)

        @pl.when(o + 1 < nseq_ref[0])
        def _():
            fetch(o + 1, 1 - slot, False)

        rows = 128
        for r in range(0, wgu_bf.shape[0], rows):
            wgu_bf[r:r + rows, :] = wgu_f32[slot, r:r + rows, :].astype(BF16)
        for r in range(0, wd_bf.shape[0], rows):
            wd_bf[r:r + rows, :] = wd_f32[slot, r:r + rows, :].astype(BF16)

    valid = valid_ref[i]

    def rows_pass(r, n):
        x = _unpack_halves(xs_ref[r:r + n, :])
        gu = jnp.dot(x, wgu_bf[...], preferred_element_type=F32) + bgu_ref[...]
        gate = jnp.minimum(gu[:, :f], SWIGLU_LIMIT)
        up = jnp.clip(gu[:, f:], -SWIGLU_LIMIT, SWIGLU_LIMIT)
        act = (up + 1.0) * gate * _sigmoid(SWIGLU_ALPHA * gate)
        y = jnp.dot(act.astype(BF16), wd_bf[...], preferred_element_type=F32) + bd_ref[...]
        ys_ref[r:r + n, :] = _pack_halves(y.astype(BF16).astype(F32))

    def zero_rows(r, n):
        ys_ref[r:r + n, :] = jnp.zeros((n, ys_ref.shape[1]), I32)

    n_rows = xs_ref.shape[0]

    @pl.when(valid > n_rows - EXPERT_SUB)
    def _():
        rows_pass(0, n_rows)

    @pl.when(valid <= n_rows - EXPERT_SUB)
    def _():
        for r in range(0, n_rows, 2 * EXPERT_SUB):
            @pl.when(valid > r + EXPERT_SUB)
            def _():
                rows_pass(r, 2 * EXPERT_SUB)

            @pl.when(jnp.logical_and(valid > r, valid <= r + EXPERT_SUB))
            def _():
                rows_pass(r, EXPERT_SUB)
                zero_rows(r + EXPERT_SUB, EXPERT_SUB)

            @pl.when(valid <= r)
            def _():
                zero_rows(r, 2 * EXPERT_SUB)


def _experts(xs, block_ord, expert_seq, n_seq, n_active, block_valid, w_gate_up, b_gate_up, w_down, b_down, tmb):
    n_rows, half = xs.shape
    n_e, d, f2 = w_gate_up.shape
    f = f2 // 2
    nb = n_rows // tmb

    def blk(i, refs):
        return jnp.minimum(i, refs[3][0] - 1)

    def expert(i, refs):
        return refs[1][refs[0][blk(i, refs)]]

    return pl.pallas_call(
        functools.partial(_expert_kernel, f=f),
        grid_spec=pltpu.PrefetchScalarGridSpec(
            num_scalar_prefetch=5,
            grid=(nb,),
            in_specs=[
                pl.BlockSpec((tmb, half), lambda i, *refs: (blk(i, refs), 0)),
                pl.BlockSpec(memory_space=pl.ANY),
                pl.BlockSpec((None, 1, f2), lambda i, *refs: (expert(i, refs), 0, 0)),
                pl.BlockSpec(memory_space=pl.ANY),
                pl.BlockSpec((None, 1, d), lambda i, *refs: (expert(i, refs), 0, 0)),
            ],
            out_specs=pl.BlockSpec((tmb, half), lambda i, *refs: (i, 0)),
            scratch_shapes=[
                pltpu.VMEM((2, d, f2), F32), pltpu.VMEM((2, f, d), F32),
                pltpu.VMEM((d, f2), BF16), pltpu.VMEM((f, d), BF16),
                pltpu.SemaphoreType.DMA((2,)),
            ],
        ),
        out_shape=jax.ShapeDtypeStruct((n_rows, half), I32),
        compiler_params=pltpu.CompilerParams(
            dimension_semantics=("arbitrary",), vmem_limit_bytes=EXPERT_VMEM_LIMIT),
        name="moe_experts",
    )(block_ord, expert_seq, n_seq, n_active, block_valid, xs, w_gate_up, b_gate_up.reshape(n_e, 1, f2), w_down,
      b_down.reshape(n_e, 1, d))


def _combine_kernel(tc_ref, ls_ref, base_ref, pos_ref, gate_ref, h_ref, ys_ref, g_ref, b_ref, o_ref,
                    buf, sems, *, tm, alpha):
    t = pl.program_id(0)
    n_t = pl.num_programs(0)
    slot = lax.rem(t, 2)
    max_bit = tm.bit_length() - 1

    def runs(tile, slot_, wait):
        _tile_runs(tile, tc_ref, ls_ref, base_ref, buf.at[slot_], ys_ref, sems.at[slot_], max_bit, False, wait)

    @pl.when(t == 0)
    def _():
        buf[...] = jnp.zeros_like(buf)
        runs(t, slot, False)

    @pl.when(t + 1 < n_t)
    def _():
        runs(t + 1, 1 - slot, False)

    runs(t, slot, True)

    pos = pos_ref[...]
    gates = gate_ref[...]
    col = lax.broadcasted_iota(I32, (tm, _sorted_rows(tm)), 1)
    weights = jnp.zeros((tm, _sorted_rows(tm)), F32)
    for k in range(TOP_K):
        weights = jnp.where(col == pos[:, k:k + 1], gates[:, k:k + 1], weights)
    y = jnp.dot(weights.astype(BF16), _unpack_halves(buf[slot]), preferred_element_type=F32)
    o_ref[...] = _layer_norm(alpha * h_ref[...] + y, g_ref[...], b_ref[...])


def _combine(h1_flat, ys, pos_t, gates_t, tables, ln_g, ln_b, alpha, tm):
    n_tok, d = h1_flat.shape
    n_t = n_tok // tm
    tc, ls, base = tables
    return pl.pallas_call(
        functools.partial(_combine_kernel, tm=tm, alpha=alpha),
        grid_spec=pltpu.PrefetchScalarGridSpec(
            num_scalar_prefetch=3,
            grid=(n_t,),
            in_specs=[
                pl.BlockSpec((tm, TOP_K), lambda i, *_: (i, 0)),
                pl.BlockSpec((tm, TOP_K), lambda i, *_: (i, 0)),
                pl.BlockSpec((tm, d), lambda i, *_: (i, 0)),
                pl.BlockSpec(memory_space=pl.ANY),
                pl.BlockSpec((1, d), lambda i, *_: (0, 0)),
                pl.BlockSpec((1, d), lambda i, *_: (0, 0)),
            ],
            out_specs=pl.BlockSpec((tm, d), lambda i, *_: (i, 0)),
            scratch_shapes=[pltpu.VMEM((2, _sorted_rows(tm), d // 2), I32), pltpu.SemaphoreType.DMA((2,))],
        ),
        out_shape=jax.ShapeDtypeStruct((n_tok, d), F32),
        compiler_params=pltpu.CompilerParams(
            dimension_semantics=("arbitrary",), vmem_limit_bytes=VMEM_LIMIT),
        name="moe_combine",
    )(tc, ls, base, pos_t, gates_t, h1_flat, ys, ln_g.reshape(1, d), ln_b.reshape(1, d))


def _moe(h1, idx, gates, rank, tile_counts, layer, w_gate_up, b_gate_up, w_down, b_down, ln_g, ln_b, alpha):
    bsz, s, d = h1.shape
    n_tok = bsz * s
    tm = _tile(s, TOKEN_TILE)
    n_t = n_tok // tm
    tmb = EXPERT_ROWS
    n_rows = n_tok * TOP_K + n_t * N_EXPERTS * (RUN_ALIGN - 1) + N_EXPERTS * tmb
    n_rows = (n_rows + tmb - 1) // tmb * tmb
    nb = n_rows // tmb

    tile_counts = (tile_counts + RUN_ALIGN - 1) // RUN_ALIGN * RUN_ALIGN
    counts = jnp.sum(tile_counts, axis=0)
    padded = (counts + tmb - 1) // tmb * tmb
    pad_end = jnp.cumsum(padded).astype(I32)
    pad_start = pad_end - padded
    base = pad_start[None, :] + jnp.cumsum(tile_counts, axis=0) - tile_counts
    local = jnp.cumsum(tile_counts, axis=1) - tile_counts
    experts = jnp.arange(N_EXPERTS, dtype=I32)
    local_tok = jnp.repeat(local, tm, axis=0)
    pos = rank + jnp.sum(jnp.where(idx[:, :, None] == experts, local_tok[None], 0), axis=-1).astype(I32)
    block_start = jnp.arange(nb, dtype=I32) * tmb
    block_expert = jnp.minimum(
        jnp.sum((block_start[:, None] >= pad_end[None, :]).astype(I32), axis=1), N_EXPERTS - 1).astype(I32)
    n_active = (pad_end[-1:] // tmb).astype(I32)
    tables = (tile_counts.reshape(-1).astype(I32), local.reshape(-1).astype(I32), base.reshape(-1).astype(I32))

    h1_flat = h1.reshape(n_tok, d)
    xs = _dispatch(h1_flat, pos, tables, pad_start + counts, pad_end, n_active, n_rows, tm, tmb)
    merge = lambda w: w.reshape((w.shape[0] * w.shape[1],) + w.shape[2:])
    present = padded > 0
    expert_ord = jnp.cumsum(present.astype(I32)) - 1
    expert_seq = (jnp.argsort(jnp.logical_not(present), stable=True) + layer * N_EXPERTS).astype(I32)
    n_seq = jnp.sum(present.astype(I32)).reshape(1)
    block_valid = jnp.clip((pad_start + counts)[block_expert] - block_start, 0, tmb).astype(I32)
    ys = _experts(xs, expert_ord[block_expert].astype(I32), expert_seq, n_seq, n_active, block_valid,
                  merge(w_gate_up), merge(b_gate_up), merge(w_down), merge(b_down), tmb)
    out = _combine(h1_flat, ys, pos.T, gates.T, tables, ln_g, ln_b, alpha, tm)
    return out.reshape(bsz, s, d)


@jax.jit
def _forward(x, mem, mem_ln_g, mem_ln_b, w_mem_kv, w_in_conv, conv_dw_w, conv_dw_b, conv_ln_g,
             conv_ln_b, w_in_sb, w_mix_out, ln_mix_g, ln_mix_b, w_router, b_router, w_gate_up,
             b_gate_up, w_down, b_down, ln_moe_g, ln_moe_b):
    depth = w_mix_out.shape[0]
    alpha = float((2 * depth) ** 0.25)
    c = conv_dw_w.shape[-1]
    kx, vx = _mem_kv(mem, mem_ln_g, mem_ln_b, w_mem_kv.astype(BF16))

    h = x
    for i in range(depth):
        j = i // 2
        if i % 2 == 0:
            y_mix, qm = _conv_mix(h, w_in_conv[j].astype(BF16), conv_dw_w[j], conv_dw_b[j],
                                  conv_ln_g[j], conv_ln_b[j])
            qm_src, qm_block = qm, 0
        else:
            u = _sb_proj(h, w_in_sb[j].astype(BF16), c)
            y_mix = _sb_attn(u, c)
            qm_src, qm_block = u, (3 * c) // MEM_WIDTH
        h1, idx, gates, rank, cnt = _mix_out(
            h, y_mix, qm_src, qm_block, kx, vx, w_mix_out[i].astype(BF16), ln_mix_g[i], ln_mix_b[i],
            w_router[i].T.astype(BF16), b_router[i], alpha)
        h = _moe(h1, idx, gates, rank, cnt[:, :, 0], i, w_gate_up, b_gate_up, w_down, b_down,
                 ln_moe_g[i], ln_moe_b[i], alpha)
    return h


def kernel(x, mem, mem_ln_g, mem_ln_b, w_mem_kv, w_in_conv, conv_dw_w, conv_dw_b, conv_ln_g, conv_ln_b,
           w_in_sb, w_mix_out, ln_mix_g, ln_mix_b, w_router, b_router, w_gate_up, b_gate_up, w_down,
           b_down, ln_moe_g, ln_moe_b):
    return _forward(x, mem, mem_ln_g, mem_ln_b, w_mem_kv, w_in_conv, conv_dw_w, conv_dw_b, conv_ln_g,
                    conv_ln_b, w_in_sb, w_mix_out, ln_mix_g, ln_mix_b, w_router, b_router, w_gate_up,
                    b_gate_up, w_down, b_down, ln_moe_g, ln_moe_b)
```

```python
import functools

import jax
import jax.numpy as jnp
from jax import lax
from jax.experimental import pallas as pl
from jax.experimental.pallas import tpu as pltpu

F32 = jnp.float32
BF16 = jnp.bfloat16
I32 = jnp.int32

HEAD_DIM = 64
MEM_HEADS = 4
MEM_WIDTH = MEM_HEADS * HEAD_DIM
CONV_WIDTH = 31
N_EXPERTS = 32
TOP_K = 4
SWIGLU_LIMIT = 7.0
SWIGLU_ALPHA = 1.702
LN_EPS = 1e-5

LANES = 128
CONV_HALO = 32
CONV_STRIDE = 4
CONV_ROWS = 8 * CONV_STRIDE
SB_SKIP_BELOW = -104.0
LOG2_E = 1.4426950408889634
SB_PAIRS_PER_STEP = 6
SB_QUERY_ROWS = 256
SB_KEY_ROWS = 128
SB_ROW_PREFIXES = (32, 64, 192)
TOKEN_TILE = 512
MIX_ROWS = 256
EXPERT_ROWS = 1024
EXPERT_SUB = 256
SORT_ROWS = 256
RUN_ALIGN = 8
COMMON_RUN_BIT = 6


def _sorted_rows(tm):
    rows = TOP_K * tm + N_EXPERTS * RUN_ALIGN
    return (rows + SORT_ROWS - 1) // SORT_ROWS * SORT_ROWS
WEIGHT_DMA_PARTS = 4
VMEM_LIMIT = 48 * 1024 * 1024
EXPERT_VMEM_LIMIT = 60 * 1024 * 1024
HIGH_HALF = -65536


def _layer_norm(x, g, b):
    mu = jnp.mean(x, axis=-1, keepdims=True)
    xc = x - mu
    var = jnp.mean(xc * xc, axis=-1, keepdims=True)
    return xc * lax.rsqrt(var + LN_EPS) * g + b


def _sigmoid(x):
    return 1.0 / (1.0 + jnp.exp(-x))


def _tile(n, want):
    t = min(n, want)
    assert n % t == 0, (n, t)
    return t


def _pack_halves(x):
    n = x.shape[1] // 2
    lo = lax.bitcast_convert_type(x[:, :n], I32)
    hi = lax.bitcast_convert_type(x[:, n:], I32)
    return jnp.bitwise_or(jnp.bitwise_and(hi, HIGH_HALF), lax.shift_right_logical(lo, 16))


def _unpack_halves(p):
    lo = lax.bitcast_convert_type(lax.shift_left(p, 16), F32)
    hi = lax.bitcast_convert_type(jnp.bitwise_and(p, HIGH_HALF), F32)
    return jnp.concatenate([lo, hi], axis=1).astype(BF16)


def _mem_kv_kernel(mem_ref, g_ref, b_ref, w_ref, kx_ref, vx_ref):
    n_mem = mem_ref.shape[0]
    xn = _layer_norm(mem_ref[...], g_ref[...], b_ref[...])
    kv = jnp.dot(xn.astype(BF16), w_ref[...], preferred_element_type=F32)
    k_t = (kv[:, :MEM_WIDTH] * (HEAD_DIM ** -0.5)).T
    v = kv[:, MEM_WIDTH:]
    k_head = lax.broadcasted_iota(I32, (MEM_WIDTH, n_mem), 0) // HEAD_DIM
    v_head = lax.broadcasted_iota(I32, (n_mem, MEM_WIDTH), 1) // HEAD_DIM
    for hd in range(MEM_HEADS):
        kx_ref[:, hd * n_mem:(hd + 1) * n_mem] = jnp.where(k_head == hd, k_t, 0.0).astype(BF16)
        vx_ref[hd * n_mem:(hd + 1) * n_mem, :] = jnp.where(v_head == hd, v, 0.0).astype(BF16)


def _mem_kv(mem, g, b, w_bf):
    bsz, n_mem, d = mem.shape
    return pl.pallas_call(
        _mem_kv_kernel,
        grid=(bsz,),
        in_specs=[
            pl.BlockSpec((None, n_mem, d), lambda i: (i, 0, 0)),
            pl.BlockSpec((1, d), lambda i: (0, 0)),
            pl.BlockSpec((1, d), lambda i: (0, 0)),
            pl.BlockSpec((d, 2 * MEM_WIDTH), lambda i: (0, 0)),
        ],
        out_specs=[
            pl.BlockSpec((None, MEM_WIDTH, MEM_HEADS * n_mem), lambda i: (i, 0, 0)),
            pl.BlockSpec((None, MEM_HEADS * n_mem, MEM_WIDTH), lambda i: (i, 0, 0)),
        ],
        out_shape=[
            jax.ShapeDtypeStruct((bsz, MEM_WIDTH, MEM_HEADS * n_mem), BF16),
            jax.ShapeDtypeStruct((bsz, MEM_HEADS * n_mem, MEM_WIDTH), BF16),
        ],
        name="mem_kv",
    )(mem, g.reshape(1, d), b.reshape(1, d), w_bf)


def _conv_mix_kernel(h_ref, w_ref, dww_ref, dwb_ref, g_ref, b_ref, y_ref, qm_ref, gbuf, cbuf, *, tm, c):
    n_groups = c // LANES

    @pl.when(pl.program_id(1) == 0)
    def _():
        gbuf[:, 0:CONV_HALO, :] = jnp.zeros((n_groups, CONV_HALO, LANES), F32)

    @pl.when(pl.program_id(1) > 0)
    def _():
        gbuf[:, 0:CONV_HALO, :] = gbuf[:, tm:tm + CONV_HALO, :]

    u = jnp.dot(h_ref[...].astype(BF16), w_ref[...], preferred_element_type=F32)
    glu = u[:, :c] * _sigmoid(u[:, c:2 * c])
    for grp in range(n_groups):
        gbuf[grp, CONV_HALO:, :] = glu[:, grp * LANES:(grp + 1) * LANES]
    qm_ref[...] = u[:, 2 * c:].astype(BF16)

    first_tap = CONV_HALO - (CONV_WIDTH - 1)

    def rows(i, carry):
        r0 = pl.multiple_of(i * CONV_ROWS, CONV_ROWS)
        for grp in range(n_groups):
            src = gbuf.at[grp, pl.ds(r0, CONV_ROWS + CONV_HALO), :]
            dst = cbuf.at[grp, pl.ds(r0, CONV_ROWS), :]
            lanes = slice(grp * LANES, (grp + 1) * LANES)
            taps = [jnp.broadcast_to(dww_ref[j:j + 1, lanes], (8, LANES)) for j in range(CONV_WIDTH)]
            bias = jnp.broadcast_to(dwb_ref[:, lanes], (8, LANES))
            for r in range(CONV_STRIDE):
                acc = bias
                for j in range(CONV_WIDTH):
                    acc = acc + src[pl.ds(first_tap + j + r, 8, stride=CONV_STRIDE), :] * taps[j]
                dst[pl.ds(r, 8, stride=CONV_STRIDE), :] = acc
        conv = jnp.concatenate([cbuf[grp, pl.ds(r0, CONV_ROWS), :] for grp in range(n_groups)], axis=-1)
        yn = _layer_norm(conv, g_ref[...], b_ref[...])
        y_ref[pl.ds(r0, CONV_ROWS), :] = (yn * _sigmoid(yn)).astype(BF16)
        return carry

    lax.fori_loop(0, tm // CONV_ROWS, rows, 0)


def _conv_mix(h, w_bf, dw_w, dw_b, ln_g, ln_b):
    bsz, s, d = h.shape
    c = dw_w.shape[1]
    tm = _tile(s, TOKEN_TILE)
    n_out = w_bf.shape[1]
    row = lambda a: a.reshape(1, -1)
    return pl.pallas_call(
        functools.partial(_conv_mix_kernel, tm=tm, c=c),
        grid=(bsz, s // tm),
        in_specs=[
            pl.BlockSpec((None, tm, d), lambda b, i: (b, i, 0)),
            pl.BlockSpec((d, n_out), lambda b, i: (0, 0)),
            pl.BlockSpec((CONV_WIDTH, c), lambda b, i: (0, 0)),
            pl.BlockSpec((1, c), lambda b, i: (0, 0)),
            pl.BlockSpec((1, c), lambda b, i: (0, 0)),
            pl.BlockSpec((1, c), lambda b, i: (0, 0)),
        ],
        out_specs=[
            pl.BlockSpec((None, tm, c), lambda b, i: (b, i, 0)),
            pl.BlockSpec((None, tm, MEM_WIDTH), lambda b, i: (b, i, 0)),
        ],
        out_shape=[
            jax.ShapeDtypeStruct((bsz, s, c), BF16),
            jax.ShapeDtypeStruct((bsz, s, MEM_WIDTH), BF16),
        ],
        scratch_shapes=[pltpu.VMEM((c // LANES, tm + CONV_HALO, LANES), F32),
                        pltpu.VMEM((c // LANES, tm, LANES), F32)],
        compiler_params=pltpu.CompilerParams(
            dimension_semantics=("arbitrary", "arbitrary"), vmem_limit_bytes=VMEM_LIMIT),
        name="conv_mix",
    )(h, w_bf, dw_w, row(dw_b), row(ln_g), row(ln_b))


def _sb_proj_kernel(h_ref, w_ref, u_ref, *, c):
    u = jnp.dot(h_ref[...].astype(BF16), w_ref[...], preferred_element_type=F32)
    u_ref[:, :c] = (u[:, :c] * (HEAD_DIM ** -0.5)).astype(BF16)
    u_ref[:, c:] = u[:, c:].astype(BF16)


def _sb_proj(h, w_bf, c):
    bsz, s, d = h.shape
    tm = _tile(s, TOKEN_TILE)
    n_out = w_bf.shape[1]
    return pl.pallas_call(
        functools.partial(_sb_proj_kernel, c=c),
        grid=(bsz, s // tm),
        in_specs=[
            pl.BlockSpec((None, tm, d), lambda b, i: (b, i, 0)),
            pl.BlockSpec((d, n_out), lambda b, i: (0, 0)),
        ],
        out_specs=pl.BlockSpec((None, tm, n_out), lambda b, i: (b, i, 0)),
        out_shape=jax.ShapeDtypeStruct((bsz, s, n_out), BF16),
        compiler_params=pltpu.CompilerParams(
            dimension_semantics=("arbitrary", "arbitrary"), vmem_limit_bytes=VMEM_LIMIT),
        name="sb_proj",
    )(h, w_bf)


def _sb_attn_kernel(q_ref, k_ref, v_ref, o_ref, carry_ref, acc_ref, *, tq, tk, n_pairs):
    qi = pl.program_id(2)
    n_heads = 2 * n_pairs
    lane = lax.broadcasted_iota(I32, (tq, LANES), 1)
    k_lane = lax.broadcasted_iota(I32, (tk, LANES), 1)
    q_heads = []
    for pr in range(n_pairs):
        q2 = q_ref[:, pr * LANES:(pr + 1) * LANES]
        q_heads += [jnp.where(lane < HEAD_DIM, q2, jnp.zeros_like(q2)),
                    jnp.where(lane >= HEAD_DIM, q2, jnp.zeros_like(q2))]
    later = (lax.broadcasted_iota(I32, (tk, tk), 0) > lax.broadcasted_iota(I32, (tk, tk), 1)).astype(BF16)

    def block(kb, masked, r0, r1):
        n = r1 - r0
        start = pl.multiple_of(kb * tk, tk)
        if masked:
            causal = ((kb * tk + lax.broadcasted_iota(I32, (n, tk), 1))
                      < (qi * tq + r0 + lax.broadcasted_iota(I32, (n, tk), 0)))
        log_nots, log_betas, parts = [], [], []
        for pr in range(n_pairs):
            k2 = k_ref[pl.ds(start, tk), pr * LANES:(pr + 1) * LANES]
            z2 = lax.dot_general(jnp.concatenate([q[r0:r1] for q in q_heads[2 * pr:2 * pr + 2]], axis=0), k2,
                                 (((1,), (1,)), ((), ())), preferred_element_type=F32)
            for half in range(2):
                z = z2[half * n:(half + 1) * n] * LOG2_E
                sp = jnp.maximum(z, 0.0) + jnp.log2(1.0 + jnp.exp2(-jnp.abs(z)))
                log_not = -sp
                if masked:
                    log_not = jnp.where(causal, log_not, 0.0)
                hi = log_not.astype(BF16)
                parts += [hi, (log_not - hi.astype(F32)).astype(BF16)]
                log_nots.append(log_not)
                log_betas.append(z - sp)
        sums = jnp.dot(jnp.concatenate(parts, axis=0), later, preferred_element_type=F32)
        for pr in range(n_pairs):
            v2 = v_ref[pl.ds(start, tk), pr * LANES:(pr + 1) * LANES]
            ws = []
            for half in range(2):
                hd = 2 * pr + half
                carry = carry_ref[hd, r0:r1, :]
                tail = sums[2 * hd * n:(2 * hd + 1) * n] + sums[(2 * hd + 1) * n:(2 * hd + 2) * n] + carry
                w = jnp.exp2(log_betas[hd] + tail)
                if masked:
                    w = jnp.where(causal, w, 0.0)
                ws.append(w.astype(BF16))
                carry_ref[hd, r0:r1, :] = carry + jnp.sum(log_nots[hd], axis=1, keepdims=True)
            v_rows = jnp.concatenate([jnp.where(k_lane < HEAD_DIM, v2, jnp.zeros_like(v2)),
                                      jnp.where(k_lane >= HEAD_DIM, v2, jnp.zeros_like(v2))], axis=0)
            acc_ref[pr, r0:r1, :] += jnp.dot(jnp.concatenate(ws, axis=1), v_rows, preferred_element_type=F32)

    carry_ref[...] = jnp.zeros_like(carry_ref)
    acc_ref[...] = jnp.zeros_like(acc_ref)
    n_diag = tq // tk
    for j in range(n_diag - 1, -1, -1):
        block(qi * n_diag + j, True, j * tk, tq)

    row_count = 1 + lax.broadcasted_iota(I32, (tq, tk), 0)

    def live_rows():
        top = carry_ref[0]
        for hd in range(1, n_heads):
            top = jnp.maximum(top, carry_ref[hd])
        return jnp.max(jnp.where(top > SB_SKIP_BELOW * LOG2_E, row_count, 0))

    def cond(state):
        kb, n_live = state
        return jnp.logical_and(kb >= 0, n_live > 0)

    def body(state):
        kb, n_live = state

        bounds = [0] + [r for r in SB_ROW_PREFIXES if r < tq] + [tq]
        for lo, hi in zip(bounds[:-1], bounds[1:]):
            @pl.when(jnp.logical_and(n_live > lo, n_live <= hi))
            def _():
                block(kb, False, 0, hi)

        return kb - 1, live_rows()

    lax.while_loop(cond, body, (qi * n_diag - 1, live_rows()))
    o_ref[...] = jnp.concatenate([acc_ref[pr] for pr in range(n_pairs)], axis=1).astype(BF16)


def _sb_attn(u, c):
    bsz, s, _ = u.shape
    tq = _tile(s, SB_QUERY_ROWS)
    tk = _tile(tq, SB_KEY_ROWS)
    width = SB_PAIRS_PER_STEP * LANES
    n_groups = c // width
    return pl.pallas_call(
        functools.partial(_sb_attn_kernel, tq=tq, tk=tk, n_pairs=SB_PAIRS_PER_STEP),
        grid=(bsz, n_groups, s // tq),
        in_specs=[
            pl.BlockSpec((None, tq, width), lambda b, p, i: (b, i, p)),
            pl.BlockSpec((None, s, width), lambda b, p, i: (b, 0, n_groups + p), pipeline_mode=pl.Buffered(1)),
            pl.BlockSpec((None, s, width), lambda b, p, i: (b, 0, 2 * n_groups + p), pipeline_mode=pl.Buffered(1)),
        ],
        out_specs=pl.BlockSpec((None, tq, width), lambda b, p, i: (b, i, p)),
        out_shape=jax.ShapeDtypeStruct((bsz, s, c), BF16),
        scratch_shapes=[pltpu.VMEM((2 * SB_PAIRS_PER_STEP, tq, tk), F32),
                        pltpu.VMEM((SB_PAIRS_PER_STEP, tq, LANES), F32)],
        compiler_params=pltpu.CompilerParams(
            dimension_semantics=("arbitrary", "arbitrary", "arbitrary"), vmem_limit_bytes=VMEM_LIMIT),
        name="sb_attn",
    )(u, u, u)


def _mix_out_kernel(h_ref, y_ref, qm_ref, kx_ref, vx_ref, wo_ref, g_ref, b_ref, wr_ref, br_ref,
                    h1_ref, idx_ref, gate_ref, rank_ref, cnt_ref, *, tm, c, n_mem, alpha):
    def rows(i, carry):
        sl = pl.ds(pl.multiple_of(i * MIX_ROWS, MIX_ROWS), MIX_ROWS)
        s = jnp.dot(qm_ref[sl, :], kx_ref[...], preferred_element_type=F32)
        probs = []
        for hd in range(MEM_HEADS):
            sh = s[:, hd * n_mem:(hd + 1) * n_mem]
            e = jnp.exp(sh - jnp.max(sh, axis=-1, keepdims=True))
            probs.append((e / jnp.sum(e, axis=-1, keepdims=True)).astype(BF16))
        y_mem = jnp.dot(jnp.concatenate(probs, axis=-1), vx_ref[...], preferred_element_type=F32)
        y = (jnp.dot(y_ref[sl, :], wo_ref[0:c, :], preferred_element_type=F32)
             + jnp.dot(y_mem.astype(BF16), wo_ref[c:, :], preferred_element_type=F32))
        h1_ref[sl, :] = _layer_norm(alpha * h_ref[sl, :] + y, g_ref[...], b_ref[...])
        return carry

    lax.fori_loop(0, tm // MIX_ROWS, rows, 0)
    h1 = h1_ref[...]

    logits = lax.dot_general(wr_ref[...], h1.astype(BF16), (((1,), (1,)), ((), ())),
                             preferred_element_type=F32) + br_ref[...]
    e_iota = lax.broadcasted_iota(I32, (N_EXPERTS, tm), 0)
    vals, sels = [], []
    for _ in range(TOP_K):
        m = jnp.max(logits, axis=0, keepdims=True)
        sel = jnp.min(jnp.where(logits == m, e_iota, N_EXPERTS), axis=0, keepdims=True)
        vals.append(m)
        sels.append(sel)
        logits = jnp.where(e_iota == sel, -jnp.inf, logits)
    exps = [jnp.exp(v - vals[0]) for v in vals]
    denom = exps[0] + exps[1] + exps[2] + exps[3]
    gate_ref[...] = jnp.concatenate([e / denom for e in exps], axis=0)
    idx_ref[...] = jnp.concatenate(sels, axis=0)

    onehot = jnp.zeros((N_EXPERTS, tm), F32)
    for sel in sels:
        onehot = onehot + (e_iota == sel).astype(F32)
    earlier = (lax.broadcasted_iota(I32, (tm, tm), 0) < lax.broadcasted_iota(I32, (tm, tm), 1)).astype(BF16)
    rank_e = jnp.dot(onehot.astype(BF16), earlier, preferred_element_type=F32)
    rank_ref[...] = jnp.concatenate(
        [jnp.sum(jnp.where(e_iota == sel, rank_e, 0.0), axis=0, keepdims=True) for sel in sels],
        axis=0).astype(I32)
    cnt_ref[...] = jnp.sum(onehot, axis=1, keepdims=True).astype(I32)


def _mix_out(h, y_mix, qm_src, qm_block, kx, vx, wo_bf, ln_g, ln_b, wr_t_bf, b_router, alpha):
    bsz, s, d = h.shape
    c = y_mix.shape[-1]
    n_mem = vx.shape[1] // MEM_HEADS
    tm = _tile(s, TOKEN_TILE)
    n_s = s // tm
    n_tok = bsz * s
    tok_spec = pl.BlockSpec((TOP_K, tm), lambda b, i: (0, b * n_s + i))
    return pl.pallas_call(
        functools.partial(_mix_out_kernel, tm=tm, c=c, n_mem=n_mem, alpha=alpha),
        grid=(bsz, n_s),
        in_specs=[
            pl.BlockSpec((None, tm, d), lambda b, i: (b, i, 0)),
            pl.BlockSpec((None, tm, c), lambda b, i: (b, i, 0)),
            pl.BlockSpec((None, tm, MEM_WIDTH), lambda b, i: (b, i, qm_block)),
            pl.BlockSpec((None, MEM_WIDTH, MEM_HEADS * n_mem), lambda b, i: (b, 0, 0)),
            pl.BlockSpec((None, MEM_HEADS * n_mem, MEM_WIDTH), lambda b, i: (b, 0, 0)),
            pl.BlockSpec((c + MEM_WIDTH, d), lambda b, i: (0, 0)),
            pl.BlockSpec((1, d), lambda b, i: (0, 0)),
            pl.BlockSpec((1, d), lambda b, i: (0, 0)),
            pl.BlockSpec((N_EXPERTS, d), lambda b, i: (0, 0)),
            pl.BlockSpec((N_EXPERTS, 1), lambda b, i: (0, 0)),
        ],
        out_specs=[
            pl.BlockSpec((None, tm, d), lambda b, i: (b, i, 0)),
            tok_spec, tok_spec, tok_spec,
            pl.BlockSpec((None, N_EXPERTS, 1), lambda b, i: (b * n_s + i, 0, 0)),
        ],
        out_shape=[
            jax.ShapeDtypeStruct((bsz, s, d), F32),
            jax.ShapeDtypeStruct((TOP_K, n_tok), I32),
            jax.ShapeDtypeStruct((TOP_K, n_tok), F32),
            jax.ShapeDtypeStruct((TOP_K, n_tok), I32),
            jax.ShapeDtypeStruct((bsz * n_s, N_EXPERTS, 1), I32),
        ],
        compiler_params=pltpu.CompilerParams(
            dimension_semantics=("arbitrary", "arbitrary"), vmem_limit_bytes=VMEM_LIMIT),
        name="mix_out",
    )(h, y_mix, qm_src, kx, vx, wo_bf, ln_g.reshape(1, d), ln_b.reshape(1, d), wr_t_bf,
      b_router.reshape(N_EXPERTS, 1))


def _copy_rows(src_ref, src_row, dst_ref, dst_row, count, sem, max_bit, wait, long_runs_rare=False):
    def copy(off, size):
        cp = pltpu.make_async_copy(src_ref.at[pl.ds(pl.multiple_of(src_row + off, RUN_ALIGN), size)],
                                   dst_ref.at[pl.ds(pl.multiple_of(dst_row + off, RUN_ALIGN), size)], sem)
        if wait:
            cp.wait()
        else:
            cp.start()

    low_bit = RUN_ALIGN.bit_length() - 1
    if long_runs_rare and max_bit > COMMON_RUN_BIT:
        max_bit = COMMON_RUN_BIT
        chunk = 1 << (COMMON_RUN_BIT + 1)

        def chunks(i, carry):
            copy(i * chunk, chunk)
            return carry

        lax.fori_loop(0, lax.shift_right_logical(count, COMMON_RUN_BIT + 1), chunks, 0)
    for bit in range(max_bit, low_bit - 1, -1):
        off = lax.shift_left(lax.shift_right_logical(count, bit + 1), bit + 1)
        pl.when(jnp.bitwise_and(lax.shift_right_logical(count, bit), 1) == 1)(
            functools.partial(copy, off, 1 << bit))


def _tile_runs(tile, tc_ref, ls_ref, base_ref, local_ref, global_ref, sem, max_bit, to_global, wait):
    if wait:
        last = tile * N_EXPERTS + N_EXPERTS - 1
        total = ls_ref[last] + tc_ref[last]
        total_bit = local_ref.shape[0].bit_length() - 1
        if to_global:
            _copy_rows(local_ref, 0, global_ref, 0, total, sem, total_bit, True)
        else:
            _copy_rows(global_ref, 0, local_ref, 0, total, sem, total_bit, True)
        return

    def per_expert(e, carry):
        k = tile * N_EXPERTS + e
        if to_global:
            _copy_rows(local_ref, ls_ref[k], global_ref, base_ref[k], tc_ref[k], sem, max_bit, wait, True)
        else:
            _copy_rows(global_ref, base_ref[k], local_ref, ls_ref[k], tc_ref[k], sem, max_bit, wait, True)
        return carry

    lax.fori_loop(0, N_EXPERTS, per_expert, 0)


def _dispatch_kernel(tc_ref, ls_ref, base_ref, fill_lo_ref, fill_hi_ref, na_ref,
                     pos_ref, h_ref, xs_ref, buf, zero_ref, sems, fill_sem, *, tm, tmb):
    t = pl.program_id(0)
    n_t = pl.num_programs(0)
    slot = lax.rem(t, 2)
    max_bit = tm.bit_length() - 1

    def runs(tile, slot_, wait):
        _tile_runs(tile, tc_ref, ls_ref, base_ref, buf.at[slot_], xs_ref, sems.at[slot_], max_bit, True, wait)

    @pl.when(t >= 2)
    def _():
        runs(t - 2, slot, True)

    x = h_ref[...].astype(BF16)
    pos = pos_ref[...]

    pos16 = pos.astype(jnp.int16)
    row16 = lax.broadcasted_iota(I32, (SORT_ROWS, tm), 0).astype(jnp.int16)
    one = jnp.ones((SORT_ROWS, tm), BF16)

    def sort_rows(i, carry):
        r0 = pl.multiple_of(i * SORT_ROWS, SORT_ROWS)
        row = row16 + r0.astype(jnp.int16)
        perm = jnp.zeros((SORT_ROWS, tm), BF16)
        for k in range(TOP_K):
            perm = jnp.where(row == pos16[k:k + 1, :], one, perm)
        rows = jnp.dot(perm, x, preferred_element_type=F32)
        buf[slot, pl.ds(r0, SORT_ROWS), :] = _pack_halves(rows)
        return carry

    lax.fori_loop(0, _sorted_rows(tm) // SORT_ROWS, sort_rows, 0)
    runs(t, slot, False)

    def zero_fill(wait):
        pad_bit = tmb.bit_length() - 2

        def per_expert(e, carry):
            _copy_rows(zero_ref, 0, xs_ref, fill_lo_ref[e], fill_hi_ref[e] - fill_lo_ref[e],
                       fill_sem, pad_bit, wait)
            return carry

        def per_block(blk, carry):
            cp = pltpu.make_async_copy(zero_ref, xs_ref.at[pl.ds(blk * tmb, tmb)], fill_sem)
            if wait:
                cp.wait()
            else:
                cp.start()
            return carry

        lax.fori_loop(0, N_EXPERTS, per_expert, 0)
        lax.fori_loop(na_ref[0], xs_ref.shape[0] // tmb, per_block, 0)

    @pl.when(t == 0)
    def _():
        zero_ref[...] = jnp.zeros_like(zero_ref)
        zero_fill(False)

    @pl.when(t == n_t - 1)
    def _():
        @pl.when(t >= 1)
        def _():
            runs(t - 1, 1 - slot, True)
        runs(t, slot, True)
        zero_fill(True)


def _dispatch(h1_flat, pos, tables, fill_lo, fill_hi, n_active, n_rows, tm, tmb):
    n_tok, d = h1_flat.shape
    n_t = n_tok // tm
    tc, ls, base = tables
    return pl.pallas_call(
        functools.partial(_dispatch_kernel, tm=tm, tmb=tmb),
        grid_spec=pltpu.PrefetchScalarGridSpec(
            num_scalar_prefetch=6,
            grid=(n_t,),
            in_specs=[
                pl.BlockSpec((TOP_K, tm), lambda i, *_: (0, i)),
                pl.BlockSpec((tm, d), lambda i, *_: (i, 0)),
            ],
            out_specs=pl.BlockSpec(memory_space=pl.ANY),
            scratch_shapes=[
                pltpu.VMEM((2, _sorted_rows(tm), d // 2), I32),
                pltpu.VMEM((tmb, d // 2), I32),
                pltpu.SemaphoreType.DMA((2,)),
                pltpu.SemaphoreType.DMA,
            ],
        ),
        out_shape=jax.ShapeDtypeStruct((n_rows, d // 2), I32),
        compiler_params=pltpu.CompilerParams(
            dimension_semantics=("arbitrary",), vmem_limit_bytes=VMEM_LIMIT),
        name="moe_dispatch",
    )(tc, ls, base, fill_lo, fill_hi, n_active, pos, h1_flat)


def _expert_kernel(ord_ref, seq_ref, nseq_ref, na_ref, valid_ref, xs_ref, wgu_hbm, bgu_ref, wd_hbm, bd_ref, ys_ref,
                   wgu_f32, wd_f32, wgu_bf, wd_bf, sems, *, f):
    i = pl.program_id(0)
    n_active = na_ref[0]
    j = jnp.minimum(i, n_active - 1)
    prev = jnp.maximum(j - 1, 0)
    o = ord_ref[j]
    new_expert = jnp.logical_or(i == 0, o != ord_ref[prev])

    def fetch(o_, slot, wait):
        e = seq_ref[o_]
        copies = []
        for src, dst, parts in ((wgu_hbm, wgu_f32, WEIGHT_DMA_PARTS), (wd_hbm, wd_f32, WEIGHT_DMA_PARTS // 2)):
            rows = src.shape[1] // parts
            for p in range(parts):
                copies.append(pltpu.make_async_copy(src.at[e, pl.ds(p * rows, rows)],
                                                    dst.at[slot, pl.ds(p * rows, rows)], sems.at[slot]))
        for cp in copies:
            if wait:
                cp.wait()
            else:
                cp.start()

    @pl.when(i == 0)
    def _():
        fetch(0, 0, False)

    @pl.when(jnp.logical_and(i < n_active, new_expert))
    def _():
        slot = lax.rem(o, 2)
        fetch(o, slot, True)

        @pl.when(o + 1 < nseq_ref[0])
        def _():
            fetch(o + 1, 1 - slot, False)

        rows = 128
        for r in range(0, wgu_bf.shape[0], rows):
            wgu_bf[r:r + rows, :] = wgu_f32[slot, r:r + rows, :].astype(BF16)
        for r in range(0, wd_bf.shape[0], rows):
            wd_bf[r:r + rows, :] = wd_f32[slot, r:r + rows, :].astype(BF16)

    valid = valid_ref[i]

    def rows_pass(r, n):
        x = _unpack_halves(xs_ref[r:r + n, :])
        gu = jnp.dot(x, wgu_bf[...], preferred_element_type=F32) + bgu_ref[...]
        gate = jnp.minimum(gu[:, :f], SWIGLU_LIMIT)
        up = jnp.clip(gu[:, f:], -SWIGLU_LIMIT, SWIGLU_LIMIT)
        act = (up + 1.0) * gate * _sigmoid(SWIGLU_ALPHA * gate)
        y = jnp.dot(act.astype(BF16), wd_bf[...], preferred_element_type=F32) + bd_ref[...]
        ys_ref[r:r + n, :] = _pack_halves(y.astype(BF16).astype(F32))

    def zero_rows(r, n):
        ys_ref[r:r + n, :] = jnp.zeros((n, ys_ref.shape[1]), I32)

    n_rows = xs_ref.shape[0]

    @pl.when(valid > n_rows - EXPERT_SUB)
    def _():
        rows_pass(0, n_rows)

    @pl.when(valid <= n_rows - EXPERT_SUB)
    def _():
        for r in range(0, n_rows, 2 * EXPERT_SUB):
            @pl.when(valid > r + EXPERT_SUB)
            def _():
                rows_pass(r, 2 * EXPERT_SUB)

            @pl.when(jnp.logical_and(valid > r, valid <= r + EXPERT_SUB))
            def _():
                rows_pass(r, EXPERT_SUB)
                zero_rows(r + EXPERT_SUB, EXPERT_SUB)

            @pl.when(valid <= r)
            def _():
                zero_rows(r, 2 * EXPERT_SUB)


def _experts(xs, block_ord, expert_seq, n_seq, n_active, block_valid, w_gate_up, b_gate_up, w_down, b_down, tmb):
    n_rows, half = xs.shape
    n_e, d, f2 = w_gate_up.shape
    f = f2 // 2
    nb = n_rows // tmb

    def blk(i, refs):
        return jnp.minimum(i, refs[3][0] - 1)

    def expert(i, refs):
        return refs[1][refs[0][blk(i, refs)]]

    return pl.pallas_call(
        functools.partial(_expert_kernel, f=f),
        grid_spec=pltpu.PrefetchScalarGridSpec(
            num_scalar_prefetch=5,
            grid=(nb,),
            in_specs=[
                pl.BlockSpec((tmb, half), lambda i, *refs: (blk(i, refs), 0)),
                pl.BlockSpec(memory_space=pl.ANY),
                pl.BlockSpec((None, 1, f2), lambda i, *refs: (expert(i, refs), 0, 0)),
                pl.BlockSpec(memory_space=pl.ANY),
                pl.BlockSpec((None, 1, d), lambda i, *refs: (expert(i, refs), 0, 0)),
            ],
            out_specs=pl.BlockSpec((tmb, half), lambda i, *refs: (i, 0)),
            scratch_shapes=[
                pltpu.VMEM((2, d, f2), F32), pltpu.VMEM((2, f, d), F32),
                pltpu.VMEM((d, f2), BF16), pltpu.VMEM((f, d), BF16),
                pltpu.SemaphoreType.DMA((2,)),
            ],
        ),
        out_shape=jax.ShapeDtypeStruct((n_rows, half), I32),
        compiler_params=pltpu.CompilerParams(
            dimension_semantics=("arbitrary",), vmem_limit_bytes=EXPERT_VMEM_LIMIT),
        name="moe_experts",
    )(block_ord, expert_seq, n_seq, n_active, block_valid, xs, w_gate_up, b_gate_up.reshape(n_e, 1, f2), w_down,
      b_down.reshape(n_e, 1, d))


def _combine_kernel(tc_ref, ls_ref, base_ref, pos_ref, gate_ref, h_ref, ys_ref, g_ref, b_ref, o_ref,
                    buf, sems, *, tm, alpha):
    t = pl.program_id(0)
    n_t = pl.num_programs(0)
    slot = lax.rem(t, 2)
    max_bit = tm.bit_length() - 1

    def runs(tile, slot_, wait):
        _tile_runs(tile, tc_ref, ls_ref, base_ref, buf.at[slot_], ys_ref, sems.at[slot_], max_bit, False, wait)

    @pl.when(t == 0)
    def _():
        buf[...] = jnp.zeros_like(buf)
        runs(t, slot, False)

    @pl.when(t + 1 < n_t)
    def _():
        runs(t + 1, 1 - slot, False)

    runs(t, slot, True)

    pos = pos_ref[...]
    gates = gate_ref[...]
    col = lax.broadcasted_iota(I32, (tm, _sorted_rows(tm)), 1)
    weights = jnp.zeros((tm, _sorted_rows(tm)), F32)
    for k in range(TOP_K):
        weights = jnp.where(col == pos[:, k:k + 1], gates[:, k:k + 1], weights)
    y = jnp.dot(weights.astype(BF16), _unpack_halves(buf[slot]), preferred_element_type=F32)
    o_ref[...] = _layer_norm(alpha * h_ref[...] + y, g_ref[...], b_ref[...])


def _combine(h1_flat, ys, pos_t, gates_t, tables, ln_g, ln_b, alpha, tm):
    n_tok, d = h1_flat.shape
    n_t = n_tok // tm
    tc, ls, base = tables
    return pl.pallas_call(
        functools.partial(_combine_kernel, tm=tm, alpha=alpha),
        grid_spec=pltpu.PrefetchScalarGridSpec(
            num_scalar_prefetch=3,
            grid=(n_t,),
            in_specs=[
                pl.BlockSpec((tm, TOP_K), lambda i, *_: (i, 0)),
                pl.BlockSpec((tm, TOP_K), lambda i, *_: (i, 0)),
                pl.BlockSpec((tm, d), lambda i, *_: (i, 0)),
                pl.BlockSpec(memory_space=pl.ANY),
                pl.BlockSpec((1, d), lambda i, *_: (0, 0)),
                pl.BlockSpec((1, d), lambda i, *_: (0, 0)),
            ],
            out_specs=pl.BlockSpec((tm, d), lambda i, *_: (i, 0)),
            scratch_shapes=[pltpu.VMEM((2, _sorted_rows(tm), d // 2), I32), pltpu.SemaphoreType.DMA((2,))],
        ),
        out_shape=jax.ShapeDtypeStruct((n_tok, d), F32),
        compiler_params=pltpu.CompilerParams(
            dimension_semantics=("arbitrary",), vmem_limit_bytes=VMEM_LIMIT),
        name="moe_combine",
    )(tc, ls, base, pos_t, gates_t, h1_flat, ys, ln_g.reshape(1, d), ln_b.reshape(1, d))


def _moe(h1, idx, gates, rank, tile_counts, layer, w_gate_up, b_gate_up, w_down, b_down, ln_g, ln_b, alpha):
    bsz, s, d = h1.shape
    n_tok = bsz * s
    tm = _tile(s, TOKEN_TILE)
    n_t = n_tok // tm
    tmb = EXPERT_ROWS
    n_rows = n_tok * TOP_K + n_t * N_EXPERTS * (RUN_ALIGN - 1) + N_EXPERTS * tmb
    n_rows = (n_rows + tmb - 1) // tmb * tmb
    nb = n_rows // tmb

    tile_counts = (tile_counts + RUN_ALIGN - 1) // RUN_ALIGN * RUN_ALIGN
    counts = jnp.sum(tile_counts, axis=0)
    padded = (counts + tmb - 1) // tmb * tmb
    pad_end = jnp.cumsum(padded).astype(I32)
    pad_start = pad_end - padded
    base = pad_start[None, :] + jnp.cumsum(tile_counts, axis=0) - tile_counts
    local = jnp.cumsum(tile_counts, axis=1) - tile_counts
    experts = jnp.arange(N_EXPERTS, dtype=I32)
    local_tok = jnp.repeat(local, tm, axis=0)
    pos = rank + jnp.sum(jnp.where(idx[:, :, None] == experts, local_tok[None], 0), axis=-1).astype(I32)
    block_start = jnp.arange(nb, dtype=I32) * tmb
    block_expert = jnp.minimum(
        jnp.sum((block_start[:, None] >= pad_end[None, :]).astype(I32), axis=1), N_EXPERTS - 1).astype(I32)
    n_active = (pad_end[-1:] // tmb).astype(I32)
    tables = (tile_counts.reshape(-1).astype(I32), local.reshape(-1).astype(I32), base.reshape(-1).astype(I32))

    h1_flat = h1.reshape(n_tok, d)
    xs = _dispatch(h1_flat, pos, tables, pad_start + counts, pad_end, n_active, n_rows, tm, tmb)
    merge = lambda w: w.reshape((w.shape[0] * w.shape[1],) + w.shape[2:])
    present = padded > 0
    expert_ord = jnp.cumsum(present.astype(I32)) - 1
    expert_seq = (jnp.argsort(jnp.logical_not(present), stable=True) + layer * N_EXPERTS).astype(I32)
    n_seq = jnp.sum(present.astype(I32)).reshape(1)
    block_valid = jnp.clip((pad_start + counts)[block_expert] - block_start, 0, tmb).astype(I32)
    ys = _experts(xs, expert_ord[block_expert].astype(I32), expert_seq, n_seq, n_active, block_valid,
                  merge(w_gate_up), merge(b_gate_up), merge(w_down), merge(b_down), tmb)
    out = _combine(h1_flat, ys, pos.T, gates.T, tables, ln_g, ln_b, alpha, tm)
    return out.reshape(bsz, s, d)


@jax.jit
def _forward(x, mem, mem_ln_g, mem_ln_b, w_mem_kv, w_in_conv, conv_dw_w, conv_dw_b, conv_ln_g,
             conv_ln_b, w_in_sb, w_mix_out, ln_mix_g, ln_mix_b, w_router, b_router, w_gate_up,
             b_gate_up, w_down, b_down, ln_moe_g, ln_moe_b):
    depth = w_mix_out.shape[0]
    alpha = float((2 * depth) ** 0.25)
    c = conv_dw_w.shape[-1]
    kx, vx = _mem_kv(mem, mem_ln_g, mem_ln_b, w_mem_kv.astype(BF16))

    h = x
    for i in range(depth):
        j = i // 2
        if i % 2 == 0:
            y_mix, qm = _conv_mix(h, w_in_conv[j].astype(BF16), conv_dw_w[j], conv_dw_b[j],
                                  conv_ln_g[j], conv_ln_b[j])
            qm_src, qm_block = qm, 0
        else:
            u = _sb_proj(h, w_in_sb[j].astype(BF16), c)
            y_mix = _sb_attn(u, c)
            qm_src, qm_block = u, (3 * c) // MEM_WIDTH
        h1, idx, gates, rank, cnt = _mix_out(
            h, y_mix, qm_src, qm_block, kx, vx, w_mix_out[i].astype(BF16), ln_mix_g[i], ln_mix_b[i],
            w_router[i].T.astype(BF16), b_router[i], alpha)
        h = _moe(h1, idx, gates, rank, cnt[:, :, 0], i, w_gate_up, b_gate_up, w_down, b_down,
                 ln_moe_g[i], ln_moe_b[i], alpha)
    return h


def kernel(x, mem, mem_ln_g, mem_ln_b, w_mem_kv, w_in_conv, conv_dw_w, conv_dw_b, conv_ln_g, conv_ln_b,
           w_in_sb, w_mix_out, ln_mix_g, ln_mix_b, w_router, b_router, w_gate_up, b_gate_up, w_down,
           b_down, ln_moe_g, ln_moe_b):
    return _forward(x, mem, mem_ln_g, mem_ln_b, w_mem_kv, w_in_conv, conv_dw_w, conv_dw_b, conv_ln_g,
                    conv_ln_b, w_in_sb, w_mix_out, ln_mix_g, ln_mix_b, w_router, b_router, w_gate_up,
                    b_gate_up, w_down, b_down, ln_moe_g, ln_moe_b)
```

```python
import functools

import jax
import jax.numpy as jnp
from jax import lax
from jax.experimental import pallas as pl
from jax.experimental.pallas import tpu as pltpu

F32 = jnp.float32
BF16 = jnp.bfloat16
I32 = jnp.int32

HEAD_DIM = 64
MEM_HEADS = 4
MEM_WIDTH = MEM_HEADS * HEAD_DIM
CONV_WIDTH = 31
N_EXPERTS = 32
TOP_K = 4
SWIGLU_LIMIT = 7.0
SWIGLU_ALPHA = 1.702
LN_EPS = 1e-5

LANES = 128
CONV_HALO = 32
CONV_STRIDE = 4
CONV_ROWS = 8 * CONV_STRIDE
SB_SKIP_BELOW = -104.0
LOG2_E = 1.4426950408889634
SB_PAIRS_PER_STEP = 6
SB_QUERY_ROWS = 256
SB_KEY_ROWS = 128
SB_ROW_PREFIXES = (32, 64, 192)
TOKEN_TILE = 512
MIX_ROWS = 256
EXPERT_ROWS = 1024
EXPERT_SUB = 256
SORT_ROWS = 256
RUN_ALIGN = 8
COMMON_RUN_BIT = 6


def _sorted_rows(tm):
    rows = TOP_K * tm + N_EXPERTS * RUN_ALIGN
    return (rows + SORT_ROWS - 1) // SORT_ROWS * SORT_ROWS
WEIGHT_DMA_PARTS = 4
VMEM_LIMIT = 48 * 1024 * 1024
EXPERT_VMEM_LIMIT = 60 * 1024 * 1024
HIGH_HALF = -65536


def _layer_norm(x, g, b):
    mu = jnp.mean(x, axis=-1, keepdims=True)
    xc = x - mu
    var = jnp.mean(xc * xc, axis=-1, keepdims=True)
    return xc * lax.rsqrt(var + LN_EPS) * g + b


def _sigmoid(x):
    return 1.0 / (1.0 + jnp.exp(-x))


def _tile(n, want):
    t = min(n, want)
    assert n % t == 0, (n, t)
    return t


def _pack_halves(x):
    n = x.shape[1] // 2
    lo = lax.bitcast_convert_type(x[:, :n], I32)
    hi = lax.bitcast_convert_type(x[:, n:], I32)
    return jnp.bitwise_or(jnp.bitwise_and(hi, HIGH_HALF), lax.shift_right_logical(lo, 16))


def _unpack_halves(p):
    lo = lax.bitcast_convert_type(lax.shift_left(p, 16), F32)
    hi = lax.bitcast_convert_type(jnp.bitwise_and(p, HIGH_HALF), F32)
    return jnp.concatenate([lo, hi], axis=1).astype(BF16)


def _mem_kv_kernel(mem_ref, g_ref, b_ref, w_ref, kx_ref, vx_ref):
    n_mem = mem_ref.shape[0]
    xn = _layer_norm(mem_ref[...], g_ref[...], b_ref[...])
    kv = jnp.dot(xn.astype(BF16), w_ref[...], preferred_element_type=F32)
    k_t = (kv[:, :MEM_WIDTH] * (HEAD_DIM ** -0.5)).T
    v = kv[:, MEM_WIDTH:]
    k_head = lax.broadcasted_iota(I32, (MEM_WIDTH, n_mem), 0) // HEAD_DIM
    v_head = lax.broadcasted_iota(I32, (n_mem, MEM_WIDTH), 1) // HEAD_DIM
    for hd in range(MEM_HEADS):
        kx_ref[:, hd * n_mem:(hd + 1) * n_mem] = jnp.where(k_head == hd, k_t, 0.0).astype(BF16)
        vx_ref[hd * n_mem:(hd + 1) * n_mem, :] = jnp.where(v_head == hd, v, 0.0).astype(BF16)


def _mem_kv(mem, g, b, w_bf):
    bsz, n_mem, d = mem.shape
    return pl.pallas_call(
        _mem_kv_kernel,
        grid=(bsz,),
        in_specs=[
            pl.BlockSpec((None, n_mem, d), lambda i: (i, 0, 0)),
            pl.BlockSpec((1, d), lambda i: (0, 0)),
            pl.BlockSpec((1, d), lambda i: (0, 0)),
            pl.BlockSpec((d, 2 * MEM_WIDTH), lambda i: (0, 0)),
        ],
        out_specs=[
            pl.BlockSpec((None, MEM_WIDTH, MEM_HEADS * n_mem), lambda i: (i, 0, 0)),
            pl.BlockSpec((None, MEM_HEADS * n_mem, MEM_WIDTH), lambda i: (i, 0, 0)),
        ],
        out_shape=[
            jax.ShapeDtypeStruct((bsz, MEM_WIDTH, MEM_HEADS * n_mem), BF16),
            jax.ShapeDtypeStruct((bsz, MEM_HEADS * n_mem, MEM_WIDTH), BF16),
        ],
        name="mem_kv",
    )(mem, g.reshape(1, d), b.reshape(1, d), w_bf)


def _conv_mix_kernel(h_ref, w_ref, dww_ref, dwb_ref, g_ref, b_ref, y_ref, qm_ref, gbuf, cbuf, *, tm, c):
    n_groups = c // LANES

    @pl.when(pl.program_id(1) == 0)
    def _():
        gbuf[:, 0:CONV_HALO, :] = jnp.zeros((n_groups, CONV_HALO, LANES), F32)

    @pl.when(pl.program_id(1) > 0)
    def _():
        gbuf[:, 0:CONV_HALO, :] = gbuf[:, tm:tm + CONV_HALO, :]

    u = jnp.dot(h_ref[...].astype(BF16), w_ref[...], preferred_element_type=F32)
    glu = u[:, :c] * _sigmoid(u[:, c:2 * c])
    for grp in range(n_groups):
        gbuf[grp, CONV_HALO:, :] = glu[:, grp * LANES:(grp + 1) * LANES]
    qm_ref[...] = u[:, 2 * c:].astype(BF16)

    first_tap = CONV_HALO - (CONV_WIDTH - 1)

    def rows(i, carry):
        r0 = pl.multiple_of(i * CONV_ROWS, CONV_ROWS)
        for grp in range(n_groups):
            src = gbuf.at[grp, pl.ds(r0, CONV_ROWS + CONV_HALO), :]
            dst = cbuf.at[grp, pl.ds(r0, CONV_ROWS), :]
            lanes = slice(grp * LANES, (grp + 1) * LANES)
            taps = [jnp.broadcast_to(dww_ref[j:j + 1, lanes], (8, LANES)) for j in range(CONV_WIDTH)]
            bias = jnp.broadcast_to(dwb_ref[:, lanes], (8, LANES))
            for r in range(CONV_STRIDE):
                acc = bias
                for j in range(CONV_WIDTH):
                    acc = acc + src[pl.ds(first_tap + j + r, 8, stride=CONV_STRIDE), :] * taps[j]
                dst[pl.ds(r, 8, stride=CONV_STRIDE), :] = acc
        conv = jnp.concatenate([cbuf[grp, pl.ds(r0, CONV_ROWS), :] for grp in range(n_groups)], axis=-1)
        yn = _layer_norm(conv, g_ref[...], b_ref[...])
        y_ref[pl.ds(r0, CONV_ROWS), :] = (yn * _sigmoid(yn)).astype(BF16)
        return carry

    lax.fori_loop(0, tm // CONV_ROWS, rows, 0)


def _conv_mix(h, w_bf, dw_w, dw_b, ln_g, ln_b):
    bsz, s, d = h.shape
    c = dw_w.shape[1]
    tm = _tile(s, TOKEN_TILE)
    n_out = w_bf.shape[1]
    row = lambda a: a.reshape(1, -1)
    return pl.pallas_call(
        functools.partial(_conv_mix_kernel, tm=tm, c=c),
        grid=(bsz, s // tm),
        in_specs=[
            pl.BlockSpec((None, tm, d), lambda b, i: (b, i, 0)),
            pl.BlockSpec((d, n_out), lambda b, i: (0, 0)),
            pl.BlockSpec((CONV_WIDTH, c), lambda b, i: (0, 0)),
            pl.BlockSpec((1, c), lambda b, i: (0, 0)),
            pl.BlockSpec((1, c), lambda b, i: (0, 0)),
            pl.BlockSpec((1, c), lambda b, i: (0, 0)),
        ],
        out_specs=[
            pl.BlockSpec((None, tm, c), lambda b, i: (b, i, 0)),
            pl.BlockSpec((None, tm, MEM_WIDTH), lambda b, i: (b, i, 0)),
        ],
        out_shape=[
            jax.ShapeDtypeStruct((bsz, s, c), BF16),
            jax.ShapeDtypeStruct((bsz, s, MEM_WIDTH), BF16),
        ],
        scratch_shapes=[pltpu.VMEM((c // LANES, tm + CONV_HALO, LANES), F32),
                        pltpu.VMEM((c // LANES, tm, LANES), F32)],
        compiler_params=pltpu.CompilerParams(
            dimension_semantics=("arbitrary", "arbitrary"), vmem_limit_bytes=VMEM_LIMIT),
        name="conv_mix",
    )(h, w_bf, dw_w, row(dw_b), row(ln_g), row(ln_b))


def _sb_proj_kernel(h_ref, w_ref, u_ref, *, c):
    u = jnp.dot(h_ref[...].astype(BF16), w_ref[...], preferred_element_type=F32)
    u_ref[:, :c] = (u[:, :c] * (HEAD_DIM ** -0.5)).astype(BF16)
    u_ref[:, c:] = u[:, c:].astype(BF16)


def _sb_proj(h, w_bf, c):
    bsz, s, d = h.shape
    tm = _tile(s, TOKEN_TILE)
    n_out = w_bf.shape[1]
    return pl.pallas_call(
        functools.partial(_sb_proj_kernel, c=c),
        grid=(bsz, s // tm),
        in_specs=[
            pl.BlockSpec((None, tm, d), lambda b, i: (b, i, 0)),
            pl.BlockSpec((d, n_out), lambda b, i: (0, 0)),
        ],
        out_specs=pl.BlockSpec((None, tm, n_out), lambda b, i: (b, i, 0)),
        out_shape=jax.ShapeDtypeStruct((bsz, s, n_out), BF16),
        compiler_params=pltpu.CompilerParams(
            dimension_semantics=("arbitrary", "arbitrary"), vmem_limit_bytes=VMEM_LIMIT),
        name="sb_proj",
    )(h, w_bf)


def _sb_attn_kernel(q_ref, k_ref, v_ref, o_ref, carry_ref, acc_ref, *, tq, tk, n_pairs):
    qi = pl.program_id(2)
    n_heads = 2 * n_pairs
    lane = lax.broadcasted_iota(I32, (tq, LANES), 1)
    k_lane = lax.broadcasted_iota(I32, (tk, LANES), 1)
    q_heads = []
    for pr in range(n_pairs):
        q2 = q_ref[:, pr * LANES:(pr + 1) * LANES]
        q_heads += [jnp.where(lane < HEAD_DIM, q2, jnp.zeros_like(q2)),
                    jnp.where(lane >= HEAD_DIM, q2, jnp.zeros_like(q2))]
    later = (lax.broadcasted_iota(I32, (tk, tk), 0) > lax.broadcasted_iota(I32, (tk, tk), 1)).astype(BF16)

    def block(kb, masked, r0, r1):
        n = r1 - r0
        start = pl.multiple_of(kb * tk, tk)
        if masked:
            causal = ((kb * tk + lax.broadcasted_iota(I32, (n, tk), 1))
                      < (qi * tq + r0 + lax.broadcasted_iota(I32, (n, tk), 0)))
        log_nots, log_betas, parts = [], [], []
        for pr in range(n_pairs):
            k2 = k_ref[pl.ds(start, tk), pr * LANES:(pr + 1) * LANES]
            z2 = lax.dot_general(jnp.concatenate([q[r0:r1] for q in q_heads[2 * pr:2 * pr + 2]], axis=0), k2,
                                 (((1,), (1,)), ((), ())), preferred_element_type=F32)
            for half in range(2):
                z = z2[half * n:(half + 1) * n] * LOG2_E
                sp = jnp.maximum(z, 0.0) + jnp.log2(1.0 + jnp.exp2(-jnp.abs(z)))
                log_not = -sp
                if masked:
                    log_not = jnp.where(causal, log_not, 0.0)
                hi = log_not.astype(BF16)
                parts += [hi, (log_not - hi.astype(F32)).astype(BF16)]
                log_nots.append(log_not)
                log_betas.append(z - sp)
        sums = jnp.dot(jnp.concatenate(parts, axis=0), later, preferred_element_type=F32)
        for pr in range(n_pairs):
            v2 = v_ref[pl.ds(start, tk), pr * LANES:(pr + 1) * LANES]
            ws = []
            for half in range(2):
                hd = 2 * pr + half
                carry = carry_ref[hd, r0:r1, :]
                tail = sums[2 * hd * n:(2 * hd + 1) * n] + sums[(2 * hd + 1) * n:(2 * hd + 2) * n] + carry
                w = jnp.exp2(log_betas[hd] + tail)
                if masked:
                    w = jnp.where(causal, w, 0.0)
                ws.append(w.astype(BF16))
                carry_ref[hd, r0:r1, :] = carry + jnp.sum(log_nots[hd], axis=1, keepdims=True)
            v_rows = jnp.concatenate([jnp.where(k_lane < HEAD_DIM, v2, jnp.zeros_like(v2)),
                                      jnp.where(k_lane >= HEAD_DIM, v2, jnp.zeros_like(v2))], axis=0)
            acc_ref[pr, r0:r1, :] += jnp.dot(jnp.concatenate(ws, axis=1), v_rows, preferred_element_type=F32)

    carry_ref[...] = jnp.zeros_like(carry_ref)
    acc_ref[...] = jnp.zeros_like(acc_ref)
    n_diag = tq // tk
    for j in range(n_diag - 1, -1, -1):
        block(qi * n_diag + j, True, j * tk, tq)

    row_count = 1 + lax.broadcasted_iota(I32, (tq, tk), 0)

    def live_rows():
        top = carry_ref[0]
        for hd in range(1, n_heads):
            top = jnp.maximum(top, carry_ref[hd])
        return jnp.max(jnp.where(top > SB_SKIP_BELOW * LOG2_E, row_count, 0))

    def cond(state):
        kb, n_live = state
        return jnp.logical_and(kb >= 0, n_live > 0)

    def body(state):
        kb, n_live = state

        bounds = [0] + [r for r in SB_ROW_PREFIXES if r < tq] + [tq]
        for lo, hi in zip(bounds[:-1], bounds[1:]):
            @pl.when(jnp.logical_and(n_live > lo, n_live <= hi))
            def _():
                block(kb, False, 0, hi)

        return kb - 1, live_rows()

    lax.while_loop(cond, body, (qi * n_diag - 1, live_rows()))
    o_ref[...] = jnp.concatenate([acc_ref[pr] for pr in range(n_pairs)], axis=1).astype(BF16)


def _sb_attn(u, c):
    bsz, s, _ = u.shape
    tq = _tile(s, SB_QUERY_ROWS)
    tk = _tile(tq, SB_KEY_ROWS)
    width = SB_PAIRS_PER_STEP * LANES
    n_groups = c // width
    return pl.pallas_call(
        functools.partial(_sb_attn_kernel, tq=tq, tk=tk, n_pairs=SB_PAIRS_PER_STEP),
        grid=(bsz, n_groups, s // tq),
        in_specs=[
            pl.BlockSpec((None, tq, width), lambda b, p, i: (b, i, p)),
            pl.BlockSpec((None, s, width), lambda b, p, i: (b, 0, n_groups + p), pipeline_mode=pl.Buffered(1)),
            pl.BlockSpec((None, s, width), lambda b, p, i: (b, 0, 2 * n_groups + p), pipeline_mode=pl.Buffered(1)),
        ],
        out_specs=pl.BlockSpec((None, tq, width), lambda b, p, i: (b, i, p)),
        out_shape=jax.ShapeDtypeStruct((bsz, s, c), BF16),
        scratch_shapes=[pltpu.VMEM((2 * SB_PAIRS_PER_STEP, tq, tk), F32),
                        pltpu.VMEM((SB_PAIRS_PER_STEP, tq, LANES), F32)],
        compiler_params=pltpu.CompilerParams(
            dimension_semantics=("arbitrary", "arbitrary", "arbitrary"), vmem_limit_bytes=VMEM_LIMIT),
        name="sb_attn",
    )(u, u, u)


def _mix_out_kernel(h_ref, y_ref, qm_ref, kx_ref, vx_ref, wo_ref, g_ref, b_ref, wr_ref, br_ref,
                    h1_ref, idx_ref, gate_ref, rank_ref, cnt_ref, *, tm, c, n_mem, alpha):
    def rows(i, carry):
        sl = pl.ds(pl.multiple_of(i * MIX_ROWS, MIX_ROWS), MIX_ROWS)
        s = jnp.dot(qm_ref[sl, :], kx_ref[...], preferred_element_type=F32)
        probs = []
        for hd in range(MEM_HEADS):
            sh = s[:, hd * n_mem:(hd + 1) * n_mem]
            e = jnp.exp(sh - jnp.max(sh, axis=-1, keepdims=True))
            probs.append((e / jnp.sum(e, axis=-1, keepdims=True)).astype(BF16))
        y_mem = jnp.dot(jnp.concatenate(probs, axis=-1), vx_ref[...], preferred_element_type=F32)
        y = (jnp.dot(y_ref[sl, :], wo_ref[0:c, :], preferred_element_type=F32)
             + jnp.dot(y_mem.astype(BF16), wo_ref[c:, :], preferred_element_type=F32))
        h1_ref[sl, :] = _layer_norm(alpha * h_ref[sl, :] + y, g_ref[...], b_ref[...])
        return carry

    lax.fori_loop(0, tm // MIX_ROWS, rows, 0)
    h1 = h1_ref[...]

    logits = lax.dot_general(wr_ref[...], h1.astype(BF16), (((1,), (1,)), ((), ())),
                             preferred_element_type=F32) + br_ref[...]
    e_iota = lax.broadcasted_iota(I32, (N_EXPERTS, tm), 0)
    vals, sels = [], []
    for _ in range(TOP_K):
        m = jnp.max(logits, axis=0, keepdims=True)
        sel = jnp.min(jnp.where(logits == m, e_iota, N_EXPERTS), axis=0, keepdims=True)
        vals.append(m)
        sels.append(sel)
        logits = jnp.where(e_iota == sel, -jnp.inf, logits)
    exps = [jnp.exp(v - vals[0]) for v in vals]
    denom = exps[0] + exps[1] + exps[2] + exps[3]
    gate_ref[...] = jnp.concatenate([e / denom for e in exps], axis=0)
    idx_ref[...] = jnp.concatenate(sels, axis=0)

    onehot = jnp.zeros((N_EXPERTS, tm), F32)
    for sel in sels:
        onehot = onehot + (e_iota == sel).astype(F32)
    earlier = (lax.broadcasted_iota(I32, (tm, tm), 0) < lax.broadcasted_iota(I32, (tm, tm), 1)).astype(BF16)
    rank_e = jnp.dot(onehot.astype(BF16), earlier, preferred_element_type=F32)
    rank_ref[...] = jnp.concatenate(
        [jnp.sum(jnp.where(e_iota == sel, rank_e, 0.0), axis=0, keepdims=True) for sel in sels],
        axis=0).astype(I32)
    cnt_ref[...] = jnp.sum(onehot, axis=1, keepdims=True).astype(I32)


def _mix_out(h, y_mix, qm_src, qm_block, kx, vx, wo_bf, ln_g, ln_b, wr_t_bf, b_router, alpha):
    bsz, s, d = h.shape
    c = y_mix.shape[-1]
    n_mem = vx.shape[1] // MEM_HEADS
    tm = _tile(s, TOKEN_TILE)
    n_s = s // tm
    n_tok = bsz * s
    tok_spec = pl.BlockSpec((TOP_K, tm), lambda b, i: (0, b * n_s + i))
    return pl.pallas_call(
        functools.partial(_mix_out_kernel, tm=tm, c=c, n_mem=n_mem, alpha=alpha),
        grid=(bsz, n_s),
        in_specs=[
            pl.BlockSpec((None, tm, d), lambda b, i: (b, i, 0)),
            pl.BlockSpec((None, tm, c), lambda b, i: (b, i, 0)),
            pl.BlockSpec((None, tm, MEM_WIDTH), lambda b, i: (b, i, qm_block)),
            pl.BlockSpec((None, MEM_WIDTH, MEM_HEADS * n_mem), lambda b, i: (b, 0, 0)),
            pl.BlockSpec((None, MEM_HEADS * n_mem, MEM_WIDTH), lambda b, i: (b, 0, 0)),
            pl.BlockSpec((c + MEM_WIDTH, d), lambda b, i: (0, 0)),
            pl.BlockSpec((1, d), lambda b, i: (0, 0)),
            pl.BlockSpec((1, d), lambda b, i: (0, 0)),
            pl.BlockSpec((N_EXPERTS, d), lambda b, i: (0, 0)),
            pl.BlockSpec((N_EXPERTS, 1), lambda b, i: (0, 0)),
        ],
        out_specs=[
            pl.BlockSpec((None, tm, d), lambda b, i: (b, i, 0)),
            tok_spec, tok_spec, tok_spec,
            pl.BlockSpec((None, N_EXPERTS, 1), lambda b, i: (b * n_s + i, 0, 0)),
        ],
        out_shape=[
            jax.ShapeDtypeStruct((bsz, s, d), F32),
            jax.ShapeDtypeStruct((TOP_K, n_tok), I32),
            jax.ShapeDtypeStruct((TOP_K, n_tok), F32),
            jax.ShapeDtypeStruct((TOP_K, n_tok), I32),
            jax.ShapeDtypeStruct((bsz * n_s, N_EXPERTS, 1), I32),
        ],
        compiler_params=pltpu.CompilerParams(
            dimension_semantics=("arbitrary", "arbitrary"), vmem_limit_bytes=VMEM_LIMIT),
        name="mix_out",
    )(h, y_mix, qm_src, kx, vx, wo_bf, ln_g.reshape(1, d), ln_b.reshape(1, d), wr_t_bf,
      b_router.reshape(N_EXPERTS, 1))


def _copy_rows(src_ref, src_row, dst_ref, dst_row, count, sem, max_bit, wait, long_runs_rare=False):
    def copy(off, size, priority=0):
        cp = pltpu.make_async_copy(src_ref.at[pl.ds(pl.multiple_of(src_row + off, RUN_ALIGN), size)],
                                   dst_ref.at[pl.ds(pl.multiple_of(dst_row + off, RUN_ALIGN), size)], sem)
        if wait:
            cp.wait()
        else:
            cp.start(priority=priority)

    low_bit = RUN_ALIGN.bit_length() - 1
    if long_runs_rare and max_bit > COMMON_RUN_BIT:
        max_bit = COMMON_RUN_BIT
        chunk = 1 << (COMMON_RUN_BIT + 1)

        def chunks(i, carry):
            copy(i * chunk, chunk)
            return carry

        lax.fori_loop(0, lax.shift_right_logical(count, COMMON_RUN_BIT + 1), chunks, 0)
    for bit in range(max_bit, low_bit - 1, -1):
        off = lax.shift_left(lax.shift_right_logical(count, bit + 1), bit + 1)
        pl.when(jnp.bitwise_and(lax.shift_right_logical(count, bit), 1) == 1)(
            functools.partial(copy, off, 1 << bit, bit % 2))


def _tile_runs(tile, tc_ref, ls_ref, base_ref, local_ref, global_ref, sem, max_bit, to_global, wait):
    if wait:
        last = tile * N_EXPERTS + N_EXPERTS - 1
        total = ls_ref[last] + tc_ref[last]
        total_bit = local_ref.shape[0].bit_length() - 1
        if to_global:
            _copy_rows(local_ref, 0, global_ref, 0, total, sem, total_bit, True)
        else:
            _copy_rows(global_ref, 0, local_ref, 0, total, sem, total_bit, True)
        return

    def per_expert(e, carry):
        k = tile * N_EXPERTS + e
        if to_global:
            _copy_rows(local_ref, ls_ref[k], global_ref, base_ref[k], tc_ref[k], sem, max_bit, wait, True)
        else:
            _copy_rows(global_ref, base_ref[k], local_ref, ls_ref[k], tc_ref[k], sem, max_bit, wait, True)
        return carry

    lax.fori_loop(0, N_EXPERTS, per_expert, 0)


def _dispatch_kernel(tc_ref, ls_ref, base_ref, fill_lo_ref, fill_hi_ref, na_ref,
                     pos_ref, h_ref, xs_ref, buf, zero_ref, sems, fill_sem, *, tm, tmb):
    t = pl.program_id(0)
    n_t = pl.num_programs(0)
    slot = lax.rem(t, 2)
    max_bit = tm.bit_length() - 1

    def runs(tile, slot_, wait):
        _tile_runs(tile, tc_ref, ls_ref, base_ref, buf.at[slot_], xs_ref, sems.at[slot_], max_bit, True, wait)

    @pl.when(t >= 2)
    def _():
        runs(t - 2, slot, True)

    x = h_ref[...].astype(BF16)
    pos = pos_ref[...]

    pos16 = pos.astype(jnp.int16)
    row16 = lax.broadcasted_iota(I32, (SORT_ROWS, tm), 0).astype(jnp.int16)
    one = jnp.ones((SORT_ROWS, tm), BF16)

    def sort_rows(i, carry):
        r0 = pl.multiple_of(i * SORT_ROWS, SORT_ROWS)
        row = row16 + r0.astype(jnp.int16)
        perm = jnp.zeros((SORT_ROWS, tm), BF16)
        for k in range(TOP_K):
            perm = jnp.where(row == pos16[k:k + 1, :], one, perm)
        rows = jnp.dot(perm, x, preferred_element_type=F32)
        buf[slot, pl.ds(r0, SORT_ROWS), :] = _pack_halves(rows)
        return carry

    lax.fori_loop(0, _sorted_rows(tm) // SORT_ROWS, sort_rows, 0)
    runs(t, slot, False)

    def zero_fill(wait):
        pad_bit = tmb.bit_length() - 2

        def per_expert(e, carry):
            _copy_rows(zero_ref, 0, xs_ref, fill_lo_ref[e], fill_hi_ref[e] - fill_lo_ref[e],
                       fill_sem, pad_bit, wait)
            return carry

        def per_block(blk, carry):
            cp = pltpu.make_async_copy(zero_ref, xs_ref.at[pl.ds(blk * tmb, tmb)], fill_sem)
            if wait:
                cp.wait()
            else:
                cp.start()
            return carry

        lax.fori_loop(0, N_EXPERTS, per_expert, 0)
        lax.fori_loop(na_ref[0], xs_ref.shape[0] // tmb, per_block, 0)

    @pl.when(t == 0)
    def _():
        zero_ref[...] = jnp.zeros_like(zero_ref)
        zero_fill(False)

    @pl.when(t == n_t - 1)
    def _():
        @pl.when(t >= 1)
        def _():
            runs(t - 1, 1 - slot, True)
        runs(t, slot, True)
        zero_fill(True)


def _dispatch(h1_flat, pos, tables, fill_lo, fill_hi, n_active, n_rows, tm, tmb):
    n_tok, d = h1_flat.shape
    n_t = n_tok // tm
    tc, ls, base = tables
    return pl.pallas_call(
        functools.partial(_dispatch_kernel, tm=tm, tmb=tmb),
        grid_spec=pltpu.PrefetchScalarGridSpec(
            num_scalar_prefetch=6,
            grid=(n_t,),
            in_specs=[
                pl.BlockSpec((TOP_K, tm), lambda i, *_: (0, i)),
                pl.BlockSpec((tm, d), lambda i, *_: (i, 0)),
            ],
            out_specs=pl.BlockSpec(memory_space=pl.ANY),
            scratch_shapes=[
                pltpu.VMEM((2, _sorted_rows(tm), d // 2), I32),
                pltpu.VMEM((tmb, d // 2), I32),
                pltpu.SemaphoreType.DMA((2,)),
                pltpu.SemaphoreType.DMA,
            ],
        ),
        out_shape=jax.ShapeDtypeStruct((n_rows, d // 2), I32),
        compiler_params=pltpu.CompilerParams(
            dimension_semantics=("arbitrary",), vmem_limit_bytes=VMEM_LIMIT),
        name="moe_dispatch",
    )(tc, ls, base, fill_lo, fill_hi, n_active, pos, h1_flat)


def _expert_kernel(ord_ref, seq_ref, nseq_ref, na_ref, valid_ref, xs_ref, wgu_hbm, bgu_ref, wd_hbm, bd_ref, ys_ref,
                   wgu_f32, wd_f32, wgu_bf, wd_bf, sems, *, f):
    i = pl.program_id(0)
    n_active = na_ref[0]
    j = jnp.minimum(i, n_active - 1)
    prev = jnp.maximum(j - 1, 0)
    o = ord_ref[j]
    new_expert = jnp.logical_or(i == 0, o != ord_ref[prev])

    def fetch(o_, slot, wait):
        e = seq_ref[o_]
        copies = []
        for src, dst, parts in ((wgu_hbm, wgu_f32, WEIGHT_DMA_PARTS), (wd_hbm, wd_f32, WEIGHT_DMA_PARTS // 2)):
            rows = src.shape[1] // parts
            for p in range(parts):
                copies.append(pltpu.make_async_copy(src.at[e, pl.ds(p * rows, rows)],
                                                    dst.at[slot, pl.ds(p * rows, rows)], sems.at[slot]))
        for cp in copies:
            if wait:
                cp.wait()
            else:
                cp.start(priority=1)

    @pl.when(i == 0)
    def _():
        fetch(0, 0, False)

    @pl.when(jnp.logical_and(i < n_active, new_expert))
    def _():
        slot = lax.rem(o, 2)
        fetch(o, slot, True)

        @pl.when(o + 1 < nseq_ref[0])
        def _():
            fetch(o + 1, 1 - slot, False)

        rows = 128
        for r in range(0, wgu_bf.shape[0], rows):
            wgu_bf[r:r + rows, :] = wgu_f32[slot, r:r + rows, :].astype(BF16)
        for r in range(0, wd_bf.shape[0], rows):
            wd_bf[r:r + rows, :] = wd_f32[slot, r:r + rows, :].astype(BF16)

    valid = valid_ref[i]

    def rows_pass(r, n):
        x = _unpack_halves(xs_ref[r:r + n, :])
        gu = jnp.dot(x, wgu_bf[...], preferred_element_type=F32) + bgu_ref[...]
        gate = jnp.minimum(gu[:, :f], SWIGLU_LIMIT)
        up = jnp.clip(gu[:, f:], -SWIGLU_LIMIT, SWIGLU_LIMIT)
        act = (up + 1.0) * gate * _sigmoid(SWIGLU_ALPHA * gate)
        y = jnp.dot(act.astype(BF16), wd_bf[...], preferred_element_type=F32) + bd_ref[...]
        ys_ref[r:r + n, :] = _pack_halves(y.astype(BF16).astype(F32))

    def zero_rows(r, n):
        ys_ref[r:r + n, :] = jnp.zeros((n, ys_ref.shape[1]), I32)

    n_rows = xs_ref.shape[0]

    @pl.when(valid > n_rows - EXPERT_SUB)
    def _():
        rows_pass(0, n_rows)

    @pl.when(valid <= n_rows - EXPERT_SUB)
    def _():
        for r in range(0, n_rows, 2 * EXPERT_SUB):
            @pl.when(valid > r + EXPERT_SUB)
            def _():
                rows_pass(r, 2 * EXPERT_SUB)

            @pl.when(jnp.logical_and(valid > r, valid <= r + EXPERT_SUB))
            def _():
                rows_pass(r, EXPERT_SUB)
                zero_rows(r + EXPERT_SUB, EXPERT_SUB)

            @pl.when(valid <= r)
            def _():
                zero_rows(r, 2 * EXPERT_SUB)


def _experts(xs, block_ord, expert_seq, n_seq, n_active, block_valid, w_gate_up, b_gate_up, w_down, b_down, tmb):
    n_rows, half = xs.shape
    n_e, d, f2 = w_gate_up.shape
    f = f2 // 2
    nb = n_rows // tmb

    def blk(i, refs):
        return jnp.minimum(i, refs[3][0] - 1)

    def expert(i, refs):
        return refs[1][refs[0][blk(i, refs)]]

    return pl.pallas_call(
        functools.partial(_expert_kernel, f=f),
        grid_spec=pltpu.PrefetchScalarGridSpec(
            num_scalar_prefetch=5,
            grid=(nb,),
            in_specs=[
                pl.BlockSpec((tmb, half), lambda i, *refs: (blk(i, refs), 0)),
                pl.BlockSpec(memory_space=pl.ANY),
                pl.BlockSpec((None, 1, f2), lambda i, *refs: (expert(i, refs), 0, 0)),
                pl.BlockSpec(memory_space=pl.ANY),
                pl.BlockSpec((None, 1, d), lambda i, *refs: (expert(i, refs), 0, 0)),
            ],
            out_specs=pl.BlockSpec((tmb, half), lambda i, *refs: (i, 0)),
            scratch_shapes=[
                pltpu.VMEM((2, d, f2), F32), pltpu.VMEM((2, f, d), F32),
                pltpu.VMEM((d, f2), BF16), pltpu.VMEM((f, d), BF16),
                pltpu.SemaphoreType.DMA((2,)),
            ],
        ),
        out_shape=jax.ShapeDtypeStruct((n_rows, half), I32),
        compiler_params=pltpu.CompilerParams(
            dimension_semantics=("arbitrary",), vmem_limit_bytes=EXPERT_VMEM_LIMIT),
        name="moe_experts",
    )(block_ord, expert_seq, n_seq, n_active, block_valid, xs, w_gate_up, b_gate_up.reshape(n_e, 1, f2), w_down,
      b_down.reshape(n_e, 1, d))


def _combine_kernel(tc_ref, ls_ref, base_ref, pos_ref, gate_ref, h_ref, ys_ref, g_ref, b_ref, o_ref,
                    buf, sems, *, tm, alpha):
    t = pl.program_id(0)
    n_t = pl.num_programs(0)
    slot = lax.rem(t, 2)
    max_bit = tm.bit_length() - 1

    def runs(tile, slot_, wait):
        _tile_runs(tile, tc_ref, ls_ref, base_ref, buf.at[slot_], ys_ref, sems.at[slot_], max_bit, False, wait)

    @pl.when(t == 0)
    def _():
        buf[...] = jnp.zeros_like(buf)
        runs(t, slot, False)

    @pl.when(t + 1 < n_t)
    def _():
        runs(t + 1, 1 - slot, False)

    runs(t, slot, True)

    pos = pos_ref[...]
    gates = gate_ref[...]
    col = lax.broadcasted_iota(I32, (tm, _sorted_rows(tm)), 1)
    weights = jnp.zeros((tm, _sorted_rows(tm)), F32)
    for k in range(TOP_K):
        weights = jnp.where(col == pos[:, k:k + 1], gates[:, k:k + 1], weights)
    y = jnp.dot(weights.astype(BF16), _unpack_halves(buf[slot]), preferred_element_type=F32)
    o_ref[...] = _layer_norm(alpha * h_ref[...] + y, g_ref[...], b_ref[...])


def _combine(h1_flat, ys, pos_t, gates_t, tables, ln_g, ln_b, alpha, tm):
    n_tok, d = h1_flat.shape
    n_t = n_tok // tm
    tc, ls, base = tables
    return pl.pallas_call(
        functools.partial(_combine_kernel, tm=tm, alpha=alpha),
        grid_spec=pltpu.PrefetchScalarGridSpec(
            num_scalar_prefetch=3,
            grid=(n_t,),
            in_specs=[
                pl.BlockSpec((tm, TOP_K), lambda i, *_: (i, 0)),
                pl.BlockSpec((tm, TOP_K), lambda i, *_: (i, 0)),
                pl.BlockSpec((tm, d), lambda i, *_: (i, 0)),
                pl.BlockSpec(memory_space=pl.ANY),
                pl.BlockSpec((1, d), lambda i, *_: (0, 0)),
                pl.BlockSpec((1, d), lambda i, *_: (0, 0)),
            ],
            out_specs=pl.BlockSpec((tm, d), lambda i, *_: (i, 0)),
            scratch_shapes=[pltpu.VMEM((2, _sorted_rows(tm), d // 2), I32), pltpu.SemaphoreType.DMA((2,))],
        ),
        out_shape=jax.ShapeDtypeStruct((n_tok, d), F32),
        compiler_params=pltpu.CompilerParams(
            dimension_semantics=("arbitrary",), vmem_limit_bytes=VMEM_LIMIT),
        name="moe_combine",
    )(tc, ls, base, pos_t, gates_t, h1_flat, ys, ln_g.reshape(1, d), ln_b.reshape(1, d))


def _moe(h1, idx, gates, rank, tile_counts, layer, w_gate_up, b_gate_up, w_down, b_down, ln_g, ln_b, alpha):
    bsz, s, d = h1.shape
    n_tok = bsz * s
    tm = _tile(s, TOKEN_TILE)
    n_t = n_tok // tm
    tmb = EXPERT_ROWS
    n_rows = n_tok * TOP_K + n_t * N_EXPERTS * (RUN_ALIGN - 1) + N_EXPERTS * tmb
    n_rows = (n_rows + tmb - 1) // tmb * tmb
    nb = n_rows // tmb

    tile_counts = (tile_counts + RUN_ALIGN - 1) // RUN_ALIGN * RUN_ALIGN
    counts = jnp.sum(tile_counts, axis=0)
    padded = (counts + tmb - 1) // tmb * tmb
    pad_end = jnp.cumsum(padded).astype(I32)
    pad_start = pad_end - padded
    base = pad_start[None, :] + jnp.cumsum(tile_counts, axis=0) - tile_counts
    local = jnp.cumsum(tile_counts, axis=1) - tile_counts
    experts = jnp.arange(N_EXPERTS, dtype=I32)
    local_tok = jnp.repeat(local, tm, axis=0)
    pos = rank + jnp.sum(jnp.where(idx[:, :, None] == experts, local_tok[None], 0), axis=-1).astype(I32)
    block_start = jnp.arange(nb, dtype=I32) * tmb
    block_expert = jnp.minimum(
        jnp.sum((block_start[:, None] >= pad_end[None, :]).astype(I32), axis=1), N_EXPERTS - 1).astype(I32)
    n_active = (pad_end[-1:] // tmb).astype(I32)
    tables = (tile_counts.reshape(-1).astype(I32), local.reshape(-1).astype(I32), base.reshape(-1).astype(I32))

    h1_flat = h1.reshape(n_tok, d)
    xs = _dispatch(h1_flat, pos, tables, pad_start + counts, pad_end, n_active, n_rows, tm, tmb)
    merge = lambda w: w.reshape((w.shape[0] * w.shape[1],) + w.shape[2:])
    present = padded > 0
    expert_ord = jnp.cumsum(present.astype(I32)) - 1
    expert_seq = (jnp.argsort(jnp.logical_not(present), stable=True) + layer * N_EXPERTS).astype(I32)
    n_seq = jnp.sum(present.astype(I32)).reshape(1)
    block_valid = jnp.clip((pad_start + counts)[block_expert] - block_start, 0, tmb).astype(I32)
    ys = _experts(xs, expert_ord[block_expert].astype(I32), expert_seq, n_seq, n_active, block_valid,
                  merge(w_gate_up), merge(b_gate_up), merge(w_down), merge(b_down), tmb)
    out = _combine(h1_flat, ys, pos.T, gates.T, tables, ln_g, ln_b, alpha, tm)
    return out.reshape(bsz, s, d)


@jax.jit
def _forward(x, mem, mem_ln_g, mem_ln_b, w_mem_kv, w_in_conv, conv_dw_w, conv_dw_b, conv_ln_g,
             conv_ln_b, w_in_sb, w_mix_out, ln_mix_g, ln_mix_b, w_router, b_router, w_gate_up,
             b_gate_up, w_down, b_down, ln_moe_g, ln_moe_b):
    depth = w_mix_out.shape[0]
    alpha = float((2 * depth) ** 0.25)
    c = conv_dw_w.shape[-1]
    kx, vx = _mem_kv(mem, mem_ln_g, mem_ln_b, w_mem_kv.astype(BF16))

    h = x
    for i in range(depth):
        j = i // 2
        if i % 2 == 0:
            y_mix, qm = _conv_mix(h, w_in_conv[j].astype(BF16), conv_dw_w[j], conv_dw_b[j],
                                  conv_ln_g[j], conv_ln_b[j])
            qm_src, qm_block = qm, 0
        else:
            u = _sb_proj(h, w_in_sb[j].astype(BF16), c)
            y_mix = _sb_attn(u, c)
            qm_src, qm_block = u, (3 * c) // MEM_WIDTH
        h1, idx, gates, rank, cnt = _mix_out(
            h, y_mix, qm_src, qm_block, kx, vx, w_mix_out[i].astype(BF16), ln_mix_g[i], ln_mix_b[i],
            w_router[i].T.astype(BF16), b_router[i], alpha)
        h = _moe(h1, idx, gates, rank, cnt[:, :, 0], i, w_gate_up, b_gate_up, w_down, b_down,
                 ln_moe_g[i], ln_moe_b[i], alpha)
    return h


def kernel(x, mem, mem_ln_g, mem_ln_b, w_mem_kv, w_in_conv, conv_dw_w, conv_dw_b, conv_ln_g, conv_ln_b,
           w_in_sb, w_mix_out, ln_mix_g, ln_mix_b, w_router, b_router, w_gate_up, b_gate_up, w_down,
           b_down, ln_moe_g, ln_moe_b):
    return _forward(x, mem, mem_ln_g, mem_ln_b, w_mem_kv, w_in_conv, conv_dw_w, conv_dw_b, conv_ln_g,
                    conv_ln_b, w_in_sb, w_mix_out, ln_mix_g, ln_mix_b, w_router, b_router, w_gate_up,
                    b_gate_up, w_down, b_down, ln_moe_g, ln_moe_b)
```
